```python
import jax
import jax.numpy as jnp
from jax import lax
import numpy as np

D_MODEL = 2048
BATCH = 4
SEQ = 2048
DEPTH = 2
DEC_BATCH = 128
DEC_SEQ = 4
PAST_LEN = 8192
PAGE_SIZE = 128

N_MIXERS = 2
N_SWA_LAYERS = (DEPTH + 1) // 2
N_SB_LAYERS = DEPTH // 2
SWA_HEADS = 32
SWA_KV_HEADS = 4
SWA_HEAD_DIM = 64
SWA_GROUP = SWA_HEADS // SWA_KV_HEADS
WINDOW = 128
SWA_BLOCK = WINDOW
ROPE_THETA = 500000.0
ROPE_DIM = SWA_HEAD_DIM // 4
SWA_SCALE = SWA_HEAD_DIM ** -0.5
SB_HEADS = 16
SB_KV_HEADS = 4
SB_HEAD_DIM = 128
SB_GROUP = SB_HEADS // SB_KV_HEADS
SB_BLOCK = 128
SB_SCALE = SB_HEAD_DIM ** -0.5
SB_BIAS_INIT = -6.0
MEM_TOKENS = 256
MEM_HEADS = 4
MEM_HEAD_DIM = 128
MEM_SCALE = MEM_HEAD_DIM ** -0.5
N_GROUPS = 4
EXPERTS_PER_GROUP = 8
N_EXPERTS = N_GROUPS * EXPERTS_PER_GROUP
TOP_K = 2
EXPERT_FF = 512
RMS_EPS = 1e-6
NEG_INF = -1e30
POOL_NUM = 5
POOL_DEN = 4

kernel_name = 'hybrid_swa_sink_stickbreak_hmoe_step'


def rms_norm(x, g):
    x32 = x.astype(jnp.float32)
    y = x32 * lax.rsqrt(jnp.mean(x32 * x32, axis=-1, keepdims=True) + RMS_EPS)
    return (y * g.astype(jnp.float32)).astype(x.dtype)


def rope_partial(x, pos):
    half = ROPE_DIM // 2
    inv_freq = ROPE_THETA ** (-2.0 * jnp.arange(half, dtype=jnp.float32) / ROPE_DIM)
    ang = pos.astype(jnp.float32)[:, None] * inv_freq[None, :]
    cos = jnp.cos(ang)[:, None, :]
    sin = jnp.sin(ang)[:, None, :]
    xr = x[..., :ROPE_DIM].astype(jnp.float32)
    x1, x2 = xr[..., :half], xr[..., half:]
    rot = jnp.concatenate([x1 * cos - x2 * sin, x2 * cos + x1 * sin], axis=-1)
    return jnp.concatenate([rot.astype(x.dtype), x[..., ROPE_DIM:]], axis=-1)


def split_heads(qkv, n_heads, n_kv, hd):
    b, s, _ = qkv.shape
    nq, nk = n_heads * hd, n_kv * hd
    q = qkv[..., :nq].reshape(b, s, n_heads, hd)
    k = qkv[..., nq:nq + nk].reshape(b, s, n_kv, hd)
    v = qkv[..., nq + nk:].reshape(b, s, n_kv, hd)
    return q, k, v


def sink_softmax(logits, sinks):
    sink = sinks.astype(jnp.float32).reshape(SWA_KV_HEADS, SWA_GROUP, 1, 1)
    sink = jnp.broadcast_to(sink, logits.shape[:-1] + (1,))
    return jax.nn.softmax(jnp.concatenate([logits, sink], axis=-1), axis=-1)[..., :-1]


def swa_prompt(q, k, v, sinks):
    b, s, _, _ = q.shape
    nb = s // SWA_BLOCK
    qb = q.reshape(b, nb, SWA_BLOCK, SWA_KV_HEADS, SWA_GROUP, SWA_HEAD_DIM)
    kb = k.reshape(b, nb, SWA_BLOCK, SWA_KV_HEADS, SWA_HEAD_DIM)
    vb = v.reshape(b, nb, SWA_BLOCK, SWA_KV_HEADS, SWA_HEAD_DIM)
    kk = jnp.concatenate([jnp.concatenate([jnp.zeros_like(kb[:, :1]), kb[:, :-1]], axis=1), kb], axis=2)
    vv = jnp.concatenate([jnp.concatenate([jnp.zeros_like(vb[:, :1]), vb[:, :-1]], axis=1), vb], axis=2)
    logits = jnp.einsum('bnqkgd,bnskd->bnkgqs', qb, kk).astype(jnp.float32) * SWA_SCALE
    qi = jnp.arange(SWA_BLOCK)[:, None]
    sj = jnp.arange(2 * SWA_BLOCK)[None, :]
    blk = jnp.arange(nb)[:, None, None]
    rel = qi + SWA_BLOCK - sj
    valid = (rel >= 0) & (rel < WINDOW) & (blk * SWA_BLOCK + sj - SWA_BLOCK >= 0)
    logits = jnp.where(valid[None, :, None, None], logits, NEG_INF)
    p = sink_softmax(logits, sinks).astype(v.dtype)
    out = jnp.einsum('bnkgqs,bnskd->bnqkgd', p, vv)
    return out.reshape(b, s, SWA_HEADS * SWA_HEAD_DIM)


def swa_sample(q, k_new, v_new, k_buf, v_buf, sinks, past_len):
    b, n, _, _ = q.shape
    w = k_buf.shape[1]
    kk = jnp.concatenate([k_buf, k_new], axis=1)
    vv = jnp.concatenate([v_buf, v_new], axis=1)
    qg = q.reshape(b, n, SWA_KV_HEADS, SWA_GROUP, SWA_HEAD_DIM)
    logits = jnp.einsum('bqkgd,bskd->bkgqs', qg, kk).astype(jnp.float32) * SWA_SCALE
    qpos = past_len + jnp.arange(n)
    kpos = past_len - w + jnp.arange(w + n)
    rel = qpos[:, None] - kpos[None, :]
    valid = (rel >= 0) & (rel < WINDOW) & (kpos[None, :] >= 0)
    logits = jnp.where(valid, logits, NEG_INF)
    p = sink_softmax(logits, sinks).astype(vv.dtype)
    out = jnp.einsum('bkgqs,bskd->bqkgd', p, vv).reshape(b, n, SWA_HEADS * SWA_HEAD_DIM)
    return out, kk[:, -w:], vv[:, -w:]


def stick_breaking(q, k, v, bias, qpos, kpos):
    z = jnp.einsum('bqkgd,bskd->bkgqs', q, k).astype(jnp.float32) * SB_SCALE
    z = z + bias.astype(jnp.float32).reshape(SB_KV_HEADS, SB_GROUP, 1, 1)
    mask = kpos[None, :] < qpos[:, None]
    log_keep = jnp.where(mask, jax.nn.log_sigmoid(-z), 0.0)
    after = lax.cumsum(log_keep, axis=z.ndim - 1, reverse=True) - log_keep
    w = jnp.where(mask, jnp.exp(jax.nn.log_sigmoid(z) + after), 0.0)
    return jnp.einsum('bkgqs,bskd->bqkgd', w.astype(v.dtype), v)


def sb_prompt(q, k, v, bias):
    b, s, _, _ = q.shape
    nb = s // SB_BLOCK
    qb = jnp.moveaxis(q.reshape(b, nb, SB_BLOCK, SB_KV_HEADS, SB_GROUP, SB_HEAD_DIM), 1, 0)
    kpos = jnp.arange(s)

    def block(args):
        q_blk, idx = args
        return stick_breaking(q_blk, k, v, bias, idx * SB_BLOCK + jnp.arange(SB_BLOCK), kpos)

    out = lax.map(block, (qb, jnp.arange(nb)))
    return jnp.moveaxis(out, 0, 1).reshape(b, s, SB_HEADS * SB_HEAD_DIM)


def sb_sample(q, k_new, v_new, k_pool, v_pool, page_table, bias):
    b, n, _, _ = q.shape
    past_len = page_table.shape[1] * PAGE_SIZE
    qg = q.reshape(b, n, SB_KV_HEADS, SB_GROUP, SB_HEAD_DIM)
    qpos = past_len + jnp.arange(n)
    kpos = jnp.arange(past_len + n)

    def one(args):
        q_s, k_s, v_s, pages = args
        k_all = jnp.concatenate([k_pool[pages].reshape(past_len, SB_KV_HEADS, SB_HEAD_DIM), k_s], axis=0)
        v_all = jnp.concatenate([v_pool[pages].reshape(past_len, SB_KV_HEADS, SB_HEAD_DIM), v_s], axis=0)
        return stick_breaking(q_s[None], k_all[None], v_all[None], bias, qpos, kpos)[0]

    out = lax.map(one, (qg, k_new, v_new, page_table))
    return out.reshape(b, n, SB_HEADS * SB_HEAD_DIM)


def mem_project(mem, g, w_kv):
    b, m, _ = mem.shape
    kv = rms_norm(mem, g) @ w_kv
    k = kv[..., :MEM_HEADS * MEM_HEAD_DIM].reshape(b, m, MEM_HEADS, MEM_HEAD_DIM)
    v = kv[..., MEM_HEADS * MEM_HEAD_DIM:].reshape(b, m, MEM_HEADS, MEM_HEAD_DIM)
    return k, v


def mem_attend(h, k, v, w_q, w_o):
    b, s, _ = h.shape
    q = (h @ w_q).reshape(b, s, MEM_HEADS, MEM_HEAD_DIM)
    logits = jnp.einsum('bqhd,bmhd->bhqm', q, k).astype(jnp.float32) * MEM_SCALE
    p = jax.nn.softmax(logits, axis=-1).astype(v.dtype)
    o = jnp.einsum('bhqm,bmhd->bqhd', p, v).reshape(b, s, MEM_HEADS * MEM_HEAD_DIM)
    return o @ w_o


def hier_moe(h, w_group, b_group, w_router, b_router, w_gate, w_up, w_down):
    t = h.shape[0]
    g_logits = (h @ w_group).astype(jnp.float32) + b_group.astype(jnp.float32)
    g_prob = jax.nn.softmax(g_logits, axis=-1)
    g_sel = jax.nn.one_hot(jnp.argmax(g_logits, axis=-1), N_GROUPS, dtype=jnp.float32)
    p_group = jnp.sum(g_prob * g_sel, axis=-1, keepdims=True)
    e_logits = ((h @ w_router).astype(jnp.float32) + b_router.astype(jnp.float32)).reshape(t, N_GROUPS, EXPERTS_PER_GROUP)
    e_logits = jnp.einsum('tge,tg->te', e_logits, g_sel)
    top_w, top_i = lax.top_k(jax.nn.softmax(e_logits, axis=-1), TOP_K)
    top_w = top_w / jnp.sum(top_w, axis=-1, keepdims=True)
    local = jnp.einsum('tk,tke->te', top_w, jax.nn.one_hot(top_i, EXPERTS_PER_GROUP, dtype=jnp.float32)) * p_group
    combine = (g_sel[:, :, None] * local[:, None, :]).astype(h.dtype)
    y = jnp.zeros_like(h)
    for g in range(N_GROUPS):
        a = jnp.einsum('td,edf->tef', h, w_gate[g])
        u = jnp.einsum('td,edf->tef', h, w_up[g])
        act = jax.nn.silu(a) * u * combine[:, g, :, None]
        y = y + jnp.einsum('tef,efd->td', act, w_down[g])
    return y


def setup_inputs(seed: int = 0) -> dict:
    key = jax.random.key(seed)
    keys = iter(jax.random.split(key, 40))
    f32 = jnp.float32

    def normal(shape, scale):
        return jax.random.normal(next(keys), shape, f32) * scale

    def gain(shape):
        return 1.0 + normal(shape, 0.05)

    n_pages = PAST_LEN // PAGE_SIZE
    n_pool = (POOL_NUM * DEC_BATCH * n_pages + POOL_DEN - 1) // POOL_DEN
    win_buf = min(WINDOW, PAST_LEN)
    swa_in = (SWA_HEADS + 2 * SWA_KV_HEADS) * SWA_HEAD_DIM
    sb_in = (SB_HEADS + 2 * SB_KV_HEADS) * SB_HEAD_DIM
    mem_w = MEM_HEADS * MEM_HEAD_DIM
    inputs = {}
    inputs['x_prompt'] = normal((BATCH, SEQ, D_MODEL), 1.0)
    inputs['x_sample'] = normal((DEC_BATCH, DEC_SEQ, D_MODEL), 1.0)
    inputs['mem_prompt'] = normal((BATCH, MEM_TOKENS, D_MODEL), 1.0)
    inputs['cache_swa_k'] = normal((N_SWA_LAYERS, DEC_BATCH, win_buf, SWA_KV_HEADS, SWA_HEAD_DIM), 1.0)
    inputs['cache_swa_v'] = normal((N_SWA_LAYERS, DEC_BATCH, win_buf, SWA_KV_HEADS, SWA_HEAD_DIM), 1.0)
    inputs['cache_sb_k'] = normal((N_SB_LAYERS, n_pool, PAGE_SIZE, SB_KV_HEADS, SB_HEAD_DIM), 1.0)
    inputs['cache_sb_v'] = normal((N_SB_LAYERS, n_pool, PAGE_SIZE, SB_KV_HEADS, SB_HEAD_DIM), 1.0)
    inputs['cache_mem_k'] = normal((DEPTH, DEC_BATCH, MEM_TOKENS, MEM_HEADS, MEM_HEAD_DIM), 1.0)
    inputs['cache_mem_v'] = normal((DEPTH, DEC_BATCH, MEM_TOKENS, MEM_HEADS, MEM_HEAD_DIM), 1.0)
    perm = jax.random.permutation(next(keys), n_pool)[:DEC_BATCH * n_pages]
    inputs['page_table'] = perm.reshape(DEC_BATCH, n_pages).astype(jnp.int32)
    inputs['norm_mix'] = gain((DEPTH, D_MODEL))
    inputs['w_in_swa'] = normal((N_SWA_LAYERS, D_MODEL, swa_in), D_MODEL ** -0.5)
    inputs['sinks_swa'] = normal((N_SWA_LAYERS, SWA_HEADS), 0.5)
    inputs['w_out_swa'] = normal((N_SWA_LAYERS, SWA_HEADS * SWA_HEAD_DIM, D_MODEL), (SWA_HEADS * SWA_HEAD_DIM) ** -0.5)
    inputs['w_in_sb'] = normal((N_SB_LAYERS, D_MODEL, sb_in), D_MODEL ** -0.5)
    inputs['sb_bias'] = SB_BIAS_INIT + normal((N_SB_LAYERS, SB_HEADS), 0.1)
    inputs['w_out_sb'] = normal((N_SB_LAYERS, SB_HEADS * SB_HEAD_DIM, D_MODEL), (SB_HEADS * SB_HEAD_DIM) ** -0.5)
    inputs['norm_mem_q'] = gain((DEPTH, D_MODEL))
    inputs['norm_mem_kv'] = gain((DEPTH, D_MODEL))
    inputs['w_mem_q'] = normal((DEPTH, D_MODEL, mem_w), D_MODEL ** -0.5)
    inputs['w_mem_kv'] = normal((DEPTH, D_MODEL, 2 * mem_w), D_MODEL ** -0.5)
    inputs['w_mem_o'] = normal((DEPTH, mem_w, D_MODEL), mem_w ** -0.5)
    inputs['norm_ffn'] = gain((DEPTH, D_MODEL))
    inputs['w_group'] = normal((DEPTH, D_MODEL, N_GROUPS), D_MODEL ** -0.5)
    inputs['b_group'] = normal((DEPTH, N_GROUPS), 0.01)
    inputs['w_router'] = normal((DEPTH, D_MODEL, N_EXPERTS), D_MODEL ** -0.5)
    inputs['b_router'] = normal((DEPTH, N_EXPERTS), 0.01)
    inputs['w_gate'] = normal((DEPTH, N_GROUPS, EXPERTS_PER_GROUP, D_MODEL, EXPERT_FF), D_MODEL ** -0.5)
    inputs['w_up'] = normal((DEPTH, N_GROUPS, EXPERTS_PER_GROUP, D_MODEL, EXPERT_FF), D_MODEL ** -0.5)
    inputs['w_down'] = normal((DEPTH, N_GROUPS, EXPERTS_PER_GROUP, EXPERT_FF, D_MODEL), EXPERT_FF ** -0.5)
    inputs['norm_final'] = gain((D_MODEL,))
    return inputs


def reference(x_prompt, x_sample, mem_prompt, cache_swa_k, cache_swa_v, cache_sb_k, cache_sb_v,
              cache_mem_k, cache_mem_v, page_table, norm_mix, w_in_swa, sinks_swa, w_out_swa,
              w_in_sb, sb_bias, w_out_sb, norm_mem_q, norm_mem_kv, w_mem_q, w_mem_kv, w_mem_o, norm_ffn,
              w_group, b_group, w_router, b_router, w_gate, w_up, w_down, norm_final):
    b_p, s_p, d = x_prompt.shape
    b_s, n_s, _ = x_sample.shape
    past_len = page_table.shape[1] * PAGE_SIZE
    pos_p = jnp.arange(s_p)
    pos_s = past_len + jnp.arange(n_s)
    xp, xs = x_prompt, x_sample
    swa_kp, swa_vp, swa_ks, swa_vs = [], [], [], []
    sb_kp, sb_vp, sb_ks, sb_vs = [], [], [], []
    mem_kp, mem_vp = [], []
    for i in range(DEPTH):
        j = i // N_MIXERS
        hp = rms_norm(xp, norm_mix[i])
        hs = rms_norm(xs, norm_mix[i])
        if i % N_MIXERS == 0:
            qp, kp, vp = split_heads(hp @ w_in_swa[j], SWA_HEADS, SWA_KV_HEADS, SWA_HEAD_DIM)
            qs, ks, vs = split_heads(hs @ w_in_swa[j], SWA_HEADS, SWA_KV_HEADS, SWA_HEAD_DIM)
            qp, kp = rope_partial(qp, pos_p), rope_partial(kp, pos_p)
            qs, ks = rope_partial(qs, pos_s), rope_partial(ks, pos_s)
            op = swa_prompt(qp, kp, vp, sinks_swa[j])
            os_, kbuf, vbuf = swa_sample(qs, ks, vs, cache_swa_k[j], cache_swa_v[j], sinks_swa[j], past_len)
            w_keep = min(WINDOW, s_p)
            swa_kp.append(kp[:, -w_keep:])
            swa_vp.append(vp[:, -w_keep:])
            swa_ks.append(kbuf)
            swa_vs.append(vbuf)
            w_out = w_out_swa[j]
        else:
            qp, kp, vp = split_heads(hp @ w_in_sb[j], SB_HEADS, SB_KV_HEADS, SB_HEAD_DIM)
            qs, ks, vs = split_heads(hs @ w_in_sb[j], SB_HEADS, SB_KV_HEADS, SB_HEAD_DIM)
            op = sb_prompt(qp, kp, vp, sb_bias[j])
            os_ = sb_sample(qs, ks, vs, cache_sb_k[j], cache_sb_v[j], page_table, sb_bias[j])
            sb_kp.append(kp)
            sb_vp.append(vp)
            sb_ks.append(ks)
            sb_vs.append(vs)
            w_out = w_out_sb[j]
        xp = xp + op @ w_out
        xs = xs + os_ @ w_out
        mk, mv = mem_project(mem_prompt, norm_mem_kv[i], w_mem_kv[i])
        mem_kp.append(mk)
        mem_vp.append(mv)
        xp = xp + mem_attend(rms_norm(xp, norm_mem_q[i]), mk, mv, w_mem_q[i], w_mem_o[i])
        xs = xs + mem_attend(rms_norm(xs, norm_mem_q[i]), cache_mem_k[i], cache_mem_v[i], w_mem_q[i], w_mem_o[i])
        hp = rms_norm(xp, norm_ffn[i]).reshape(b_p * s_p, d)
        hs = rms_norm(xs, norm_ffn[i]).reshape(b_s * n_s, d)
        xp = xp + hier_moe(hp, w_group[i], b_group[i], w_router[i], b_router[i], w_gate[i], w_up[i], w_down[i]).reshape(b_p, s_p, d)
        xs = xs + hier_moe(hs, w_group[i], b_group[i], w_router[i], b_router[i], w_gate[i], w_up[i], w_down[i]).reshape(b_s, n_s, d)
    y_prompt = rms_norm(xp, norm_final)
    y_sample = rms_norm(xs, norm_final)
    return (y_prompt, y_sample,
            jnp.stack(swa_kp), jnp.stack(swa_vp), jnp.stack(swa_ks), jnp.stack(swa_vs),
            jnp.stack(sb_kp), jnp.stack(sb_vp), jnp.stack(sb_ks), jnp.stack(sb_vs),
            jnp.stack(mem_kp), jnp.stack(mem_vp))
```

```python
import functools

import jax
import jax.numpy as jnp
from jax import lax
from jax.experimental import pallas as pl
from jax.experimental.pallas import tpu as pltpu

F32 = jnp.float32
BF16 = jnp.bfloat16

SWA_HEADS, SWA_KV_HEADS, SWA_HEAD_DIM = 32, 4, 64
SWA_GROUP = SWA_HEADS // SWA_KV_HEADS
WINDOW = 128
ROPE_THETA = 500000.0
ROPE_DIM = SWA_HEAD_DIM // 4
SWA_SCALE = SWA_HEAD_DIM ** -0.5
SB_HEADS, SB_KV_HEADS, SB_HEAD_DIM = 16, 4, 128
SB_GROUP = SB_HEADS // SB_KV_HEADS
SB_BLOCK = 128
SB_SCALE = SB_HEAD_DIM ** -0.5
MEM_HEADS, MEM_HEAD_DIM = 4, 128
MEM_SCALE = MEM_HEAD_DIM ** -0.5
N_GROUPS, EXPERTS_PER_GROUP = 4, 8
N_EXPERTS = N_GROUPS * EXPERTS_PER_GROUP
RMS_EPS = 1e-6
NEG_INF = -1e30

LANES = 128
VMEM_LIMIT = 56 * 1024 * 1024
MOE_TILE = 256


def _cparams(sem):
    return pltpu.CompilerParams(dimension_semantics=sem, vmem_limit_bytes=VMEM_LIMIT)


def _pick(n, prefs):
    for p in prefs:
        if n % p == 0:
            return p
    return n


def _dot_t(a, b):
    return lax.dot_general(a, b, (((1,), (1,)), ((), ())), preferred_element_type=F32)


def _dot(a, b):
    return jnp.dot(a, b, preferred_element_type=F32)


def _norm_matmul_kernel(*refs, add, emit_x, rope_cols, tn):
    it = iter(refs)
    x_ref = next(it)
    ya_ref = next(it) if add else None
    yb_ref = next(it) if add else None
    g_ref = next(it)
    w_ref = next(it)
    if rope_cols:
        cos_ref, sp_ref, sm_ref = next(it), next(it), next(it)
    o_ref = next(it)
    xo_ref = next(it) if emit_x else None
    h_scr = next(it)
    j = pl.program_id(1)

    @pl.when(j == 0)
    def _():
        xv = x_ref[...]
        if add:
            xv = xv + ya_ref[...] + yb_ref[...]
        if emit_x:
            xo_ref[...] = xv
        ms = jnp.mean(xv * xv, axis=-1, keepdims=True)
        h_scr[...] = (xv * lax.rsqrt(ms + RMS_EPS) * g_ref[...]).astype(BF16)

    acc = _dot(h_scr[...], w_ref[...].astype(BF16))
    if not rope_cols:
        o_ref[...] = acc.astype(o_ref.dtype)
        return

    @pl.when(j * tn < rope_cols)
    def _():
        reps = tn // LANES
        cos = jnp.concatenate([cos_ref[...]] * reps, axis=1)
        s_plus = jnp.concatenate([sp_ref[...]] * reps, axis=1)
        s_minus = jnp.concatenate([sm_ref[...]] * reps, axis=1)
        half = ROPE_DIM // 2
        roped = (acc * cos + pltpu.roll(acc, half, axis=1) * s_plus
                 + pltpu.roll(acc, tn - half, axis=1) * s_minus)
        col = j * tn + lax.broadcasted_iota(jnp.int32, acc.shape, 1)
        o_ref[...] = jnp.where(col < rope_cols, roped, acc).astype(o_ref.dtype)

    @pl.when(j * tn >= rope_cols)
    def _():
        o_ref[...] = acc.astype(o_ref.dtype)


def _norm_matmul(x, g, w, *, n_rows=None, y2=None, rope=None, rope_cols=0, emit_x=False,
                 out_dtype=F32):
    t = n_rows if n_rows is not None else x.shape[0]
    d = x.shape[1]
    n = w.shape[1]
    add = y2 is not None
    tm = _pick(t, (256,) if add else (512, 256))
    tn = _pick(n, (512, 256, 128))
    n_i = t // tm
    in_specs = [pl.BlockSpec((tm, d), lambda i, j: (i, 0))]
    args = [x]
    if add:
        in_specs += [pl.BlockSpec((tm, d), lambda i, j: (i, 0)),
                     pl.BlockSpec((tm, d), lambda i, j: (i + n_i, 0))]
        args += [y2, y2]
    in_specs += [pl.BlockSpec((1, d), lambda i, j: (0, 0)),
                 pl.BlockSpec((d, tn), lambda i, j: (0, j))]
    args += [g.reshape(1, d), w]
    if rope_cols:
        in_specs += [pl.BlockSpec((tm, LANES), lambda i, j: (i, 0))] * 3
        args += list(rope)
    out_shape = [jax.ShapeDtypeStruct((t, n), out_dtype)]
    out_specs = [pl.BlockSpec((tm, tn), lambda i, j: (i, j))]
    if emit_x:
        out_shape.append(jax.ShapeDtypeStruct((t, d), F32))
        out_specs.append(pl.BlockSpec((tm, d), lambda i, j: (i, 0)))
    res = pl.pallas_call(
        functools.partial(_norm_matmul_kernel, add=add, emit_x=emit_x, rope_cols=rope_cols, tn=tn),
        grid=(n_i, n // tn),
        in_specs=in_specs,
        out_specs=out_specs,
        out_shape=out_shape,
        scratch_shapes=[pltpu.VMEM((tm, d), BF16)],
        compiler_params=_cparams(("parallel", "arbitrary")),
    )(*args)
    return res if emit_x else res[0]


def _resid_matmul_kernel(a_ref, w_ref, r_ref, o_ref):
    o_ref[...] = r_ref[...] + _dot(a_ref[...].astype(BF16), w_ref[...].astype(BF16))


def _resid_matmul(a, w, resid):
    t, k = a.shape
    n = w.shape[1]
    tm = _pick(t, (512, 256))
    tn = _pick(n, (512, 256, 128))
    return pl.pallas_call(
        _resid_matmul_kernel,
        grid=(t // tm, n // tn),
        in_specs=[pl.BlockSpec((tm, k), lambda i, j: (i, 0)),
                  pl.BlockSpec((k, tn), lambda i, j: (0, j)),
                  pl.BlockSpec((tm, tn), lambda i, j: (i, j))],
        out_specs=pl.BlockSpec((tm, tn), lambda i, j: (i, j)),
        out_shape=jax.ShapeDtypeStruct((t, n), F32),
        compiler_params=_cparams(("parallel", "arbitrary")),
    )(a, w, resid)


def _final_norm_kernel(x_ref, ya_ref, yb_ref, g_ref, o_ref):
    xv = x_ref[...] + ya_ref[...] + yb_ref[...]
    ms = jnp.mean(xv * xv, axis=-1, keepdims=True)
    o_ref[...] = xv * lax.rsqrt(ms + RMS_EPS) * g_ref[...]


def _final_norm(x, y2, g, row0, n_rows):
    t, d = x.shape
    tm = _pick(n_rows, (256, 128, 8))
    assert row0 % tm == 0 and t % tm == 0
    b0, bt = row0 // tm, t // tm
    return pl.pallas_call(
        _final_norm_kernel,
        grid=(n_rows // tm,),
        in_specs=[pl.BlockSpec((tm, d), lambda i: (i + b0, 0)),
                  pl.BlockSpec((tm, d), lambda i: (i + b0, 0)),
                  pl.BlockSpec((tm, d), lambda i: (i + b0 + bt, 0)),
                  pl.BlockSpec((1, d), lambda i: (0, 0))],
        out_specs=pl.BlockSpec((tm, d), lambda i: (i, 0)),
        out_shape=jax.ShapeDtypeStruct((n_rows, d), F32),
        compiler_params=_cparams(("parallel",)),
    )(x, y2, y2, g.reshape(1, d))


def _swa_prompt_kernel(sink_ref, q_ref, kp_ref, kc_ref, vp_ref, vc_ref, o_ref):
    n = pl.program_id(1)
    blk = WINDOW
    qi = lax.broadcasted_iota(jnp.int32, (blk, 2 * blk), 0)
    sj = lax.broadcasted_iota(jnp.int32, (blk, 2 * blk), 1)
    rel = qi + blk - sj
    first_key = jnp.where(n > 0, 0, blk)
    valid = (rel >= 0) & (rel < WINDOW) & (sj >= first_key)
    k = jnp.concatenate([kp_ref[...], kc_ref[...]], axis=0).astype(BF16)
    v = jnp.concatenate([vp_ref[...], vc_ref[...]], axis=0).astype(BF16)
    hd = SWA_HEAD_DIM
    for kh in range(SWA_KV_HEADS):
        k_h = k[:, kh * hd:(kh + 1) * hd]
        v_h = v[:, kh * hd:(kh + 1) * hd]
        for g2 in range(SWA_GROUP // 2):
            pair = []
            for gg in range(2):
                h = kh * SWA_GROUP + g2 * 2 + gg
                q_h = q_ref[:, h * hd:(h + 1) * hd].astype(BF16)
                s = _dot_t(q_h, k_h) * SWA_SCALE
                s = jnp.where(valid, s, NEG_INF)
                sink = sink_ref[h]
                m = jnp.maximum(jnp.max(s, axis=-1, keepdims=True), sink)
                p = jnp.exp(s - m)
                denom = jnp.sum(p, axis=-1, keepdims=True) + jnp.exp(sink - m)
                pair.append(_dot(p.astype(BF16), v_h) / denom)
            h0 = kh * SWA_GROUP + g2 * 2
            o_ref[:, h0 * hd:(h0 + 2) * hd] = jnp.concatenate(pair, axis=1).astype(o_ref.dtype)


def _swa_prompt(qkv, sinks, batch, seq):
    nq = SWA_HEADS * SWA_HEAD_DIM
    nk = SWA_KV_HEADS * SWA_HEAD_DIM
    nb = seq // WINDOW
    kcol, vcol = nq // nk, nq // nk + 1

    def prev(b, n):
        return b * nb + jnp.maximum(n - 1, 0)

    return pl.pallas_call(
        _swa_prompt_kernel,
        grid=(batch, nb),
        in_specs=[pl.BlockSpec(memory_space=pltpu.SMEM),
                  pl.BlockSpec((WINDOW, nq), lambda b, n: (b * nb + n, 0)),
                  pl.BlockSpec((WINDOW, nk), lambda b, n: (prev(b, n), kcol)),
                  pl.BlockSpec((WINDOW, nk), lambda b, n: (b * nb + n, kcol)),
                  pl.BlockSpec((WINDOW, nk), lambda b, n: (prev(b, n), vcol)),
                  pl.BlockSpec((WINDOW, nk), lambda b, n: (b * nb + n, vcol))],
        out_specs=pl.BlockSpec((WINDOW, nq), lambda b, n: (b * nb + n, 0)),
        out_shape=jax.ShapeDtypeStruct((batch * seq, nq), BF16),
        compiler_params=_cparams(("parallel", "arbitrary")),
    )(sinks, qkv, qkv, qkv, qkv, qkv)


def _swa_sample_kernel(q_ref, kn_ref, vn_ref, ck_ref, cv_ref, sink_ref, o_ref, nk_ref, nv_ref,
                       kall, vall, *, n_new, seqs):
    w = WINDOW
    rows = SWA_GROUP * n_new
    hd = SWA_HEAD_DIM
    zeros = jnp.zeros((w, kall.shape[1]), F32)
    kall[w:2 * w, :] = zeros
    vall[w:2 * w, :] = zeros
    r = lax.broadcasted_iota(jnp.int32, (rows, 2 * w), 0)
    j = lax.broadcasted_iota(jnp.int32, (rows, 2 * w), 1)
    qn = r % n_new
    valid = ((j < w) & (j > qn)) | ((j >= w) & (j - w <= qn))

    def body(s, carry):
        kall[0:w, :] = ck_ref[s]
        vall[0:w, :] = cv_ref[s]
        kall[w:w + n_new, :] = kn_ref[s]
        vall[w:w + n_new, :] = vn_ref[s]
        nk_ref[s] = kall[n_new:n_new + w, :]
        nv_ref[s] = vall[n_new:n_new + w, :]
        k = kall[...].astype(BF16)
        v = vall[...].astype(BF16)
        for kh in range(SWA_KV_HEADS):
            q = q_ref[s, kh].astype(BF16)
            lg = _dot_t(q, k[:, kh * hd:(kh + 1) * hd]) * SWA_SCALE
            lg = jnp.where(valid, lg, NEG_INF)
            sink = sink_ref[kh][:, 0:1]
            m = jnp.maximum(jnp.max(lg, axis=-1, keepdims=True), sink)
            p = jnp.exp(lg - m)
            denom = jnp.sum(p, axis=-1, keepdims=True) + jnp.exp(sink - m)
            o_ref[s, kh] = (_dot(p.astype(BF16), v[:, kh * hd:(kh + 1) * hd]) / denom).astype(o_ref.dtype)
        return carry

    lax.fori_loop(0, seqs, body, 0)


def _swa_sample(q, k_new, v_new, cache_k, cache_v, sink_rows):
    b, kvh, rows, hd = q.shape
    n_new = k_new.shape[1]
    w, kw = cache_k.shape[1], cache_k.shape[2]
    assert w == WINDOW
    seqs = _pick(b, (8, 4, 2, 1))
    return pl.pallas_call(
        functools.partial(_swa_sample_kernel, n_new=n_new, seqs=seqs),
        grid=(b // seqs,),
        in_specs=[pl.BlockSpec((seqs, kvh, rows, hd), lambda i: (i, 0, 0, 0)),
                  pl.BlockSpec((seqs, n_new, kw), lambda i: (i, 0, 0)),
                  pl.BlockSpec((seqs, n_new, kw), lambda i: (i, 0, 0)),
                  pl.BlockSpec((seqs, w, kw), lambda i: (i, 0, 0)),
                  pl.BlockSpec((seqs, w, kw), lambda i: (i, 0, 0)),
                  pl.BlockSpec((kvh, rows, LANES), lambda i: (0, 0, 0))],
        out_specs=[pl.BlockSpec((seqs, kvh, rows, hd), lambda i: (i, 0, 0, 0)),
                   pl.BlockSpec((seqs, w, kw), lambda i: (i, 0, 0)),
                   pl.BlockSpec((seqs, w, kw), lambda i: (i, 0, 0))],
        out_shape=[jax.ShapeDtypeStruct((b, kvh, rows, hd), BF16),
                   jax.ShapeDtypeStruct((b, w, kw), F32),
                   jax.ShapeDtypeStruct((b, w, kw), F32)],
        scratch_shapes=[pltpu.VMEM((2 * w, kw), F32), pltpu.VMEM((2 * w, kw), F32)],
        compiler_params=_cparams(("parallel",)),
    )(q, k_new, v_new, cache_k, cache_v, sink_rows)


def _mem_heads(q, k, v):
    hd = MEM_HEAD_DIM
    outs = []
    for h in range(MEM_HEADS):
        sl = slice(h * hd, (h + 1) * hd)
        s = _dot_t(q[:, sl], k[:, sl]) * MEM_SCALE
        m = jnp.max(s, axis=-1, keepdims=True)
        p = jnp.exp(s - m)
        denom = jnp.sum(p, axis=-1, keepdims=True)
        outs.append(_dot(p.astype(BF16), v[:, sl]) / denom)
    return jnp.concatenate(outs, axis=1)


def _mem_prompt_kernel(q_ref, k_ref, v_ref, o_ref):
    o_ref[...] = _mem_heads(q_ref[...], k_ref[...].astype(BF16), v_ref[...].astype(BF16)).astype(o_ref.dtype)


def _mem_prompt(q, kv, batch, seq):
    width = MEM_HEADS * MEM_HEAD_DIM
    m = kv.shape[0] // batch
    tq = _pick(seq, (512, 256, 128))
    nq = seq // tq
    return pl.pallas_call(
        _mem_prompt_kernel,
        grid=(batch, nq),
        in_specs=[pl.BlockSpec((tq, width), lambda b, i: (b * nq + i, 0)),
                  pl.BlockSpec((m, width), lambda b, i: (b, 0)),
                  pl.BlockSpec((m, width), lambda b, i: (b, 1))],
        out_specs=pl.BlockSpec((tq, width), lambda b, i: (b * nq + i, 0)),
        out_shape=jax.ShapeDtypeStruct((batch * seq, width), BF16),
        compiler_params=_cparams(("parallel", "arbitrary")),
    )(q, kv, kv)


def _mem_sample_kernel(q_ref, k_ref, v_ref, o_ref, *, seqs):
    def body(s, carry):
        o_ref[s] = _mem_heads(q_ref[s], k_ref[s].astype(BF16), v_ref[s].astype(BF16)).astype(o_ref.dtype)
        return carry

    lax.fori_loop(0, seqs, body, 0)


def _mem_sample(q, k, v):
    b, rows, width = q.shape
    m = k.shape[1]
    seqs = _pick(b, (8, 4, 2, 1))
    return pl.pallas_call(
        functools.partial(_mem_sample_kernel, seqs=seqs),
        grid=(b // seqs,),
        in_specs=[pl.BlockSpec((seqs, rows, width), lambda i: (i, 0, 0)),
                  pl.BlockSpec((seqs, m, width), lambda i: (i, 0, 0)),
                  pl.BlockSpec((seqs, m, width), lambda i: (i, 0, 0))],
        out_specs=pl.BlockSpec((seqs, rows, width), lambda i: (i, 0, 0)),
        out_shape=jax.ShapeDtypeStruct((b, rows, width), BF16),
        compiler_params=_cparams(("parallel",)),
    )(q, k, v)


def _sb_tile(z, mask, c, tri2):
    sp = jnp.maximum(z, 0.0) + jnp.log1p(jnp.exp(-jnp.abs(z)))
    lk = -sp
    if mask is not None:
        lk = jnp.where(mask, lk, 0.0)
    hi = lk.astype(BF16)
    lo = (lk - hi.astype(F32)).astype(BF16)
    cum = _dot(hi, tri2) + _dot(lo, tri2)
    kb = z.shape[1]
    wgt = jnp.exp(z - sp + cum[:, :kb] + c)
    if mask is not None:
        wgt = jnp.where(mask, wgt, 0.0)
    return wgt, cum[:, kb:]


def _tri2(kb):
    r = lax.broadcasted_iota(jnp.int32, (kb, 2 * kb), 0)
    cidx = lax.broadcasted_iota(jnp.int32, (kb, 2 * kb), 1)
    return jnp.where((cidx >= kb) | (r > cidx), 1.0, 0.0).astype(BF16)


def _sb_prompt_kernel(bias_ref, q_ref, k_ref, v_ref, o_ref, kb_scr, vb_scr, c_scr, acc_scr):
    kh = pl.program_id(1)
    n = pl.program_id(2)
    blk = SB_BLOCK
    hd = SB_HEAD_DIM

    @pl.when(n == 0)
    def _():
        kb_scr[...] = k_ref[...].astype(BF16)
        vb_scr[...] = v_ref[...].astype(BF16)

    q = q_ref[...]
    qs = jnp.concatenate([q[:, g * hd:(g + 1) * hd] for g in range(SB_GROUP)], axis=0).astype(BF16)
    bias = jnp.concatenate(
        [jnp.full((blk, LANES), bias_ref[kh * SB_GROUP + g], F32) for g in range(SB_GROUP)], axis=0)
    tri2 = _tri2(blk)
    rows = SB_GROUP * blk
    qi = lax.broadcasted_iota(jnp.int32, (rows, blk), 0) % blk
    sj = lax.broadcasted_iota(jnp.int32, (rows, blk), 1)
    diag_mask = sj < qi

    def tile(kblk, mask):
        start = pl.multiple_of(kblk * blk, blk)
        k_t = kb_scr[pl.ds(start, blk), :]
        v_t = vb_scr[pl.ds(start, blk), :]
        z = _dot_t(qs, k_t) * SB_SCALE + bias
        wgt, tot = _sb_tile(z, mask, c_scr[...], tri2)
        acc_scr[...] += _dot(wgt.astype(BF16), v_t)
        c_scr[...] += tot

    c_scr[...] = jnp.zeros_like(c_scr)
    acc_scr[...] = jnp.zeros_like(acc_scr)
    tile(n, diag_mask)

    def body(i, carry):
        tile(n - 1 - i, None)
        return carry

    lax.fori_loop(0, n, body, 0)
    acc = acc_scr[...]
    o_ref[...] = jnp.concatenate([acc[g * blk:(g + 1) * blk] for g in range(SB_GROUP)],
                                 axis=1).astype(o_ref.dtype)


def _sb_prompt(qkv, bias, batch, seq):
    hd = SB_HEAD_DIM
    nb = seq // SB_BLOCK
    gw = SB_GROUP * hd
    kcol0 = SB_HEADS
    vcol0 = SB_HEADS + SB_KV_HEADS
    return pl.pallas_call(
        _sb_prompt_kernel,
        grid=(batch, SB_KV_HEADS, nb),
        in_specs=[pl.BlockSpec(memory_space=pltpu.SMEM),
                  pl.BlockSpec((SB_BLOCK, gw), lambda b, kh, n: (b * nb + n, kh)),
                  pl.BlockSpec((seq, hd), lambda b, kh, n: (b, kcol0 + kh)),
                  pl.BlockSpec((seq, hd), lambda b, kh, n: (b, vcol0 + kh))],
        out_specs=pl.BlockSpec((SB_BLOCK, gw), lambda b, kh, n: (b * nb + n, kh)),
        out_shape=jax.ShapeDtypeStruct((batch * seq, SB_HEADS * hd), BF16),
        scratch_shapes=[pltpu.VMEM((seq, hd), BF16), pltpu.VMEM((seq, hd), BF16),
                        pltpu.VMEM((SB_GROUP * SB_BLOCK, LANES), F32),
                        pltpu.VMEM((SB_GROUP * SB_BLOCK, hd), F32)],
        compiler_params=_cparams(("parallel", "parallel", "arbitrary")),
    )(bias, qkv, qkv, qkv)


def _sb_sample_kernel(pt_ref, q_ref, kn_ref, vn_ref, bias_ref, *rest, n_new, pages_per_step):
    k_refs = rest[:pages_per_step]
    v_refs = rest[pages_per_step:2 * pages_per_step]
    o_ref, kpad, vpad, c_scr, acc_scr = rest[2 * pages_per_step:]
    del pt_ref
    step = pl.program_id(1)
    hd = SB_HEAD_DIM
    page = kpad.shape[0]
    rows = SB_GROUP * n_new
    tri2 = _tri2(page)
    q = q_ref[0]
    bias = bias_ref[...]

    def tile(k_page, v_page, mask):
        kb = k_page.astype(BF16)
        vb = v_page.astype(BF16)
        z = jnp.concatenate(
            [_dot_t(q[kh * rows:(kh + 1) * rows], kb[:, kh * hd:(kh + 1) * hd])
             for kh in range(SB_KV_HEADS)], axis=0) * SB_SCALE + bias
        wgt, tot = _sb_tile(z, mask, c_scr[...], tri2)
        wb = wgt.astype(BF16)
        for kh in range(SB_KV_HEADS):
            acc_scr[kh * rows:(kh + 1) * rows, :] += _dot(wb[kh * rows:(kh + 1) * rows],
                                                         vb[:, kh * hd:(kh + 1) * hd])
        c_scr[...] += tot

    @pl.when(step == 0)
    def _():
        c_scr[...] = jnp.zeros_like(c_scr)
        acc_scr[...] = jnp.zeros_like(acc_scr)
        kpad[...] = jnp.zeros_like(kpad)
        vpad[...] = jnp.zeros_like(vpad)
        kpad[0:n_new, :] = kn_ref[0]
        vpad[0:n_new, :] = vn_ref[0]
        r = lax.broadcasted_iota(jnp.int32, (SB_KV_HEADS * rows, page), 0)
        j = lax.broadcasted_iota(jnp.int32, (SB_KV_HEADS * rows, page), 1)
        tile(kpad[...], vpad[...], j < (r % n_new))

    for i in range(pages_per_step):
        tile(k_refs[i][0], v_refs[i][0], None)

    @pl.when(step == pl.num_programs(1) - 1)
    def _():
        o_ref[0] = acc_scr[...].astype(o_ref.dtype)


def _sb_sample(q, k_new, v_new, k_pool, v_pool, page_table, bias_rows):
    b, qrows, hd = q.shape
    n_new = k_new.shape[1]
    n_pages = page_table.shape[1]
    page, kw = k_pool.shape[1], k_pool.shape[2]
    assert page == SB_BLOCK
    pps = _pick(n_pages, (8, 4, 2, 1))
    steps = n_pages // pps

    def page_map(i):
        return lambda s, c, pt: (pt[s * n_pages + (n_pages - 1 - (c * pps + i))], 0, 0)

    pool_specs = [pl.BlockSpec((1, page, kw), page_map(i)) for i in range(pps)]
    grid_spec = pltpu.PrefetchScalarGridSpec(
        num_scalar_prefetch=1,
        grid=(b, steps),
        in_specs=[pl.BlockSpec((1, qrows, hd), lambda s, c, pt: (s, 0, 0)),
                  pl.BlockSpec((1, n_new, kw), lambda s, c, pt: (s, 0, 0)),
                  pl.BlockSpec((1, n_new, kw), lambda s, c, pt: (s, 0, 0)),
                  pl.BlockSpec((qrows, LANES), lambda s, c, pt: (0, 0))] + pool_specs + pool_specs,
        out_specs=pl.BlockSpec((1, qrows, hd), lambda s, c, pt: (s, 0, 0)),
        scratch_shapes=[pltpu.VMEM((page, kw), F32), pltpu.VMEM((page, kw), F32),
                        pltpu.VMEM((qrows, LANES), F32), pltpu.VMEM((qrows, hd), F32)],
    )
    return pl.pallas_call(
        functools.partial(_sb_sample_kernel, n_new=n_new, pages_per_step=pps),
        grid_spec=grid_spec,
        out_shape=jax.ShapeDtypeStruct((b, qrows, hd), BF16),
        compiler_params=_cparams(("parallel", "arbitrary")),
    )(page_table.reshape(-1), q, k_new, v_new, bias_rows, *([k_pool] * pps), *([v_pool] * pps))


def _router_kernel(x_ref, g_ref, w_ref, b_ref, h_ref, info_ref):
    xv = x_ref[...]
    ms = jnp.mean(xv * xv, axis=-1, keepdims=True)
    h = xv * lax.rsqrt(ms + RMS_EPS) * g_ref[...]
    h_ref[...] = h
    logits = jnp.dot(h, w_ref[...], preferred_element_type=F32,
                     precision=lax.Precision.HIGHEST) + b_ref[...]
    lane = lax.broadcasted_iota(jnp.int32, logits.shape, 1)
    big = jnp.int32(1 << 20)
    is_g = (lane >= N_EXPERTS) & (lane < N_EXPERTS + N_GROUPS)
    gl = jnp.where(is_g, logits, -jnp.inf)
    gmax = jnp.max(gl, axis=-1, keepdims=True)
    gidx = jnp.min(jnp.where(gl == gmax, lane - N_EXPERTS, big), axis=-1, keepdims=True)
    p_group = 1.0 / jnp.sum(jnp.exp(gl - gmax), axis=-1, keepdims=True)
    lo = gidx * EXPERTS_PER_GROUP
    in_grp = (lane >= lo) & (lane < lo + EXPERTS_PER_GROUP)
    el = jnp.where(in_grp, logits, -jnp.inf)
    emax = jnp.max(el, axis=-1, keepdims=True)
    pe = jnp.exp(el - emax)
    prob = pe / jnp.sum(pe, axis=-1, keepdims=True)
    prob = jnp.where(in_grp, prob, -1.0)
    p1 = jnp.max(prob, axis=-1, keepdims=True)
    i1 = jnp.min(jnp.where(prob == p1, lane, big), axis=-1, keepdims=True)
    rest = jnp.where(lane == i1, -1.0, prob)
    p2 = jnp.max(rest, axis=-1, keepdims=True)
    i2 = jnp.min(jnp.where(rest == p2, lane, big), axis=-1, keepdims=True)
    tot = p1 + p2
    w1 = p1 / tot * p_group
    w2 = p2 / tot * p_group
    info = jnp.where(lane == 0, i1.astype(F32),
                     jnp.where(lane == 1, i2.astype(F32),
                               jnp.where(lane == 2, w1, jnp.where(lane == 3, w2, 0.0))))
    info_ref[...] = info


def _router(x, g, w_router, b_router, w_group, b_group):
    t, d = x.shape
    pad = LANES - N_EXPERTS - N_GROUPS
    w = jnp.concatenate([w_router, w_group, jnp.zeros((d, pad), F32)], axis=1)
    bias = jnp.concatenate([b_router, b_group, jnp.zeros((pad,), F32)]).reshape(1, LANES)
    tm = _pick(t, (256, 128, 8))
    return pl.pallas_call(
        _router_kernel,
        grid=(t // tm,),
        in_specs=[pl.BlockSpec((tm, d), lambda i: (i, 0)),
                  pl.BlockSpec((1, d), lambda i: (0, 0)),
                  pl.BlockSpec((d, LANES), lambda i: (0, 0)),
                  pl.BlockSpec((1, LANES), lambda i: (0, 0))],
        out_specs=[pl.BlockSpec((tm, d), lambda i: (i, 0)),
                   pl.BlockSpec((tm, LANES), lambda i: (i, 0))],
        out_shape=[jax.ShapeDtypeStruct((t, d), F32), jax.ShapeDtypeStruct((t, LANES), F32)],
        compiler_params=_cparams(("parallel",)),
    )(x, g.reshape(1, d), w, bias)


def _moe_kernel(te_ref, nvalid_ref, tok_ref, dst_ref, h_hbm, sw_ref, wg_ref, wu_ref, wd_ref, y_hbm,
                xbuf, ybuf, gsem, ssem, *, tm):
    del te_ref
    i = pl.program_id(0)
    base = i * tm

    def gather_copy(r):
        return pltpu.make_async_copy(h_hbm.at[pl.ds(tok_ref[base + r], 1)], xbuf.at[pl.ds(r, 1)], gsem)

    def scatter_copy(r):
        return pltpu.make_async_copy(ybuf.at[pl.ds(r, 1)], y_hbm.at[pl.ds(dst_ref[base + r], 1)], ssem)

    n_valid = nvalid_ref[i]

    @pl.when(n_valid > 0)
    def _():
        def start_g(r, c):
            gather_copy(r).start()
            return c

        lax.fori_loop(0, tm, start_g, 0)

        def wait_g(r, c):
            gather_copy(r).wait()
            return c

        lax.fori_loop(0, tm, wait_g, 0)
        x = xbuf[...].astype(BF16)
        a = _dot(x, wg_ref[0].astype(BF16))
        u = _dot(x, wu_ref[0].astype(BF16))
        act = a * jax.nn.sigmoid(a) * u * sw_ref[...]
        ybuf[...] = _dot(act.astype(BF16), wd_ref[0].astype(BF16))

        def start_s(r, c):
            scatter_copy(r).start()
            return c

        lax.fori_loop(0, n_valid, start_s, 0)

        def wait_s(r, c):
            scatter_copy(r).wait()
            return c

        lax.fori_loop(0, n_valid, wait_s, 0)


def _moe(h, info, w_gate, w_up, w_down, layer):
    t, d = h.shape
    n_exp, ff = N_EXPERTS, w_gate.shape[-1]
    tm = MOE_TILE
    pairs = 2 * t
    n_tiles = -(-pairs // tm) + N_EXPERTS
    n_slots = n_tiles * tm
    eid = jnp.concatenate([info[:, 0], info[:, 1]]).astype(jnp.int32)
    wts = jnp.concatenate([info[:, 2], info[:, 3]])
    order = jnp.argsort(eid, stable=True).astype(jnp.int32)
    counts = jnp.sum((eid[:, None] == jnp.arange(n_exp, dtype=jnp.int32)[None, :]).astype(jnp.int32), axis=0)
    padded = (counts + tm - 1) // tm * tm
    pad_end = jnp.cumsum(padded)
    pad_start = pad_end - padded
    start = jnp.cumsum(counts) - counts
    tile_start = jnp.arange(n_tiles, dtype=jnp.int32) * tm
    n_used = (pad_end[-1] // tm).astype(jnp.int32)
    tile_e = jnp.sum((tile_start[:, None] >= pad_end[None, :]).astype(jnp.int32), axis=1)
    tile_e = jnp.minimum(tile_e, n_exp - 1)
    last_e = tile_e[jnp.maximum(n_used - 1, 0)]
    tile_e = jnp.where(jnp.arange(n_tiles) < n_used, tile_e, last_e).astype(jnp.int32)
    slot = jnp.arange(n_slots, dtype=jnp.int32)
    slot_e = jnp.repeat(tile_e, tm)
    rank = slot - pad_start[slot_e]
    valid = (rank < counts[slot_e]) & (slot < pad_end[-1])
    src = jnp.clip(start[slot_e] + rank, 0, pairs - 1)
    pair = order[src]
    slot_tok = jnp.where(valid, pair % t, 0).astype(jnp.int32)
    slot_dst = jnp.where(valid, pair, 0).astype(jnp.int32)
    tile_valid = jnp.clip(counts[tile_e] - (tile_start - pad_start[tile_e]), 0, tm)
    tile_valid = jnp.where(jnp.arange(n_tiles) < n_used, tile_valid, 0).astype(jnp.int32)
    slot_w = jnp.where(valid, wts[pair], 0.0).reshape(n_slots, 1)

    wg = w_gate.reshape((-1,) + w_gate.shape[-2:])
    wu = w_up.reshape((-1,) + w_up.shape[-2:])
    wd = w_down.reshape((-1,) + w_down.shape[-2:])
    e0 = layer * n_exp
    grid_spec = pltpu.PrefetchScalarGridSpec(
        num_scalar_prefetch=4,
        grid=(n_tiles,),
        in_specs=[pl.BlockSpec(memory_space=pl.ANY),
                  pl.BlockSpec((tm, 1), lambda i, te, nu, tk, ds: (i, 0)),
                  pl.BlockSpec((1, d, ff), lambda i, te, nu, tk, ds: (e0 + te[i], 0, 0)),
                  pl.BlockSpec((1, d, ff), lambda i, te, nu, tk, ds: (e0 + te[i], 0, 0)),
                  pl.BlockSpec((1, ff, d), lambda i, te, nu, tk, ds: (e0 + te[i], 0, 0))],
        out_specs=pl.BlockSpec(memory_space=pl.ANY),
        scratch_shapes=[pltpu.VMEM((tm, d), F32), pltpu.VMEM((tm, d), F32),
                        pltpu.SemaphoreType.DMA(()), pltpu.SemaphoreType.DMA(())],
    )
    return pl.pallas_call(
        functools.partial(_moe_kernel, tm=tm),
        grid_spec=grid_spec,
        out_shape=jax.ShapeDtypeStruct((pairs, d), F32),
        compiler_params=_cparams(("arbitrary",)),
    )(tile_e, tile_valid, slot_tok, slot_dst, h, slot_w, wg, wu, wd)


def _rope_tables(pos):
    half = ROPE_DIM // 2
    inv_freq = ROPE_THETA ** (-2.0 * jnp.arange(half, dtype=F32) / ROPE_DIM)
    ang = pos.astype(F32)[:, None] * inv_freq[None, :]
    cos, sin = jnp.cos(ang), jnp.sin(ang)
    t = pos.shape[0]
    ones = jnp.ones((t, SWA_HEAD_DIM - ROPE_DIM), F32)
    zeros = jnp.zeros((t, half), F32)
    zrest = jnp.zeros((t, SWA_HEAD_DIM - ROPE_DIM), F32)
    c_head = jnp.concatenate([cos, cos, ones], axis=1)
    plus_head = jnp.concatenate([zeros, sin, zrest], axis=1)
    minus_head = jnp.concatenate([-sin, zeros, zrest], axis=1)
    rep = LANES // SWA_HEAD_DIM
    return tuple(jnp.concatenate([a] * rep, axis=1) for a in (c_head, plus_head, minus_head))


def kernel(x_prompt, x_sample, mem_prompt, cache_swa_k, cache_swa_v, cache_sb_k, cache_sb_v, cache_mem_k, cache_mem_v, page_table, norm_mix, w_in_swa, sinks_swa, w_out_swa, w_in_sb, sb_bias, w_out_sb, norm_mem_q, norm_mem_kv, w_mem_q, w_mem_kv, w_mem_o, norm_ffn, w_group, b_group, w_router, b_router, w_gate, w_up, w_down, norm_final):
    b_p, s_p, d = x_prompt.shape
    b_s, n_s, _ = x_sample.shape
    depth = norm_mix.shape[0]
    t_p, t_s = b_p * s_p, b_s * n_s
    t = t_p + t_s
    past_len = page_table.shape[1] * cache_sb_k.shape[2]
    assert s_p >= WINDOW and s_p % WINDOW == 0

    x = jnp.concatenate([x_prompt.reshape(t_p, d), x_sample.reshape(t_s, d)], axis=0)
    pos = jnp.concatenate([jnp.tile(jnp.arange(s_p), b_p), jnp.tile(past_len + jnp.arange(n_s), b_s)])
    rope = _rope_tables(pos)
    mem_flat = mem_prompt.reshape(-1, d)
    mem_m = mem_prompt.shape[1]
    mem_w = MEM_HEADS * MEM_HEAD_DIM

    swa_kp, swa_vp, swa_ks, swa_vs = [], [], [], []
    sb_kp, sb_vp, sb_ks, sb_vs = [], [], [], []
    mem_kp, mem_vp = [], []
    y2 = None
    for i in range(depth):
        j = i // 2
        first = y2 is None
        if i % 2 == 0:
            nq = SWA_HEADS * SWA_HEAD_DIM
            nk = SWA_KV_HEADS * SWA_HEAD_DIM
            res = _norm_matmul(x, norm_mix[i], w_in_swa[j], n_rows=t, y2=y2, rope=rope,
                               rope_cols=nq + nk, emit_x=not first)
            qkv, x = (res, x) if first else res
            o_p = _swa_prompt(qkv, sinks_swa[j], b_p, s_p)
            qs = qkv[t_p:, :nq].reshape(b_s, n_s, SWA_KV_HEADS, SWA_GROUP, SWA_HEAD_DIM)
            qs = qs.transpose(0, 2, 3, 1, 4).reshape(b_s, SWA_KV_HEADS, SWA_GROUP * n_s, SWA_HEAD_DIM)
            kn = qkv[t_p:, nq:nq + nk].reshape(b_s, n_s, nk)
            vn = qkv[t_p:, nq + nk:].reshape(b_s, n_s, nk)
            sink_rows = jnp.broadcast_to(
                jnp.repeat(sinks_swa[j].reshape(SWA_KV_HEADS, SWA_GROUP), n_s, axis=1)[:, :, None],
                (SWA_KV_HEADS, SWA_GROUP * n_s, LANES))
            o_s, kbuf, vbuf = _swa_sample(qs, kn, vn, cache_swa_k[j].reshape(b_s, -1, nk),
                                          cache_swa_v[j].reshape(b_s, -1, nk), sink_rows)
            o_s = o_s.reshape(b_s, SWA_KV_HEADS, SWA_GROUP, n_s, SWA_HEAD_DIM)
            o_s = o_s.transpose(0, 3, 1, 2, 4).reshape(t_s, nq)
            kp = qkv[:t_p, nq:nq + nk].reshape(b_p, s_p, SWA_KV_HEADS, SWA_HEAD_DIM)
            vp = qkv[:t_p, nq + nk:].reshape(b_p, s_p, SWA_KV_HEADS, SWA_HEAD_DIM)
            swa_kp.append(kp[:, -WINDOW:])
            swa_vp.append(vp[:, -WINDOW:])
            swa_ks.append(kbuf.reshape(b_s, -1, SWA_KV_HEADS, SWA_HEAD_DIM))
            swa_vs.append(vbuf.reshape(b_s, -1, SWA_KV_HEADS, SWA_HEAD_DIM))
            w_out = w_out_swa[j]
        else:
            nq = SB_HEADS * SB_HEAD_DIM
            nk = SB_KV_HEADS * SB_HEAD_DIM
            res = _norm_matmul(x, norm_mix[i], w_in_sb[j], n_rows=t, y2=y2, emit_x=not first)
            qkv, x = (res, x) if first else res
            o_p = _sb_prompt(qkv, sb_bias[j], b_p, s_p)
            qs = qkv[t_p:, :nq].reshape(b_s, n_s, SB_KV_HEADS, SB_GROUP, SB_HEAD_DIM)
            qs = qs.transpose(0, 2, 3, 1, 4).reshape(b_s, SB_HEADS * n_s, SB_HEAD_DIM).astype(BF16)
            kn = qkv[t_p:, nq:nq + nk].reshape(b_s, n_s, nk)
            vn = qkv[t_p:, nq + nk:].reshape(b_s, n_s, nk)
            bias_rows = jnp.broadcast_to(jnp.repeat(sb_bias[j], n_s)[:, None], (SB_HEADS * n_s, LANES))
            o_s = _sb_sample(qs, kn, vn, cache_sb_k[j].reshape(-1, cache_sb_k.shape[2], nk),
                             cache_sb_v[j].reshape(-1, cache_sb_v.shape[2], nk), page_table, bias_rows)
            o_s = o_s.reshape(b_s, SB_KV_HEADS, SB_GROUP, n_s, SB_HEAD_DIM)
            o_s = o_s.transpose(0, 3, 1, 2, 4).reshape(t_s, nq)
            sb_kp.append(qkv[:t_p, nq:nq + nk].reshape(b_p, s_p, SB_KV_HEADS, SB_HEAD_DIM))
            sb_vp.append(qkv[:t_p, nq + nk:].reshape(b_p, s_p, SB_KV_HEADS, SB_HEAD_DIM))
            sb_ks.append(kn.reshape(b_s, n_s, SB_KV_HEADS, SB_HEAD_DIM))
            sb_vs.append(vn.reshape(b_s, n_s, SB_KV_HEADS, SB_HEAD_DIM))
            w_out = w_out_sb[j]
        x = _resid_matmul(jnp.concatenate([o_p, o_s], axis=0), w_out, x)

        mkv = _norm_matmul(mem_flat, norm_mem_kv[i], w_mem_kv[i])
        mem_kp.append(mkv[:, :mem_w].reshape(b_p, mem_m, MEM_HEADS, MEM_HEAD_DIM))
        mem_vp.append(mkv[:, mem_w:].reshape(b_p, mem_m, MEM_HEADS, MEM_HEAD_DIM))
        qm = _norm_matmul(x, norm_mem_q[i], w_mem_q[i], out_dtype=BF16)
        om_p = _mem_prompt(qm, mkv, b_p, s_p)
        row_pad = 16 - n_s % 16 if n_s % 16 else 0
        qm_s = jnp.pad(qm[t_p:].reshape(b_s, n_s, mem_w), ((0, 0), (0, row_pad), (0, 0)))
        om_s = _mem_sample(qm_s, cache_mem_k[i].reshape(b_s, -1, mem_w), cache_mem_v[i].reshape(b_s, -1, mem_w))
        om = jnp.concatenate([om_p, om_s[:, :n_s].reshape(t_s, mem_w)], axis=0)
        x = _resid_matmul(om, w_mem_o[i], x)

        h, info = _router(x, norm_ffn[i], w_router[i], b_router[i], w_group[i], b_group[i])
        y2 = _moe(h, info, w_gate, w_up, w_down, i)

    y_prompt = _final_norm(x, y2, norm_final, 0, t_p).reshape(b_p, s_p, d)
    y_sample = _final_norm(x, y2, norm_final, t_p, t_s).reshape(b_s, n_s, d)
    return (y_prompt, y_sample,
            jnp.stack(swa_kp), jnp.stack(swa_vp), jnp.stack(swa_ks), jnp.stack(swa_vs),
            jnp.stack(sb_kp), jnp.stack(sb_vp), jnp.stack(sb_ks), jnp.stack(sb_vs),
            jnp.stack(mem_kp), jnp.stack(mem_vp))
```

```python
import functools

import jax
import jax.numpy as jnp
from jax import lax
from jax.experimental import pallas as pl
from jax.experimental.pallas import tpu as pltpu

F32 = jnp.float32
BF16 = jnp.bfloat16

SWA_HEADS, SWA_KV_HEADS, SWA_HEAD_DIM = 32, 4, 64
SWA_GROUP = SWA_HEADS // SWA_KV_HEADS
WINDOW = 128
ROPE_THETA = 500000.0
ROPE_DIM = SWA_HEAD_DIM // 4
SWA_SCALE = SWA_HEAD_DIM ** -0.5
SB_HEADS, SB_KV_HEADS, SB_HEAD_DIM = 16, 4, 128
SB_GROUP = SB_HEADS // SB_KV_HEADS
SB_BLOCK = 128
SB_SCALE = SB_HEAD_DIM ** -0.5
MEM_HEADS, MEM_HEAD_DIM = 4, 128
MEM_SCALE = MEM_HEAD_DIM ** -0.5
N_GROUPS, EXPERTS_PER_GROUP = 4, 8
N_EXPERTS = N_GROUPS * EXPERTS_PER_GROUP
RMS_EPS = 1e-6
NEG_INF = -1e30

LANES = 128
VMEM_LIMIT = 56 * 1024 * 1024
MOE_TILE = 256


def _cparams(sem):
    return pltpu.CompilerParams(dimension_semantics=sem, vmem_limit_bytes=VMEM_LIMIT)


def _pick(n, prefs):
    for p in prefs:
        if n % p == 0:
            return p
    return n


def _dot_t(a, b):
    return lax.dot_general(a, b, (((1,), (1,)), ((), ())), preferred_element_type=F32)


def _dot(a, b):
    return jnp.dot(a, b, preferred_element_type=F32)


def _moe_combine(xv, ya_ref, yb_ref, info_ref):
    info = info_ref[...]
    return xv + ya_ref[...] * info[:, 2:3] + yb_ref[...] * info[:, 3:4]


def _norm_matmul_kernel(*refs, add, emit_x, rope_cols, tn):
    it = iter(refs)
    x_ref = next(it)
    ya_ref = next(it) if add else None
    yb_ref = next(it) if add else None
    info_ref = next(it) if add else None
    g_ref = next(it)
    w_ref = next(it)
    if rope_cols:
        cos_ref, sp_ref, sm_ref = next(it), next(it), next(it)
    o_ref = next(it)
    xo_ref = next(it) if emit_x else None
    h_scr = next(it)
    j = pl.program_id(1)

    @pl.when(j == 0)
    def _():
        xv = x_ref[...]
        if add:
            xv = _moe_combine(xv, ya_ref, yb_ref, info_ref)
        if emit_x:
            xo_ref[...] = xv
        ms = jnp.mean(xv * xv, axis=-1, keepdims=True)
        h_scr[...] = (xv * lax.rsqrt(ms + RMS_EPS) * g_ref[...]).astype(BF16)

    acc = _dot(h_scr[...], w_ref[...].astype(BF16))
    if not rope_cols:
        o_ref[...] = acc.astype(o_ref.dtype)
        return

    @pl.when(j * tn < rope_cols)
    def _():
        reps = tn // LANES
        cos = jnp.concatenate([cos_ref[...]] * reps, axis=1)
        s_plus = jnp.concatenate([sp_ref[...]] * reps, axis=1)
        s_minus = jnp.concatenate([sm_ref[...]] * reps, axis=1)
        half = ROPE_DIM // 2
        roped = (acc * cos + pltpu.roll(acc, half, axis=1) * s_plus
                 + pltpu.roll(acc, tn - half, axis=1) * s_minus)
        col = j * tn + lax.broadcasted_iota(jnp.int32, acc.shape, 1)
        o_ref[...] = jnp.where(col < rope_cols, roped, acc).astype(o_ref.dtype)

    @pl.when(j * tn >= rope_cols)
    def _():
        o_ref[...] = acc.astype(o_ref.dtype)


def _norm_matmul(x, g, w, *, n_rows=None, moe=None, rope=None, rope_cols=0, emit_x=False,
                 out_dtype=F32):
    t = n_rows if n_rows is not None else x.shape[0]
    d = x.shape[1]
    n = w.shape[1]
    add = moe is not None
    tm = _pick(t, (256,) if add else (512, 256))
    tn = _pick(n, (512, 256, 128))
    n_i = t // tm
    in_specs = [pl.BlockSpec((tm, d), lambda i, j: (i, 0))]
    args = [x]
    if add:
        in_specs += [pl.BlockSpec((tm, d), lambda i, j: (i, 0)),
                     pl.BlockSpec((tm, d), lambda i, j: (i + n_i, 0)),
                     pl.BlockSpec((tm, LANES), lambda i, j: (i, 0))]
        args += [moe[0], moe[0], moe[1]]
    in_specs += [pl.BlockSpec((1, d), lambda i, j: (0, 0)),
                 pl.BlockSpec((d, tn), lambda i, j: (0, j))]
    args += [g.reshape(1, d), w]
    if rope_cols:
        in_specs += [pl.BlockSpec((tm, LANES), lambda i, j: (i, 0))] * 3
        args += list(rope)
    out_shape = [jax.ShapeDtypeStruct((t, n), out_dtype)]
    out_specs = [pl.BlockSpec((tm, tn), lambda i, j: (i, j))]
    if emit_x:
        out_shape.append(jax.ShapeDtypeStruct((t, d), F32))
        out_specs.append(pl.BlockSpec((tm, d), lambda i, j: (i, 0)))
    res = pl.pallas_call(
        functools.partial(_norm_matmul_kernel, add=add, emit_x=emit_x, rope_cols=rope_cols, tn=tn),
        grid=(n_i, n // tn),
        in_specs=in_specs,
        out_specs=out_specs,
        out_shape=out_shape,
        scratch_shapes=[pltpu.VMEM((tm, d), BF16)],
        compiler_params=_cparams(("parallel", "arbitrary")),
    )(*args)
    return res if emit_x else res[0]


def _resid_matmul_kernel(a_ref, w_ref, r_ref, o_ref):
    o_ref[...] = r_ref[...] + _dot(a_ref[...].astype(BF16), w_ref[...].astype(BF16))


def _resid_matmul(a, w, resid):
    t, k = a.shape
    n = w.shape[1]
    tm = _pick(t, (512, 256))
    tn = _pick(n, (512, 256, 128))
    return pl.pallas_call(
        _resid_matmul_kernel,
        grid=(t // tm, n // tn),
        in_specs=[pl.BlockSpec((tm, k), lambda i, j: (i, 0)),
                  pl.BlockSpec((k, tn), lambda i, j: (0, j)),
                  pl.BlockSpec((tm, tn), lambda i, j: (i, j))],
        out_specs=pl.BlockSpec((tm, tn), lambda i, j: (i, j)),
        out_shape=jax.ShapeDtypeStruct((t, n), F32),
        compiler_params=_cparams(("parallel", "arbitrary")),
    )(a, w, resid)


def _final_norm_kernel(x_ref, ya_ref, yb_ref, info_ref, g_ref, o_ref):
    xv = _moe_combine(x_ref[...], ya_ref, yb_ref, info_ref)
    ms = jnp.mean(xv * xv, axis=-1, keepdims=True)
    o_ref[...] = xv * lax.rsqrt(ms + RMS_EPS) * g_ref[...]


def _final_norm(x, y2, info, g, row0, n_rows):
    t, d = x.shape
    tm = _pick(n_rows, (256, 128, 8))
    assert row0 % tm == 0 and t % tm == 0
    b0, bt = row0 // tm, t // tm
    return pl.pallas_call(
        _final_norm_kernel,
        grid=(n_rows // tm,),
        in_specs=[pl.BlockSpec((tm, d), lambda i: (i + b0, 0)),
                  pl.BlockSpec((tm, d), lambda i: (i + b0, 0)),
                  pl.BlockSpec((tm, d), lambda i: (i + b0 + bt, 0)),
                  pl.BlockSpec((tm, LANES), lambda i: (i + b0, 0)),
                  pl.BlockSpec((1, d), lambda i: (0, 0))],
        out_specs=pl.BlockSpec((tm, d), lambda i: (i, 0)),
        out_shape=jax.ShapeDtypeStruct((n_rows, d), F32),
        compiler_params=_cparams(("parallel",)),
    )(x, y2, y2, info, g.reshape(1, d))


def _swa_prompt_kernel(sink_ref, q_ref, kp_ref, kc_ref, vp_ref, vc_ref, o_ref):
    n = pl.program_id(1)
    blk = WINDOW
    qi = lax.broadcasted_iota(jnp.int32, (blk, 2 * blk), 0)
    sj = lax.broadcasted_iota(jnp.int32, (blk, 2 * blk), 1)
    rel = qi + blk - sj
    first_key = jnp.where(n > 0, 0, blk)
    valid = (rel >= 0) & (rel < WINDOW) & (sj >= first_key)
    k = jnp.concatenate([kp_ref[...], kc_ref[...]], axis=0).astype(BF16)
    v = jnp.concatenate([vp_ref[...], vc_ref[...]], axis=0).astype(BF16)
    hd = SWA_HEAD_DIM
    for kh in range(SWA_KV_HEADS):
        k_h = k[:, kh * hd:(kh + 1) * hd]
        v_h = v[:, kh * hd:(kh + 1) * hd]
        for g2 in range(SWA_GROUP // 2):
            pair = []
            for gg in range(2):
                h = kh * SWA_GROUP + g2 * 2 + gg
                q_h = q_ref[:, h * hd:(h + 1) * hd].astype(BF16)
                s = _dot_t(q_h, k_h) * SWA_SCALE
                s = jnp.where(valid, s, NEG_INF)
                sink = sink_ref[h]
                m = jnp.maximum(jnp.max(s, axis=-1, keepdims=True), sink)
                p = jnp.exp(s - m)
                denom = jnp.sum(p, axis=-1, keepdims=True) + jnp.exp(sink - m)
                pair.append(_dot(p.astype(BF16), v_h) / denom)
            h0 = kh * SWA_GROUP + g2 * 2
            o_ref[:, h0 * hd:(h0 + 2) * hd] = jnp.concatenate(pair, axis=1).astype(o_ref.dtype)


def _swa_prompt(qkv, sinks, batch, seq):
    nq = SWA_HEADS * SWA_HEAD_DIM
    nk = SWA_KV_HEADS * SWA_HEAD_DIM
    nb = seq // WINDOW
    kcol, vcol = nq // nk, nq // nk + 1

    def prev(b, n):
        return b * nb + jnp.maximum(n - 1, 0)

    return pl.pallas_call(
        _swa_prompt_kernel,
        grid=(batch, nb),
        in_specs=[pl.BlockSpec(memory_space=pltpu.SMEM),
                  pl.BlockSpec((WINDOW, nq), lambda b, n: (b * nb + n, 0)),
                  pl.BlockSpec((WINDOW, nk), lambda b, n: (prev(b, n), kcol)),
                  pl.BlockSpec((WINDOW, nk), lambda b, n: (b * nb + n, kcol)),
                  pl.BlockSpec((WINDOW, nk), lambda b, n: (prev(b, n), vcol)),
                  pl.BlockSpec((WINDOW, nk), lambda b, n: (b * nb + n, vcol))],
        out_specs=pl.BlockSpec((WINDOW, nq), lambda b, n: (b * nb + n, 0)),
        out_shape=jax.ShapeDtypeStruct((batch * seq, nq), BF16),
        compiler_params=_cparams(("parallel", "arbitrary")),
    )(sinks, qkv, qkv, qkv, qkv, qkv)


def _swa_sample_kernel(q_ref, kn_ref, vn_ref, ck_ref, cv_ref, sink_ref, o_ref, nk_ref, nv_ref,
                       kall, vall, *, n_new, seqs):
    w = WINDOW
    rows = SWA_GROUP * n_new
    hd = SWA_HEAD_DIM
    zeros = jnp.zeros((w, kall.shape[1]), F32)
    kall[w:2 * w, :] = zeros
    vall[w:2 * w, :] = zeros
    r = lax.broadcasted_iota(jnp.int32, (rows, 2 * w), 0)
    j = lax.broadcasted_iota(jnp.int32, (rows, 2 * w), 1)
    qn = r % n_new
    valid = ((j < w) & (j > qn)) | ((j >= w) & (j - w <= qn))

    def body(s, carry):
        kall[0:w, :] = ck_ref[s]
        vall[0:w, :] = cv_ref[s]
        kall[w:w + n_new, :] = kn_ref[s]
        vall[w:w + n_new, :] = vn_ref[s]
        nk_ref[s] = kall[n_new:n_new + w, :]
        nv_ref[s] = vall[n_new:n_new + w, :]
        k = kall[...].astype(BF16)
        v = vall[...].astype(BF16)
        for kh in range(SWA_KV_HEADS):
            q = q_ref[s, kh].astype(BF16)
            lg = _dot_t(q, k[:, kh * hd:(kh + 1) * hd]) * SWA_SCALE
            lg = jnp.where(valid, lg, NEG_INF)
            sink = sink_ref[kh][:, 0:1]
            m = jnp.maximum(jnp.max(lg, axis=-1, keepdims=True), sink)
            p = jnp.exp(lg - m)
            denom = jnp.sum(p, axis=-1, keepdims=True) + jnp.exp(sink - m)
            o_ref[s, kh] = (_dot(p.astype(BF16), v[:, kh * hd:(kh + 1) * hd]) / denom).astype(o_ref.dtype)
        return carry

    lax.fori_loop(0, seqs, body, 0)


def _swa_sample(q, k_new, v_new, cache_k, cache_v, sink_rows):
    b, kvh, rows, hd = q.shape
    n_new = k_new.shape[1]
    w, kw = cache_k.shape[1], cache_k.shape[2]
    assert w == WINDOW
    seqs = _pick(b, (8, 4, 2, 1))
    return pl.pallas_call(
        functools.partial(_swa_sample_kernel, n_new=n_new, seqs=seqs),
        grid=(b // seqs,),
        in_specs=[pl.BlockSpec((seqs, kvh, rows, hd), lambda i: (i, 0, 0, 0)),
                  pl.BlockSpec((seqs, n_new, kw), lambda i: (i, 0, 0)),
                  pl.BlockSpec((seqs, n_new, kw), lambda i: (i, 0, 0)),
                  pl.BlockSpec((seqs, w, kw), lambda i: (i, 0, 0)),
                  pl.BlockSpec((seqs, w, kw), lambda i: (i, 0, 0)),
                  pl.BlockSpec((kvh, rows, LANES), lambda i: (0, 0, 0))],
        out_specs=[pl.BlockSpec((seqs, kvh, rows, hd), lambda i: (i, 0, 0, 0)),
                   pl.BlockSpec((seqs, w, kw), lambda i: (i, 0, 0)),
                   pl.BlockSpec((seqs, w, kw), lambda i: (i, 0, 0))],
        out_shape=[jax.ShapeDtypeStruct((b, kvh, rows, hd), BF16),
                   jax.ShapeDtypeStruct((b, w, kw), F32),
                   jax.ShapeDtypeStruct((b, w, kw), F32)],
        scratch_shapes=[pltpu.VMEM((2 * w, kw), F32), pltpu.VMEM((2 * w, kw), F32)],
        compiler_params=_cparams(("parallel",)),
    )(q, k_new, v_new, cache_k, cache_v, sink_rows)


def _mem_heads(q, k_of, v_of):
    hd = MEM_HEAD_DIM
    outs = []
    for h in range(MEM_HEADS):
        s = _dot_t(q[:, h * hd:(h + 1) * hd], k_of(h)) * MEM_SCALE
        m = jnp.max(s, axis=-1, keepdims=True)
        p = jnp.exp(s - m)
        denom = jnp.sum(p, axis=-1, keepdims=True)
        outs.append(_dot(p.astype(BF16), v_of(h)) / denom)
    return jnp.concatenate(outs, axis=1)


def _mem_prompt_kernel(q_ref, k_ref, v_ref, o_ref):
    hd = MEM_HEAD_DIM
    k = k_ref[...].astype(BF16)
    v = v_ref[...].astype(BF16)
    o_ref[...] = _mem_heads(q_ref[...], lambda h: k[:, h * hd:(h + 1) * hd],
                            lambda h: v[:, h * hd:(h + 1) * hd]).astype(o_ref.dtype)


def _mem_prompt(q, kv, batch, seq):
    width = MEM_HEADS * MEM_HEAD_DIM
    m = kv.shape[0] // batch
    tq = _pick(seq, (512, 256, 128))
    nq = seq // tq
    return pl.pallas_call(
        _mem_prompt_kernel,
        grid=(batch, nq),
        in_specs=[pl.BlockSpec((tq, width), lambda b, i: (b * nq + i, 0)),
                  pl.BlockSpec((m, width), lambda b, i: (b, 0)),
                  pl.BlockSpec((m, width), lambda b, i: (b, 1))],
        out_specs=pl.BlockSpec((tq, width), lambda b, i: (b * nq + i, 0)),
        out_shape=jax.ShapeDtypeStruct((batch * seq, width), BF16),
        compiler_params=_cparams(("parallel", "arbitrary")),
    )(q, kv, kv)


def _mem_sample_kernel(q_ref, k_ref, v_ref, o_ref, *, seqs, m):
    nh = MEM_HEADS

    def body(s, carry):
        o_ref[s] = _mem_heads(q_ref[s],
                              lambda h: k_ref[s, pl.ds(h, m, stride=nh), :].astype(BF16),
                              lambda h: v_ref[s, pl.ds(h, m, stride=nh), :].astype(BF16)).astype(o_ref.dtype)
        return carry

    lax.fori_loop(0, seqs, body, 0)


def _mem_sample(q, k, v, layer):
    b, rows, width = q.shape
    mh, hd = k.shape[1], k.shape[2]
    seqs = _pick(b, (8, 4, 2, 1))
    nb = b // seqs
    return pl.pallas_call(
        functools.partial(_mem_sample_kernel, seqs=seqs, m=mh // MEM_HEADS),
        grid=(nb,),
        in_specs=[pl.BlockSpec((seqs, rows, width), lambda i: (i, 0, 0)),
                  pl.BlockSpec((seqs, mh, hd), lambda i: (layer * nb + i, 0, 0)),
                  pl.BlockSpec((seqs, mh, hd), lambda i: (layer * nb + i, 0, 0))],
        out_specs=pl.BlockSpec((seqs, rows, width), lambda i: (i, 0, 0)),
        out_shape=jax.ShapeDtypeStruct((b, rows, width), BF16),
        compiler_params=_cparams(("parallel",)),
    )(q, k, v)


def _sb_local(z, mask, tri2):
    sp = jnp.maximum(z, 0.0) + jnp.log(1.0 + jnp.exp(-jnp.abs(z)))
    spm = sp if mask is None else jnp.where(mask, sp, 0.0)
    hi = spm.astype(BF16)
    lo = (spm - hi.astype(F32)).astype(BF16)
    cum = _dot(hi, tri2) + _dot(lo, tri2)
    kb = z.shape[1]
    a = z - sp - cum[:, :kb]
    if mask is not None:
        a = jnp.where(mask, a, NEG_INF)
    return a, -cum[:, kb:]


def _tri2(kb):
    r = lax.broadcasted_iota(jnp.int32, (kb, 2 * kb), 0)
    cidx = lax.broadcasted_iota(jnp.int32, (kb, 2 * kb), 1)
    return jnp.where((cidx >= kb) | (r > cidx), 1.0, 0.0).astype(BF16)


def _sb_prompt_kernel(bias_ref, q_ref, k_ref, v_ref, o_ref, kb_scr, vb_scr, c_scr, acc_scr):
    kh = pl.program_id(1)
    n = pl.program_id(2)
    blk = SB_BLOCK
    hd = SB_HEAD_DIM

    @pl.when(n == 0)
    def _():
        kb_scr[...] = k_ref[...].astype(BF16)
        vb_scr[...] = v_ref[...].astype(BF16)

    q = q_ref[...]
    qs = jnp.concatenate([q[:, g * hd:(g + 1) * hd] for g in range(SB_GROUP)], axis=0).astype(BF16)
    bias = jnp.concatenate(
        [jnp.full((blk, LANES), bias_ref[kh * SB_GROUP + g], F32) for g in range(SB_GROUP)], axis=0)
    tri2 = _tri2(blk)
    rows = SB_GROUP * blk
    qi = lax.broadcasted_iota(jnp.int32, (rows, blk), 0) % blk
    sj = lax.broadcasted_iota(jnp.int32, (rows, blk), 1)
    diag_mask = sj < qi

    def local(kblk, mask):
        start = pl.multiple_of(kblk * blk, blk)
        z = _dot_t(qs, kb_scr[pl.ds(start, blk), :]) * SB_SCALE + bias
        return _sb_local(z, mask, tri2)

    def accumulate(kblk, a, tot):
        start = pl.multiple_of(kblk * blk, blk)
        wgt = jnp.exp(a + c_scr[...])
        acc_scr[...] += _dot(wgt.astype(BF16), vb_scr[pl.ds(start, blk), :])
        c_scr[...] += tot

    c_scr[...] = jnp.zeros_like(c_scr)
    acc_scr[...] = jnp.zeros_like(acc_scr)
    accumulate(n, *local(n, diag_mask))

    def body(i, carry):
        k1 = n - 1 - 2 * i
        l1 = local(k1, None)
        l2 = local(k1 - 1, None)
        accumulate(k1, *l1)
        accumulate(k1 - 1, *l2)
        return carry

    lax.fori_loop(0, n // 2, body, 0)

    @pl.when(n % 2 == 1)
    def _():
        accumulate(0, *local(0, None))

    acc = acc_scr[...]
    o_ref[...] = jnp.concatenate([acc[g * blk:(g + 1) * blk] for g in range(SB_GROUP)],
                                 axis=1).astype(o_ref.dtype)


def _sb_prompt(qkv, bias, batch, seq):
    hd = SB_HEAD_DIM
    nb = seq // SB_BLOCK
    gw = SB_GROUP * hd
    kcol0 = SB_HEADS
    vcol0 = SB_HEADS + SB_KV_HEADS
    return pl.pallas_call(
        _sb_prompt_kernel,
        grid=(batch, SB_KV_HEADS, nb),
        in_specs=[pl.BlockSpec(memory_space=pltpu.SMEM),
                  pl.BlockSpec((SB_BLOCK, gw), lambda b, kh, n: (b * nb + n, kh)),
                  pl.BlockSpec((seq, hd), lambda b, kh, n: (b, kcol0 + kh)),
                  pl.BlockSpec((seq, hd), lambda b, kh, n: (b, vcol0 + kh))],
        out_specs=pl.BlockSpec((SB_BLOCK, gw), lambda b, kh, n: (b * nb + n, kh)),
        out_shape=jax.ShapeDtypeStruct((batch * seq, SB_HEADS * hd), BF16),
        scratch_shapes=[pltpu.VMEM((seq, hd), BF16), pltpu.VMEM((seq, hd), BF16),
                        pltpu.VMEM((SB_GROUP * SB_BLOCK, LANES), F32),
                        pltpu.VMEM((SB_GROUP * SB_BLOCK, hd), F32)],
        compiler_params=_cparams(("parallel", "parallel", "arbitrary")),
    )(bias, qkv, qkv, qkv)


def _sb_sample_kernel(pt_ref, q_ref, kn_ref, vn_ref, bias_ref, *rest, n_new, pages_per_step):
    k_refs = rest[:pages_per_step]
    v_refs = rest[pages_per_step:2 * pages_per_step]
    o_ref, kpad, vpad, c_scr, acc_scr = rest[2 * pages_per_step:]
    del pt_ref
    step = pl.program_id(1)
    nkv = SB_KV_HEADS
    page = SB_BLOCK
    rows = SB_GROUP * n_new
    tri2 = _tri2(page)
    q = q_ref[0]
    bias = bias_ref[...]

    def head(ref, kh):
        return ref[pl.ds(kh, page, stride=nkv), :].astype(BF16)

    def tiles(page_refs, mask):
        npg = len(page_refs)
        z = jnp.concatenate(
            [_dot_t(q[kh * rows:(kh + 1) * rows], head(k_ref, kh))
             for k_ref, _ in page_refs for kh in range(nkv)], axis=0)
        z = z * SB_SCALE + jnp.concatenate([bias] * npg, axis=0)
        a, tot = _sb_local(z, mask, tri2)
        wb = jnp.exp(a).astype(BF16)
        qr = nkv * rows
        off = c_scr[...]
        acc = acc_scr[...]
        for p, (_, v_ref) in enumerate(page_refs):
            o_p = jnp.concatenate(
                [_dot(wb[p * qr + kh * rows:p * qr + (kh + 1) * rows], head(v_ref, kh))
                 for kh in range(nkv)], axis=0)
            acc = acc + jnp.exp(off) * o_p
            off = off + tot[p * qr:(p + 1) * qr]
        acc_scr[...] = acc
        c_scr[...] = off

    @pl.when(step == 0)
    def _():
        c_scr[...] = jnp.zeros_like(c_scr)
        acc_scr[...] = jnp.zeros_like(acc_scr)
        kpad[...] = jnp.zeros_like(kpad)
        vpad[...] = jnp.zeros_like(vpad)
        kpad[0:nkv * n_new, :] = kn_ref[0]
        vpad[0:nkv * n_new, :] = vn_ref[0]
        r = lax.broadcasted_iota(jnp.int32, (nkv * rows, page), 0)
        j = lax.broadcasted_iota(jnp.int32, (nkv * rows, page), 1)
        tiles([(kpad, vpad)], j < (r % n_new))

    tiles([(k_refs[i].at[0], v_refs[i].at[0]) for i in range(pages_per_step)], None)

    @pl.when(step == pl.num_programs(1) - 1)
    def _():
        o_ref[0] = acc_scr[...].astype(o_ref.dtype)


def _sb_sample(q, k_new, v_new, k_pool, v_pool, page_table, bias_rows, page0):
    b, qrows, hd = q.shape
    new_rows = k_new.shape[1]
    n_new = new_rows // SB_KV_HEADS
    n_pages = page_table.shape[1]
    prow = k_pool.shape[1]
    assert prow == SB_BLOCK * SB_KV_HEADS
    pps = _pick(n_pages, (8, 4, 2, 1))
    steps = n_pages // pps

    def page_map(i):
        return lambda s, c, pt: (page0 + pt[s * n_pages + (n_pages - 1 - (c * pps + i))], 0, 0)

    pool_specs = [pl.BlockSpec((1, prow, hd), page_map(i)) for i in range(pps)]
    grid_spec = pltpu.PrefetchScalarGridSpec(
        num_scalar_prefetch=1,
        grid=(b, steps),
        in_specs=[pl.BlockSpec((1, qrows, hd), lambda s, c, pt: (s, 0, 0)),
                  pl.BlockSpec((1, new_rows, hd), lambda s, c, pt: (s, 0, 0)),
                  pl.BlockSpec((1, new_rows, hd), lambda s, c, pt: (s, 0, 0)),
                  pl.BlockSpec((qrows, LANES), lambda s, c, pt: (0, 0))] + pool_specs + pool_specs,
        out_specs=pl.BlockSpec((1, qrows, hd), lambda s, c, pt: (s, 0, 0)),
        scratch_shapes=[pltpu.VMEM((prow, hd), F32), pltpu.VMEM((prow, hd), F32),
                        pltpu.VMEM((qrows, LANES), F32), pltpu.VMEM((qrows, hd), F32)],
    )
    return pl.pallas_call(
        functools.partial(_sb_sample_kernel, n_new=n_new, pages_per_step=pps),
        grid_spec=grid_spec,
        out_shape=jax.ShapeDtypeStruct((b, qrows, hd), BF16),
        compiler_params=_cparams(("parallel", "arbitrary")),
    )(page_table.reshape(-1), q, k_new, v_new, bias_rows, *([k_pool] * pps), *([v_pool] * pps))


def _router_kernel(x_ref, g_ref, w_ref, b_ref, h_ref, info_ref):
    xv = x_ref[...]
    ms = jnp.mean(xv * xv, axis=-1, keepdims=True)
    h = xv * lax.rsqrt(ms + RMS_EPS) * g_ref[...]
    h_ref[...] = h
    logits = jnp.dot(h, w_ref[...], preferred_element_type=F32,
                     precision=lax.Precision.HIGHEST) + b_ref[...]
    lane = lax.broadcasted_iota(jnp.int32, logits.shape, 1)
    big = jnp.int32(1 << 20)
    is_g = (lane >= N_EXPERTS) & (lane < N_EXPERTS + N_GROUPS)
    gl = jnp.where(is_g, logits, -jnp.inf)
    gmax = jnp.max(gl, axis=-1, keepdims=True)
    gidx = jnp.min(jnp.where(gl == gmax, lane - N_EXPERTS, big), axis=-1, keepdims=True)
    p_group = 1.0 / jnp.sum(jnp.exp(gl - gmax), axis=-1, keepdims=True)
    lo = gidx * EXPERTS_PER_GROUP
    in_grp = (lane >= lo) & (lane < lo + EXPERTS_PER_GROUP)
    el = jnp.where(in_grp, logits, -jnp.inf)
    emax = jnp.max(el, axis=-1, keepdims=True)
    pe = jnp.exp(el - emax)
    prob = pe / jnp.sum(pe, axis=-1, keepdims=True)
    prob = jnp.where(in_grp, prob, -1.0)
    p1 = jnp.max(prob, axis=-1, keepdims=True)
    i1 = jnp.min(jnp.where(prob == p1, lane, big), axis=-1, keepdims=True)
    rest = jnp.where(lane == i1, -1.0, prob)
    p2 = jnp.max(rest, axis=-1, keepdims=True)
    i2 = jnp.min(jnp.where(rest == p2, lane, big), axis=-1, keepdims=True)
    tot = p1 + p2
    w1 = p1 / tot * p_group
    w2 = p2 / tot * p_group
    info = jnp.where(lane == 0, i1.astype(F32),
                     jnp.where(lane == 1, i2.astype(F32),
                               jnp.where(lane == 2, w1, jnp.where(lane == 3, w2, 0.0))))
    info_ref[...] = info


def _router(x, g, w_router, b_router, w_group, b_group):
    t, d = x.shape
    pad = LANES - N_EXPERTS - N_GROUPS
    w = jnp.concatenate([w_router, w_group, jnp.zeros((d, pad), F32)], axis=1)
    bias = jnp.concatenate([b_router, b_group, jnp.zeros((pad,), F32)]).reshape(1, LANES)
    tm = _pick(t, (256, 128, 8))
    return pl.pallas_call(
        _router_kernel,
        grid=(t // tm,),
        in_specs=[pl.BlockSpec((tm, d), lambda i: (i, 0)),
                  pl.BlockSpec((1, d), lambda i: (0, 0)),
                  pl.BlockSpec((d, LANES), lambda i: (0, 0)),
                  pl.BlockSpec((1, LANES), lambda i: (0, 0))],
        out_specs=[pl.BlockSpec((tm, d), lambda i: (i, 0)),
                   pl.BlockSpec((tm, LANES), lambda i: (i, 0))],
        out_shape=[jax.ShapeDtypeStruct((t, d), F32), jax.ShapeDtypeStruct((t, LANES), F32)],
        compiler_params=_cparams(("parallel",)),
    )(x, g.reshape(1, d), w, bias)


def _moe_kernel(te_ref, r0_ref, nv_ref, nused_ref, tok_ref, dst_ref, h_hbm, wg_ref, wu_ref, wd_ref, y_hbm,
                xbuf, ybuf, gsem, ssem, *, tm, n_pairs):
    del te_ref
    i = pl.program_id(0)
    n_tiles = pl.num_programs(0)
    slot = i % 2
    n_valid = nv_ref[i]
    n_used = nused_ref[0]

    def start_gather(tile, sl):
        base = r0_ref[tile]
        for r in range(tm):
            pltpu.make_async_copy(h_hbm.at[pl.ds(tok_ref[base + r], 1)],
                                  xbuf.at[sl, pl.ds(r, 1)], gsem.at[sl]).start()

    def wait_gather(sl):
        pltpu.make_async_copy(h_hbm.at[pl.ds(0, tm)], xbuf.at[sl], gsem.at[sl]).wait()

    def start_scatter(sl):
        base = r0_ref[i]
        spare = n_pairs + sl * tm
        for r in range(tm):
            dst = jnp.where(r < n_valid, dst_ref[base + r], spare + r)
            pltpu.make_async_copy(ybuf.at[sl, pl.ds(r, 1)], y_hbm.at[pl.ds(dst, 1)], ssem.at[sl]).start()

    def wait_scatter(sl):
        pltpu.make_async_copy(ybuf.at[sl], y_hbm.at[pl.ds(0, tm)], ssem.at[sl]).wait()

    @pl.when(i == 0)
    def _():
        ybuf[0] = jnp.zeros(ybuf.shape[1:], ybuf.dtype)
        for sl in range(2):
            pltpu.make_async_copy(ybuf.at[0], y_hbm.at[pl.ds(n_pairs + sl * tm, tm)], ssem.at[sl]).start()
        for sl in range(2):
            wait_scatter(sl)
        start_gather(0, 0)

    @pl.when(n_valid > 0)
    def _():
        wait_gather(slot)

        @pl.when(i + 1 < n_used)
        def _():
            start_gather(i + 1, 1 - slot)

        @pl.when(i >= 2)
        def _():
            wait_scatter(slot)

        x = xbuf[slot].astype(BF16)
        a = _dot(x, wg_ref[0].astype(BF16))
        u = _dot(x, wu_ref[0].astype(BF16))
        act = a * jax.nn.sigmoid(a) * u
        ybuf[slot] = _dot(act.astype(BF16), wd_ref[0].astype(BF16))
        start_scatter(slot)

    @pl.when(i == n_tiles - 1)
    def _():
        last = n_used - 1
        wait_scatter(last % 2)

        @pl.when(last >= 1)
        def _():
            wait_scatter(1 - last % 2)


def _lookup(table, idx):
    n = table.shape[0]
    hit = idx[:, None] == jnp.arange(n, dtype=jnp.int32)[None, :]
    return jnp.sum(jnp.where(hit, table[None, :], 0), axis=1)


def _moe(h, info, w_gate, w_up, w_down, layer):
    t, d = h.shape
    n_exp, ff = N_EXPERTS, w_gate.shape[-1]
    tm = MOE_TILE
    pairs = 2 * t
    n_tiles = -(-pairs // tm) + n_exp
    eid = jnp.concatenate([info[:, 0], info[:, 1]]).astype(jnp.int32)
    sorted_eid, order = lax.sort_key_val(eid, jnp.arange(pairs, dtype=jnp.int32))
    bounds = jnp.sum((sorted_eid[None, :] < jnp.arange(n_exp + 1, dtype=jnp.int32)[:, None]).astype(jnp.int32),
                     axis=1)
    starts, counts = bounds[:-1], bounds[1:] - bounds[:-1]
    tiles_e = (counts + tm - 1) // tm
    tile_end = jnp.cumsum(tiles_e)
    n_used = tile_end[-1]
    tile = jnp.arange(n_tiles, dtype=jnp.int32)
    tile_e = jnp.minimum(jnp.sum((tile[:, None] >= tile_end[None, :]).astype(jnp.int32), axis=1), n_exp - 1)
    used = tile < n_used
    k_in_e = tile - _lookup(tile_end - tiles_e, tile_e)
    tile_r0 = jnp.where(used, _lookup(starts, tile_e) + k_in_e * tm, 0).astype(jnp.int32)
    tile_valid = jnp.where(used, jnp.clip(_lookup(counts, tile_e) - k_in_e * tm, 0, tm), 0).astype(jnp.int32)
    last_e = jnp.sum(jnp.where(tile == n_used - 1, tile_e, 0))
    tile_e = jnp.where(used, tile_e, last_e).astype(jnp.int32)
    tok_sorted = jnp.pad(order % t, (0, tm))
    dst_sorted = jnp.pad(order, (0, tm))

    wg = w_gate.reshape((-1,) + w_gate.shape[-2:])
    wu = w_up.reshape((-1,) + w_up.shape[-2:])
    wd = w_down.reshape((-1,) + w_down.shape[-2:])
    e0 = layer * n_exp

    def w_map(i, te, *_):
        return (e0 + te[i], 0, 0)

    grid_spec = pltpu.PrefetchScalarGridSpec(
        num_scalar_prefetch=6,
        grid=(n_tiles,),
        in_specs=[pl.BlockSpec(memory_space=pl.ANY),
                  pl.BlockSpec((1, d, ff), w_map),
                  pl.BlockSpec((1, d, ff), w_map),
                  pl.BlockSpec((1, ff, d), w_map)],
        out_specs=pl.BlockSpec(memory_space=pl.ANY),
        scratch_shapes=[pltpu.VMEM((2, tm, d), F32), pltpu.VMEM((2, tm, d), F32),
                        pltpu.SemaphoreType.DMA((2,)), pltpu.SemaphoreType.DMA((2,))],
    )
    return pl.pallas_call(
        functools.partial(_moe_kernel, tm=tm, n_pairs=pairs),
        grid_spec=grid_spec,
        out_shape=jax.ShapeDtypeStruct((pairs + 2 * tm, d), F32),
        compiler_params=_cparams(("arbitrary",)),
    )(tile_e, tile_r0, tile_valid, n_used.reshape(1).astype(jnp.int32), tok_sorted, dst_sorted, h, wg, wu, wd)


def _rope_tables(pos):
    half = ROPE_DIM // 2
    inv_freq = ROPE_THETA ** (-2.0 * jnp.arange(half, dtype=F32) / ROPE_DIM)
    ang = pos.astype(F32)[:, None] * inv_freq[None, :]
    cos, sin = jnp.cos(ang), jnp.sin(ang)
    t = pos.shape[0]
    ones = jnp.ones((t, SWA_HEAD_DIM - ROPE_DIM), F32)
    zeros = jnp.zeros((t, half), F32)
    zrest = jnp.zeros((t, SWA_HEAD_DIM - ROPE_DIM), F32)
    c_head = jnp.concatenate([cos, cos, ones], axis=1)
    plus_head = jnp.concatenate([zeros, sin, zrest], axis=1)
    minus_head = jnp.concatenate([-sin, zeros, zrest], axis=1)
    rep = LANES // SWA_HEAD_DIM
    return tuple(jnp.concatenate([a] * rep, axis=1) for a in (c_head, plus_head, minus_head))


def kernel(x_prompt, x_sample, mem_prompt, cache_swa_k, cache_swa_v, cache_sb_k, cache_sb_v, cache_mem_k, cache_mem_v, page_table, norm_mix, w_in_swa, sinks_swa, w_out_swa, w_in_sb, sb_bias, w_out_sb, norm_mem_q, norm_mem_kv, w_mem_q, w_mem_kv, w_mem_o, norm_ffn, w_group, b_group, w_router, b_router, w_gate, w_up, w_down, norm_final):
    b_p, s_p, d = x_prompt.shape
    b_s, n_s, _ = x_sample.shape
    depth = norm_mix.shape[0]
    t_p, t_s = b_p * s_p, b_s * n_s
    t = t_p + t_s
    past_len = page_table.shape[1] * cache_sb_k.shape[2]
    assert s_p >= WINDOW and s_p % WINDOW == 0

    x = jnp.concatenate([x_prompt.reshape(t_p, d), x_sample.reshape(t_s, d)], axis=0)
    pos = jnp.concatenate([jnp.tile(jnp.arange(s_p), b_p), jnp.tile(past_len + jnp.arange(n_s), b_s)])
    rope = _rope_tables(pos)
    mem_flat = mem_prompt.reshape(-1, d)
    mem_m = mem_prompt.shape[1]
    mem_w = MEM_HEADS * MEM_HEAD_DIM
    sb_pool_k = cache_sb_k.reshape(-1, cache_sb_k.shape[2] * SB_KV_HEADS, SB_HEAD_DIM)
    sb_pool_v = cache_sb_v.reshape(-1, cache_sb_v.shape[2] * SB_KV_HEADS, SB_HEAD_DIM)
    mem_cache_k = cache_mem_k.reshape(-1, cache_mem_k.shape[2] * MEM_HEADS, MEM_HEAD_DIM)
    mem_cache_v = cache_mem_v.reshape(-1, cache_mem_v.shape[2] * MEM_HEADS, MEM_HEAD_DIM)

    swa_kp, swa_vp, swa_ks, swa_vs = [], [], [], []
    sb_kp, sb_vp, sb_ks, sb_vs = [], [], [], []
    mem_kp, mem_vp = [], []
    moe = None
    for i in range(depth):
        j = i // 2
        first = moe is None
        if i % 2 == 0:
            nq = SWA_HEADS * SWA_HEAD_DIM
            nk = SWA_KV_HEADS * SWA_HEAD_DIM
            res = _norm_matmul(x, norm_mix[i], w_in_swa[j], n_rows=t, moe=moe, rope=rope,
                               rope_cols=nq + nk, emit_x=not first)
            qkv, x = (res, x) if first else res
            o_p = _swa_prompt(qkv, sinks_swa[j], b_p, s_p)
            qs = qkv[t_p:, :nq].reshape(b_s, n_s, SWA_KV_HEADS, SWA_GROUP, SWA_HEAD_DIM)
            qs = qs.transpose(0, 2, 3, 1, 4).reshape(b_s, SWA_KV_HEADS, SWA_GROUP * n_s, SWA_HEAD_DIM)
            kn = qkv[t_p:, nq:nq + nk].reshape(b_s, n_s, nk)
            vn = qkv[t_p:, nq + nk:].reshape(b_s, n_s, nk)
            sink_rows = jnp.broadcast_to(
                jnp.repeat(sinks_swa[j].reshape(SWA_KV_HEADS, SWA_GROUP), n_s, axis=1)[:, :, None],
                (SWA_KV_HEADS, SWA_GROUP * n_s, LANES))
            o_s, kbuf, vbuf = _swa_sample(qs, kn, vn, cache_swa_k[j].reshape(b_s, -1, nk),
                                          cache_swa_v[j].reshape(b_s, -1, nk), sink_rows)
            o_s = o_s.reshape(b_s, SWA_KV_HEADS, SWA_GROUP, n_s, SWA_HEAD_DIM)
            o_s = o_s.transpose(0, 3, 1, 2, 4).reshape(t_s, nq)
            kp = qkv[:t_p, nq:nq + nk].reshape(b_p, s_p, SWA_KV_HEADS, SWA_HEAD_DIM)
            vp = qkv[:t_p, nq + nk:].reshape(b_p, s_p, SWA_KV_HEADS, SWA_HEAD_DIM)
            swa_kp.append(kp[:, -WINDOW:])
            swa_vp.append(vp[:, -WINDOW:])
            swa_ks.append(kbuf.reshape(b_s, -1, SWA_KV_HEADS, SWA_HEAD_DIM))
            swa_vs.append(vbuf.reshape(b_s, -1, SWA_KV_HEADS, SWA_HEAD_DIM))
            w_out = w_out_swa[j]
        else:
            nq = SB_HEADS * SB_HEAD_DIM
            nk = SB_KV_HEADS * SB_HEAD_DIM
            res = _norm_matmul(x, norm_mix[i], w_in_sb[j], n_rows=t, moe=moe, emit_x=not first)
            qkv, x = (res, x) if first else res
            o_p = _sb_prompt(qkv, sb_bias[j], b_p, s_p)
            qs = qkv[t_p:, :nq].reshape(b_s, n_s, SB_KV_HEADS, SB_GROUP, SB_HEAD_DIM)
            qs = qs.transpose(0, 2, 3, 1, 4).reshape(b_s, SB_HEADS * n_s, SB_HEAD_DIM).astype(BF16)
            kn = qkv[t_p:, nq:nq + nk].reshape(b_s, n_s * SB_KV_HEADS, SB_HEAD_DIM)
            vn = qkv[t_p:, nq + nk:].reshape(b_s, n_s * SB_KV_HEADS, SB_HEAD_DIM)
            bias_rows = jnp.broadcast_to(jnp.repeat(sb_bias[j], n_s)[:, None], (SB_HEADS * n_s, LANES))
            o_s = _sb_sample(qs, kn, vn, sb_pool_k, sb_pool_v, page_table, bias_rows,
                             j * cache_sb_k.shape[1])
            o_s = o_s.reshape(b_s, SB_KV_HEADS, SB_GROUP, n_s, SB_HEAD_DIM)
            o_s = o_s.transpose(0, 3, 1, 2, 4).reshape(t_s, nq)
            sb_kp.append(qkv[:t_p, nq:nq + nk].reshape(b_p, s_p, SB_KV_HEADS, SB_HEAD_DIM))
            sb_vp.append(qkv[:t_p, nq + nk:].reshape(b_p, s_p, SB_KV_HEADS, SB_HEAD_DIM))
            sb_ks.append(kn.reshape(b_s, n_s, SB_KV_HEADS, SB_HEAD_DIM))
            sb_vs.append(vn.reshape(b_s, n_s, SB_KV_HEADS, SB_HEAD_DIM))
            w_out = w_out_sb[j]
        x = _resid_matmul(jnp.concatenate([o_p, o_s], axis=0), w_out, x)

        mkv = _norm_matmul(mem_flat, norm_mem_kv[i], w_mem_kv[i])
        mem_kp.append(mkv[:, :mem_w].reshape(b_p, mem_m, MEM_HEADS, MEM_HEAD_DIM))
        mem_vp.append(mkv[:, mem_w:].reshape(b_p, mem_m, MEM_HEADS, MEM_HEAD_DIM))
        qm = _norm_matmul(x, norm_mem_q[i], w_mem_q[i], out_dtype=BF16)
        om_p = _mem_prompt(qm, mkv, b_p, s_p)
        row_pad = 16 - n_s % 16 if n_s % 16 else 0
        qm_s = jnp.pad(qm[t_p:].reshape(b_s, n_s, mem_w), ((0, 0), (0, row_pad), (0, 0)))
        om_s = _mem_sample(qm_s, mem_cache_k, mem_cache_v, i)
        om = jnp.concatenate([om_p, om_s[:, :n_s].reshape(t_s, mem_w)], axis=0)
        x = _resid_matmul(om, w_mem_o[i], x)

        h, info = _router(x, norm_ffn[i], w_router[i], b_router[i], w_group[i], b_group[i])
        moe = (_moe(h, info, w_gate, w_up, w_down, i), info)

    y_prompt = _final_norm(x, moe[0], moe[1], norm_final, 0, t_p).reshape(b_p, s_p, d)
    y_sample = _final_norm(x, moe[0], moe[1], norm_final, t_p, t_s).reshape(b_s, n_s, d)
    return (y_prompt, y_sample,
            jnp.stack(swa_kp), jnp.stack(swa_vp), jnp.stack(swa_ks), jnp.stack(swa_vs),
            jnp.stack(sb_kp), jnp.stack(sb_vp), jnp.stack(sb_ks), jnp.stack(sb_vs),
            jnp.stack(mem_kp), jnp.stack(mem_vp))
```

```python
import functools
import math

import jax
import jax.numpy as jnp
from jax import lax
from jax.experimental import pallas as pl
from jax.experimental.pallas import tpu as pltpu

F32 = jnp.float32
BF16 = jnp.bfloat16

SWA_HEADS, SWA_KV_HEADS, SWA_HEAD_DIM = 32, 4, 64
SWA_GROUP = SWA_HEADS // SWA_KV_HEADS
WINDOW = 128
ROPE_THETA = 500000.0
ROPE_DIM = SWA_HEAD_DIM // 4
SWA_SCALE = SWA_HEAD_DIM ** -0.5
SB_HEADS, SB_KV_HEADS, SB_HEAD_DIM = 16, 4, 128
SB_GROUP = SB_HEADS // SB_KV_HEADS
SB_BLOCK = 128
SB_SCALE = SB_HEAD_DIM ** -0.5
MEM_HEADS, MEM_HEAD_DIM = 4, 128
MEM_SCALE = MEM_HEAD_DIM ** -0.5
N_GROUPS, EXPERTS_PER_GROUP = 4, 8
N_EXPERTS = N_GROUPS * EXPERTS_PER_GROUP
RMS_EPS = 1e-6
NEG_INF = -1e30

LANES = 128
VMEM_LIMIT = 56 * 1024 * 1024
MOE_TILE = 256
SB_PAGES_PER_STEP = 16
SB_PAGE_CHAIN = 4


def _cparams(sem):
    return pltpu.CompilerParams(dimension_semantics=sem, vmem_limit_bytes=VMEM_LIMIT)


def _pick(n, prefs):
    for p in prefs:
        if n % p == 0:
            return p
    return n


def _dot_t(a, b):
    return lax.dot_general(a, b, (((1,), (1,)), ((), ())), preferred_element_type=F32)


def _dot(a, b):
    return jnp.dot(a, b, preferred_element_type=F32)


def _rows(refs, n_first):
    if len(refs) == 1:
        return refs[0][...]
    return jnp.where(pl.program_id(0) < n_first, refs[0][...], refs[1][...])


def _row_specs(srcs, tm, n_first):
    if not isinstance(srcs, tuple):
        return [pl.BlockSpec((tm, srcs.shape[1]), lambda i, *_: (i, 0))], [srcs]
    width = srcs[0].shape[1]
    return ([pl.BlockSpec((tm, width), lambda i, *_: (jnp.minimum(i, n_first - 1), 0)),
             pl.BlockSpec((tm, width), lambda i, *_: (jnp.maximum(i - n_first, 0), 0))], list(srcs))


def _moe_combine(xv, ya_ref, yb_ref, info_ref):
    info = info_ref[...]
    return xv + ya_ref[...] * info[:, 2:3] + yb_ref[...] * info[:, 3:4]


def _norm_matmul_kernel(*refs, n_x, n_first, add, emit_x, rope_cols, tn):
    it = iter(refs)
    x_refs = [next(it) for _ in range(n_x)]
    ya_ref = next(it) if add else None
    yb_ref = next(it) if add else None
    info_ref = next(it) if add else None
    g_ref = next(it)
    w_ref = next(it)
    if rope_cols:
        cos_ref, sp_ref, sm_ref = next(it), next(it), next(it)
    o_ref = next(it)
    xo_ref = next(it) if emit_x else None
    h_scr = next(it)
    j = pl.program_id(1)

    @pl.when(j == 0)
    def _():
        xv = _rows(x_refs, n_first)
        if add:
            xv = _moe_combine(xv, ya_ref, yb_ref, info_ref)
        if emit_x:
            xo_ref[...] = xv
        ms = jnp.mean(xv * xv, axis=-1, keepdims=True)
        h_scr[...] = (xv * lax.rsqrt(ms + RMS_EPS) * g_ref[...]).astype(BF16)

    acc = _dot(h_scr[...], w_ref[...])
    if not rope_cols:
        o_ref[...] = acc.astype(o_ref.dtype)
        return

    @pl.when(j * tn < rope_cols)
    def _():
        reps = tn // LANES
        cos = jnp.concatenate([cos_ref[...]] * reps, axis=1)
        s_plus = jnp.concatenate([sp_ref[...]] * reps, axis=1)
        s_minus = jnp.concatenate([sm_ref[...]] * reps, axis=1)
        half = ROPE_DIM // 2
        roped = (acc * cos + pltpu.roll(acc, half, axis=1) * s_plus
                 + pltpu.roll(acc, tn - half, axis=1) * s_minus)
        col = j * tn + lax.broadcasted_iota(jnp.int32, acc.shape, 1)
        o_ref[...] = jnp.where(col < rope_cols, roped, acc).astype(o_ref.dtype)

    @pl.when(j * tn >= rope_cols)
    def _():
        o_ref[...] = acc.astype(o_ref.dtype)


def _norm_matmul(x, g, w, *, moe=None, rope=None, rope_cols=0, emit_x=False,
                 out_dtype=F32):
    pair = isinstance(x, tuple)
    t = x[0].shape[0] + x[1].shape[0] if pair else x.shape[0]
    d = x[0].shape[1] if pair else x.shape[1]
    n = w.shape[1]
    add = moe is not None
    tm = _pick(math.gcd(x[0].shape[0], x[1].shape[0]) if pair else t, (512, 256))
    tn = _pick(n, (1024, 512, 256, 128) if add else (1280, 1024, 512, 256, 128))
    n_i = t // tm
    n_first = x[0].shape[0] // tm if pair else n_i
    in_specs, args = _row_specs(x, tm, n_first)
    n_x = len(args)
    if add:
        in_specs += [pl.BlockSpec((tm, d), lambda i, j: (i, 0)),
                     pl.BlockSpec((tm, d), lambda i, j: (i + n_i, 0)),
                     pl.BlockSpec((tm, LANES), lambda i, j: (i, 0))]
        args += [moe[0], moe[0], moe[1]]
    in_specs += [pl.BlockSpec((1, d), lambda i, j: (0, 0)),
                 pl.BlockSpec((d, tn), lambda i, j: (0, j))]
    args += [g.reshape(1, d), w]
    if rope_cols:
        in_specs += [pl.BlockSpec((tm, LANES), lambda i, j: (i, 0))] * 3
        args += list(rope)
    out_shape = [jax.ShapeDtypeStruct((t, n), out_dtype)]
    out_specs = [pl.BlockSpec((tm, tn), lambda i, j: (i, j))]
    if emit_x:
        out_shape.append(jax.ShapeDtypeStruct((t, d), F32))
        out_specs.append(pl.BlockSpec((tm, d), lambda i, j: (i, 0)))
    res = pl.pallas_call(
        functools.partial(_norm_matmul_kernel, n_x=n_x, n_first=n_first,
                          add=add, emit_x=emit_x, rope_cols=rope_cols, tn=tn),
        grid=(n_i, n // tn),
        in_specs=in_specs,
        out_specs=out_specs,
        out_shape=out_shape,
        scratch_shapes=[pltpu.VMEM((tm, d), BF16)],
        compiler_params=_cparams(("parallel", "arbitrary")),
    )(*args)
    return res if emit_x else res[0]


def _resid_matmul_kernel(*refs, n_a, n_r, n_first):
    a_refs, w_ref, r_refs, o_ref = refs[:n_a], refs[n_a], refs[n_a + 1:n_a + 1 + n_r], refs[-1]
    o_ref[...] = _rows(r_refs, n_first) + _dot(_rows(a_refs, n_first), w_ref[...])


def _resid_matmul(a, w, resid):
    first = a[0] if isinstance(a, tuple) else resid[0] if isinstance(resid, tuple) else None
    t = sum(p.shape[0] for p in a) if isinstance(a, tuple) else a.shape[0]
    n = w.shape[1]
    tm = _pick(t if first is None else math.gcd(first.shape[0], t - first.shape[0]), (512, 256))
    n_first = t // tm if first is None else first.shape[0] // tm
    a_specs, a_args = _row_specs(a, tm, n_first)
    r_specs, r_args = _row_specs(resid, tm, n_first)
    return pl.pallas_call(
        functools.partial(_resid_matmul_kernel, n_a=len(a_args), n_r=len(r_args), n_first=n_first),
        grid=(t // tm,),
        in_specs=a_specs + [pl.BlockSpec(w.shape, lambda i: (0, 0))] + r_specs,
        out_specs=pl.BlockSpec((tm, n), lambda i: (i, 0)),
        out_shape=jax.ShapeDtypeStruct((t, n), F32),
        compiler_params=_cparams(("parallel",)),
    )(*a_args, w, *r_args)


def _final_norm_kernel(x_ref, ya_ref, yb_ref, info_ref, g_ref, o_ref):
    xv = _moe_combine(x_ref[...], ya_ref, yb_ref, info_ref)
    ms = jnp.mean(xv * xv, axis=-1, keepdims=True)
    o_ref[...] = xv * lax.rsqrt(ms + RMS_EPS) * g_ref[...]


def _final_norm(x, y2, info, g, row0, n_rows):
    t, d = x.shape
    tm = _pick(n_rows, (256, 128, 8))
    assert row0 % tm == 0 and t % tm == 0
    b0, bt = row0 // tm, t // tm
    return pl.pallas_call(
        _final_norm_kernel,
        grid=(n_rows // tm,),
        in_specs=[pl.BlockSpec((tm, d), lambda i: (i + b0, 0)),
                  pl.BlockSpec((tm, d), lambda i: (i + b0, 0)),
                  pl.BlockSpec((tm, d), lambda i: (i + b0 + bt, 0)),
                  pl.BlockSpec((tm, LANES), lambda i: (i + b0, 0)),
                  pl.BlockSpec((1, d), lambda i: (0, 0))],
        out_specs=pl.BlockSpec((tm, d), lambda i: (i, 0)),
        out_shape=jax.ShapeDtypeStruct((n_rows, d), F32),
        compiler_params=_cparams(("parallel",)),
    )(x, y2, y2, info, g.reshape(1, d))


def _swa_prompt_kernel(sink_ref, q_ref, kp_ref, kc_ref, vp_ref, vc_ref, o_ref):
    n = pl.program_id(1)
    blk = WINDOW
    qi = lax.broadcasted_iota(jnp.int32, (blk, 2 * blk), 0)
    sj = lax.broadcasted_iota(jnp.int32, (blk, 2 * blk), 1)
    rel = qi + blk - sj
    first_key = jnp.where(n > 0, 0, blk)
    valid = (rel >= 0) & (rel < WINDOW) & (sj >= first_key)
    k = jnp.concatenate([kp_ref[...], kc_ref[...]], axis=0).astype(BF16)
    v = jnp.concatenate([vp_ref[...], vc_ref[...]], axis=0).astype(BF16)
    hd = SWA_HEAD_DIM
    for kh in range(SWA_KV_HEADS):
        k_h = k[:, kh * hd:(kh + 1) * hd]
        v_h = v[:, kh * hd:(kh + 1) * hd]
        for g2 in range(SWA_GROUP // 2):
            pair = []
            for gg in range(2):
                h = kh * SWA_GROUP + g2 * 2 + gg
                q_h = q_ref[:, h * hd:(h + 1) * hd].astype(BF16)
                s = _dot_t(q_h, k_h) * SWA_SCALE
                s = jnp.where(valid, s, NEG_INF)
                sink = sink_ref[h]
                m = jnp.maximum(jnp.max(s, axis=-1, keepdims=True), sink)
                p = jnp.exp(s - m)
                denom = jnp.sum(p, axis=-1, keepdims=True) + jnp.exp(sink - m)
                pair.append(_dot(p.astype(BF16), v_h) / denom)
            h0 = kh * SWA_GROUP + g2 * 2
            o_ref[:, h0 * hd:(h0 + 2) * hd] = jnp.concatenate(pair, axis=1).astype(o_ref.dtype)


def _swa_prompt(qkv, sinks, batch, seq):
    nq = SWA_HEADS * SWA_HEAD_DIM
    nk = SWA_KV_HEADS * SWA_HEAD_DIM
    nb = seq // WINDOW
    kcol, vcol = nq // nk, nq // nk + 1

    def prev(b, n):
        return b * nb + jnp.maximum(n - 1, 0)

    return pl.pallas_call(
        _swa_prompt_kernel,
        grid=(batch, nb),
        in_specs=[pl.BlockSpec(memory_space=pltpu.SMEM),
                  pl.BlockSpec((WINDOW, nq), lambda b, n: (b * nb + n, 0)),
                  pl.BlockSpec((WINDOW, nk), lambda b, n: (prev(b, n), kcol)),
                  pl.BlockSpec((WINDOW, nk), lambda b, n: (b * nb + n, kcol)),
                  pl.BlockSpec((WINDOW, nk), lambda b, n: (prev(b, n), vcol)),
                  pl.BlockSpec((WINDOW, nk), lambda b, n: (b * nb + n, vcol))],
        out_specs=pl.BlockSpec((WINDOW, nq), lambda b, n: (b * nb + n, 0)),
        out_shape=jax.ShapeDtypeStruct((batch * seq, nq), BF16),
        compiler_params=_cparams(("parallel", "arbitrary")),
    )(sinks, qkv, qkv, qkv, qkv, qkv)


def _swa_sample_kernel(q_ref, kn_ref, vn_ref, ck_ref, cv_ref, sink_ref, o_ref, nk_ref, nv_ref,
                       kall, vall, *, n_new, seqs):
    w = WINDOW
    rows = SWA_GROUP * n_new
    hd = SWA_HEAD_DIM
    zeros = jnp.zeros((w, kall.shape[1]), F32)
    kall[w:2 * w, :] = zeros
    vall[w:2 * w, :] = zeros
    r = lax.broadcasted_iota(jnp.int32, (rows, 2 * w), 0)
    j = lax.broadcasted_iota(jnp.int32, (rows, 2 * w), 1)
    qn = r % n_new
    valid = ((j < w) & (j > qn)) | ((j >= w) & (j - w <= qn))

    def body(s, carry):
        kall[0:w, :] = ck_ref[s]
        vall[0:w, :] = cv_ref[s]
        kall[w:w + n_new, :] = kn_ref[s]
        vall[w:w + n_new, :] = vn_ref[s]
        nk_ref[s] = kall[n_new:n_new + w, :]
        nv_ref[s] = vall[n_new:n_new + w, :]
        k = kall[...].astype(BF16)
        v = vall[...].astype(BF16)
        for kh in range(SWA_KV_HEADS):
            q = q_ref[s, kh].astype(BF16)
            lg = _dot_t(q, k[:, kh * hd:(kh + 1) * hd]) * SWA_SCALE
            lg = jnp.where(valid, lg, NEG_INF)
            sink = sink_ref[kh][:, 0:1]
            m = jnp.maximum(jnp.max(lg, axis=-1, keepdims=True), sink)
            p = jnp.exp(lg - m)
            denom = jnp.sum(p, axis=-1, keepdims=True) + jnp.exp(sink - m)
            o_ref[s, kh] = (_dot(p.astype(BF16), v[:, kh * hd:(kh + 1) * hd]) / denom).astype(o_ref.dtype)
        return carry

    lax.fori_loop(0, seqs, body, 0)


def _swa_sample(q, k_new, v_new, cache_k, cache_v, sink_rows):
    b, kvh, rows, hd = q.shape
    n_new = k_new.shape[1]
    w, kw = cache_k.shape[1], cache_k.shape[2]
    assert w == WINDOW
    seqs = _pick(b, (8, 4, 2, 1))
    return pl.pallas_call(
        functools.partial(_swa_sample_kernel, n_new=n_new, seqs=seqs),
        grid=(b // seqs,),
        in_specs=[pl.BlockSpec((seqs, kvh, rows, hd), lambda i: (i, 0, 0, 0)),
                  pl.BlockSpec((seqs, n_new, kw), lambda i: (i, 0, 0)),
                  pl.BlockSpec((seqs, n_new, kw), lambda i: (i, 0, 0)),
                  pl.BlockSpec((seqs, w, kw), lambda i: (i, 0, 0)),
                  pl.BlockSpec((seqs, w, kw), lambda i: (i, 0, 0)),
                  pl.BlockSpec((kvh, rows, LANES), lambda i: (0, 0, 0))],
        out_specs=[pl.BlockSpec((seqs, kvh, rows, hd), lambda i: (i, 0, 0, 0)),
                   pl.BlockSpec((seqs, w, kw), lambda i: (i, 0, 0)),
                   pl.BlockSpec((seqs, w, kw), lambda i: (i, 0, 0))],
        out_shape=[jax.ShapeDtypeStruct((b, kvh, rows, hd), BF16),
                   jax.ShapeDtypeStruct((b, w, kw), F32),
                   jax.ShapeDtypeStruct((b, w, kw), F32)],
        scratch_shapes=[pltpu.VMEM((2 * w, kw), F32), pltpu.VMEM((2 * w, kw), F32)],
        compiler_params=_cparams(("parallel",)),
    )(q, k_new, v_new, cache_k, cache_v, sink_rows)


def _mem_heads(q, k_of, v_of):
    hd = MEM_HEAD_DIM
    outs = []
    for h in range(MEM_HEADS):
        s = _dot_t(q[:, h * hd:(h + 1) * hd], k_of(h)) * MEM_SCALE
        m = jnp.max(s, axis=-1, keepdims=True)
        p = jnp.exp(s - m)
        denom = jnp.sum(p, axis=-1, keepdims=True)
        outs.append(_dot(p.astype(BF16), v_of(h)) / denom)
    return jnp.concatenate(outs, axis=1)


def _mem_prompt_kernel(q_ref, k_ref, v_ref, o_ref):
    hd = MEM_HEAD_DIM
    k = k_ref[...].astype(BF16)
    v = v_ref[...].astype(BF16)
    o_ref[...] = _mem_heads(q_ref[...], lambda h: k[:, h * hd:(h + 1) * hd],
                            lambda h: v[:, h * hd:(h + 1) * hd]).astype(o_ref.dtype)


def _mem_prompt(q, kv, batch, seq):
    width = MEM_HEADS * MEM_HEAD_DIM
    m = kv.shape[0] // batch
    tq = _pick(seq, (512, 256, 128))
    nq = seq // tq
    return pl.pallas_call(
        _mem_prompt_kernel,
        grid=(batch, nq),
        in_specs=[pl.BlockSpec((tq, width), lambda b, i: (b * nq + i, 0)),
                  pl.BlockSpec((m, width), lambda b, i: (b, 0)),
                  pl.BlockSpec((m, width), lambda b, i: (b, 1))],
        out_specs=pl.BlockSpec((tq, width), lambda b, i: (b * nq + i, 0)),
        out_shape=jax.ShapeDtypeStruct((batch * seq, width), BF16),
        compiler_params=_cparams(("parallel", "arbitrary")),
    )(q, kv, kv)


def _mem_sample_kernel(q_ref, k_ref, v_ref, o_ref, *, seqs, m):
    nh = MEM_HEADS

    def body(s, carry):
        o_ref[s] = _mem_heads(q_ref[s],
                              lambda h: k_ref[s, pl.ds(h, m, stride=nh), :].astype(BF16),
                              lambda h: v_ref[s, pl.ds(h, m, stride=nh), :].astype(BF16)).astype(o_ref.dtype)
        return carry

    lax.fori_loop(0, seqs, body, 0)


def _mem_sample(q, k, v, layer):
    b, rows, width = q.shape
    mh, hd = k.shape[1], k.shape[2]
    seqs = _pick(b, (8, 4, 2, 1))
    nb = b // seqs
    return pl.pallas_call(
        functools.partial(_mem_sample_kernel, seqs=seqs, m=mh // MEM_HEADS),
        grid=(nb,),
        in_specs=[pl.BlockSpec((seqs, rows, width), lambda i: (i, 0, 0)),
                  pl.BlockSpec((seqs, mh, hd), lambda i: (layer * nb + i, 0, 0)),
                  pl.BlockSpec((seqs, mh, hd), lambda i: (layer * nb + i, 0, 0))],
        out_specs=pl.BlockSpec((seqs, rows, width), lambda i: (i, 0, 0)),
        out_shape=jax.ShapeDtypeStruct((b, rows, width), BF16),
        compiler_params=_cparams(("parallel",)),
    )(q, k, v)


def _sb_local(z, mask, tri2):
    sp = jnp.maximum(z, 0.0) + jnp.log(1.0 + jnp.exp(-jnp.abs(z)))
    spm = sp if mask is None else jnp.where(mask, sp, 0.0)
    cum = _dot(spm.astype(BF16), tri2)
    kb = z.shape[1]
    a = z - sp - cum[:, :kb]
    if mask is not None:
        a = jnp.where(mask, a, NEG_INF)
    return a, -cum[:, kb:]


def _tri2(kb):
    r = lax.broadcasted_iota(jnp.int32, (kb, 2 * kb), 0)
    cidx = lax.broadcasted_iota(jnp.int32, (kb, 2 * kb), 1)
    return jnp.where((cidx >= kb) | (r > cidx), 1.0, 0.0).astype(BF16)


def _sb_prompt_kernel(bias_ref, q_ref, k_ref, v_ref, o_ref, kb_scr, vb_scr, c_scr, acc_scr):
    kh = pl.program_id(1)
    n = pl.program_id(2)
    blk = SB_BLOCK
    hd = SB_HEAD_DIM

    @pl.when(n == 0)
    def _():
        kb_scr[...] = k_ref[...].astype(BF16)
        vb_scr[...] = v_ref[...].astype(BF16)

    q = q_ref[...]
    qs = jnp.concatenate([q[:, g * hd:(g + 1) * hd] for g in range(SB_GROUP)], axis=0).astype(BF16)
    bias = jnp.concatenate(
        [jnp.full((blk, LANES), bias_ref[kh * SB_GROUP + g], F32) for g in range(SB_GROUP)], axis=0)
    tri2 = _tri2(blk)
    rows = SB_GROUP * blk
    qi = lax.broadcasted_iota(jnp.int32, (rows, blk), 0) % blk
    sj = lax.broadcasted_iota(jnp.int32, (rows, blk), 1)
    diag_mask = sj < qi

    def local(kblk, mask):
        start = pl.multiple_of(kblk * blk, blk)
        z = _dot_t(qs, kb_scr[pl.ds(start, blk), :]) * SB_SCALE + bias
        return _sb_local(z, mask, tri2)

    def accumulate(kblk, a, tot):
        start = pl.multiple_of(kblk * blk, blk)
        wgt = jnp.exp(a + c_scr[...])
        acc_scr[...] += _dot(wgt.astype(BF16), vb_scr[pl.ds(start, blk), :])
        c_scr[...] += tot

    c_scr[...] = jnp.zeros_like(c_scr)
    acc_scr[...] = jnp.zeros_like(acc_scr)
    accumulate(n, *local(n, diag_mask))

    def body(i, carry):
        k1 = n - 1 - 2 * i
        l1 = local(k1, None)
        l2 = local(k1 - 1, None)
        accumulate(k1, *l1)
        accumulate(k1 - 1, *l2)
        return carry

    lax.fori_loop(0, n // 2, body, 0)

    @pl.when(n % 2 == 1)
    def _():
        accumulate(0, *local(0, None))

    acc = acc_scr[...]
    o_ref[...] = jnp.concatenate([acc[g * blk:(g + 1) * blk] for g in range(SB_GROUP)],
                                 axis=1).astype(o_ref.dtype)


def _sb_prompt(qkv, bias, batch, seq):
    hd = SB_HEAD_DIM
    nb = seq // SB_BLOCK
    gw = SB_GROUP * hd
    kcol0 = SB_HEADS
    vcol0 = SB_HEADS + SB_KV_HEADS
    return pl.pallas_call(
        _sb_prompt_kernel,
        grid=(batch, SB_KV_HEADS, nb),
        in_specs=[pl.BlockSpec(memory_space=pltpu.SMEM),
                  pl.BlockSpec((SB_BLOCK, gw), lambda b, kh, n: (b * nb + n, kh)),
                  pl.BlockSpec((seq, hd), lambda b, kh, n: (b, kcol0 + kh)),
                  pl.BlockSpec((seq, hd), lambda b, kh, n: (b, vcol0 + kh))],
        out_specs=pl.BlockSpec((SB_BLOCK, gw), lambda b, kh, n: (b * nb + n, kh)),
        out_shape=jax.ShapeDtypeStruct((batch * seq, SB_HEADS * hd), BF16),
        scratch_shapes=[pltpu.VMEM((seq, hd), BF16), pltpu.VMEM((seq, hd), BF16),
                        pltpu.VMEM((SB_GROUP * SB_BLOCK, LANES), F32),
                        pltpu.VMEM((SB_GROUP * SB_BLOCK, hd), F32)],
        compiler_params=_cparams(("parallel", "parallel", "arbitrary")),
    )(bias, qkv, qkv, qkv)


def _sb_sample_kernel(pt_ref, q_ref, kn_ref, vn_ref, bias_ref, *rest, n_new, pages_per_step):
    k_refs = rest[:pages_per_step]
    v_refs = rest[pages_per_step:2 * pages_per_step]
    o_ref, kpad, vpad, c_scr, acc_scr = rest[2 * pages_per_step:]
    del pt_ref
    step = pl.program_id(1)
    nkv = SB_KV_HEADS
    page = SB_BLOCK
    rows = SB_GROUP * n_new
    tri2 = _tri2(page)
    q = q_ref[0]
    bias = bias_ref[...]

    def head(ref, kh):
        return ref[pl.ds(kh, page, stride=nkv), :].astype(BF16)

    qr = nkv * rows

    def local(page_refs, mask):
        npg = len(page_refs)
        z = jnp.concatenate(
            [_dot_t(q[kh * rows:(kh + 1) * rows], head(k_ref, kh))
             for k_ref, _ in page_refs for kh in range(nkv)], axis=0)
        z = z * SB_SCALE + jnp.concatenate([bias] * npg, axis=0)
        a, tot = _sb_local(z, mask, tri2)
        wb = jnp.exp(a).astype(BF16)
        outs = [jnp.concatenate(
            [_dot(wb[p * qr + kh * rows:p * qr + (kh + 1) * rows], head(v_ref, kh))
             for kh in range(nkv)], axis=0) for p, (_, v_ref) in enumerate(page_refs)]
        return outs, [tot[p * qr:(p + 1) * qr] for p in range(npg)]

    def combine(parts):
        off = c_scr[...]
        acc = acc_scr[...]
        for outs, tots in parts:
            for o_p, t_p in zip(outs, tots):
                acc = acc + jnp.exp(off) * o_p
                off = off + t_p
        acc_scr[...] = acc
        c_scr[...] = off

    @pl.when(step == 0)
    def _():
        c_scr[...] = jnp.zeros_like(c_scr)
        acc_scr[...] = jnp.zeros_like(acc_scr)
        kpad[...] = jnp.zeros_like(kpad)
        vpad[...] = jnp.zeros_like(vpad)
        kpad[0:nkv * n_new, :] = kn_ref[0]
        vpad[0:nkv * n_new, :] = vn_ref[0]
        r = lax.broadcasted_iota(jnp.int32, (nkv * rows, page), 0)
        j = lax.broadcasted_iota(jnp.int32, (nkv * rows, page), 1)
        combine([local([(kpad, vpad)], j < (r % n_new))])

    pages = [(k_refs[i].at[0], v_refs[i].at[0]) for i in range(pages_per_step)]
    chain = min(SB_PAGE_CHAIN, pages_per_step)
    combine([local(pages[c:c + chain], None) for c in range(0, pages_per_step, chain)])

    @pl.when(step == pl.num_programs(1) - 1)
    def _():
        o_ref[0] = acc_scr[...].astype(o_ref.dtype)


def _sb_sample(q, k_new, v_new, k_pool, v_pool, page_table, bias_rows, page0):
    b, qrows, hd = q.shape
    new_rows = k_new.shape[1]
    n_new = new_rows // SB_KV_HEADS
    n_pages = page_table.shape[1]
    prow = k_pool.shape[1]
    assert prow == SB_BLOCK * SB_KV_HEADS
    pps = _pick(n_pages, (SB_PAGES_PER_STEP, 8, 4, 2, 1))
    steps = n_pages // pps

    def page_map(i):
        return lambda s, c, pt: (page0 + pt[s * n_pages + (n_pages - 1 - (c * pps + i))], 0, 0)

    pool_specs = [pl.BlockSpec((1, prow, hd), page_map(i)) for i in range(pps)]
    grid_spec = pltpu.PrefetchScalarGridSpec(
        num_scalar_prefetch=1,
        grid=(b, steps),
        in_specs=[pl.BlockSpec((1, qrows, hd), lambda s, c, pt: (s, 0, 0)),
                  pl.BlockSpec((1, new_rows, hd), lambda s, c, pt: (s, 0, 0)),
                  pl.BlockSpec((1, new_rows, hd), lambda s, c, pt: (s, 0, 0)),
                  pl.BlockSpec((qrows, LANES), lambda s, c, pt: (0, 0))] + pool_specs + pool_specs,
        out_specs=pl.BlockSpec((1, qrows, hd), lambda s, c, pt: (s, 0, 0)),
        scratch_shapes=[pltpu.VMEM((prow, hd), F32), pltpu.VMEM((prow, hd), F32),
                        pltpu.VMEM((qrows, LANES), F32), pltpu.VMEM((qrows, hd), F32)],
    )
    return pl.pallas_call(
        functools.partial(_sb_sample_kernel, n_new=n_new, pages_per_step=pps),
        grid_spec=grid_spec,
        out_shape=jax.ShapeDtypeStruct((b, qrows, hd), BF16),
        compiler_params=_cparams(("parallel", "arbitrary")),
    )(page_table.reshape(-1), q, k_new, v_new, bias_rows, *([k_pool] * pps), *([v_pool] * pps))


def _router_kernel(x_ref, g_ref, w_ref, b_ref, h_ref, info_ref):
    xv = x_ref[...]
    ms = jnp.mean(xv * xv, axis=-1, keepdims=True)
    h = xv * lax.rsqrt(ms + RMS_EPS) * g_ref[...]
    h_ref[...] = h
    logits = jnp.dot(h, w_ref[...], preferred_element_type=F32,
                     precision=lax.Precision.HIGHEST) + b_ref[...]
    lane = lax.broadcasted_iota(jnp.int32, logits.shape, 1)
    big = jnp.int32(1 << 20)
    is_g = (lane >= N_EXPERTS) & (lane < N_EXPERTS + N_GROUPS)
    gl = jnp.where(is_g, logits, -jnp.inf)
    gmax = jnp.max(gl, axis=-1, keepdims=True)
    gidx = jnp.min(jnp.where(gl == gmax, lane - N_EXPERTS, big), axis=-1, keepdims=True)
    p_group = 1.0 / jnp.sum(jnp.exp(gl - gmax), axis=-1, keepdims=True)
    lo = gidx * EXPERTS_PER_GROUP
    in_grp = (lane >= lo) & (lane < lo + EXPERTS_PER_GROUP)
    el = jnp.where(in_grp, logits, -jnp.inf)
    emax = jnp.max(el, axis=-1, keepdims=True)
    pe = jnp.exp(el - emax)
    prob = pe / jnp.sum(pe, axis=-1, keepdims=True)
    prob = jnp.where(in_grp, prob, -1.0)
    p1 = jnp.max(prob, axis=-1, keepdims=True)
    i1 = jnp.min(jnp.where(prob == p1, lane, big), axis=-1, keepdims=True)
    rest = jnp.where(lane == i1, -1.0, prob)
    p2 = jnp.max(rest, axis=-1, keepdims=True)
    i2 = jnp.min(jnp.where(rest == p2, lane, big), axis=-1, keepdims=True)
    tot = p1 + p2
    w1 = p1 / tot * p_group
    w2 = p2 / tot * p_group
    info = jnp.where(lane == 0, i1.astype(F32),
                     jnp.where(lane == 1, i2.astype(F32),
                               jnp.where(lane == 2, w1, jnp.where(lane == 3, w2, 0.0))))
    info_ref[...] = info


def _router(x, g, w_router, b_router, w_group, b_group):
    t, d = x.shape
    pad = LANES - N_EXPERTS - N_GROUPS
    w = jnp.concatenate([w_router, w_group, jnp.zeros((d, pad), F32)], axis=1)
    bias = jnp.concatenate([b_router, b_group, jnp.zeros((pad,), F32)]).reshape(1, LANES)
    tm = _pick(t, (256, 128, 8))
    return pl.pallas_call(
        _router_kernel,
        grid=(t // tm,),
        in_specs=[pl.BlockSpec((tm, d), lambda i: (i, 0)),
                  pl.BlockSpec((1, d), lambda i: (0, 0)),
                  pl.BlockSpec((d, LANES), lambda i: (0, 0)),
                  pl.BlockSpec((1, LANES), lambda i: (0, 0))],
        out_specs=[pl.BlockSpec((tm, d), lambda i: (i, 0)),
                   pl.BlockSpec((tm, LANES), lambda i: (i, 0))],
        out_shape=[jax.ShapeDtypeStruct((t, d), F32), jax.ShapeDtypeStruct((t, LANES), F32)],
        compiler_params=_cparams(("parallel",)),
    )(x, g.reshape(1, d), w, bias)


def _moe_kernel(te_ref, r0_ref, nv_ref, nused_ref, tok_ref, dst_ref, h_hbm, wg_ref, wu_ref, wd_ref, y_hbm,
                xbuf, ybuf, gsem, ssem, *, tm, n_pairs):
    del te_ref
    i = pl.program_id(0)
    n_tiles = pl.num_programs(0)
    slot = i % 2
    n_valid = nv_ref[i]
    n_used = nused_ref[0]

    def start_gather(tile, sl):
        base = r0_ref[tile]
        for r in range(tm):
            pltpu.make_async_copy(h_hbm.at[pl.ds(tok_ref[base + r], 1)],
                                  xbuf.at[sl, pl.ds(r, 1)], gsem.at[sl]).start()

    def wait_gather(sl):
        pltpu.make_async_copy(h_hbm.at[pl.ds(0, tm)], xbuf.at[sl], gsem.at[sl]).wait()

    def start_scatter(sl):
        base = r0_ref[i]
        spare = n_pairs + sl * tm
        for r in range(tm):
            dst = jnp.where(r < n_valid, dst_ref[base + r], spare + r)
            pltpu.make_async_copy(ybuf.at[sl, pl.ds(r, 1)], y_hbm.at[pl.ds(dst, 1)], ssem.at[sl]).start()

    def wait_scatter(sl):
        pltpu.make_async_copy(ybuf.at[sl], y_hbm.at[pl.ds(0, tm)], ssem.at[sl]).wait()

    @pl.when(i == 0)
    def _():
        ybuf[0] = jnp.zeros(ybuf.shape[1:], ybuf.dtype)
        for sl in range(2):
            pltpu.make_async_copy(ybuf.at[0], y_hbm.at[pl.ds(n_pairs + sl * tm, tm)], ssem.at[sl]).start()
        for sl in range(2):
            wait_scatter(sl)
        start_gather(0, 0)

    @pl.when(n_valid > 0)
    def _():
        wait_gather(slot)

        @pl.when(i + 1 < n_used)
        def _():
            start_gather(i + 1, 1 - slot)

        @pl.when(i >= 2)
        def _():
            wait_scatter(slot)

        x = xbuf[slot].astype(BF16)
        a = _dot(x, wg_ref[0].astype(BF16))
        u = _dot(x, wu_ref[0].astype(BF16))
        act = a * jax.nn.sigmoid(a) * u
        ybuf[slot] = _dot(act.astype(BF16), wd_ref[0].astype(BF16))
        start_scatter(slot)

    @pl.when(i == n_tiles - 1)
    def _():
        last = n_used - 1
        wait_scatter(last % 2)

        @pl.when(last >= 1)
        def _():
            wait_scatter(1 - last % 2)


def _lookup(table, idx):
    n = table.shape[0]
    hit = idx[:, None] == jnp.arange(n, dtype=jnp.int32)[None, :]
    return jnp.sum(jnp.where(hit, table[None, :], 0), axis=1)


def _moe(h, info, w_gate, w_up, w_down, layer):
    t, d = h.shape
    n_exp, ff = N_EXPERTS, w_gate.shape[-1]
    tm = MOE_TILE
    pairs = 2 * t
    n_tiles = -(-pairs // tm) + n_exp
    eid = jnp.concatenate([info[:, 0], info[:, 1]]).astype(jnp.int32)
    sorted_eid, order = lax.sort_key_val(eid, jnp.arange(pairs, dtype=jnp.int32))
    bounds = jnp.sum((sorted_eid[None, :] < jnp.arange(n_exp + 1, dtype=jnp.int32)[:, None]).astype(jnp.int32),
                     axis=1)
    starts, counts = bounds[:-1], bounds[1:] - bounds[:-1]
    tiles_e = (counts + tm - 1) // tm
    tile_end = jnp.cumsum(tiles_e)
    n_used = tile_end[-1]
    tile = jnp.arange(n_tiles, dtype=jnp.int32)
    tile_e = jnp.minimum(jnp.sum((tile[:, None] >= tile_end[None, :]).astype(jnp.int32), axis=1), n_exp - 1)
    used = tile < n_used
    k_in_e = tile - _lookup(tile_end - tiles_e, tile_e)
    tile_r0 = jnp.where(used, _lookup(starts, tile_e) + k_in_e * tm, 0).astype(jnp.int32)
    tile_valid = jnp.where(used, jnp.clip(_lookup(counts, tile_e) - k_in_e * tm, 0, tm), 0).astype(jnp.int32)
    last_e = jnp.sum(jnp.where(tile == n_used - 1, tile_e, 0))
    tile_e = jnp.where(used, tile_e, last_e).astype(jnp.int32)
    tok_sorted = jnp.pad(order % t, (0, tm))
    dst_sorted = jnp.pad(order, (0, tm))

    wg = w_gate.reshape((-1,) + w_gate.shape[-2:])
    wu = w_up.reshape((-1,) + w_up.shape[-2:])
    wd = w_down.reshape((-1,) + w_down.shape[-2:])
    e0 = layer * n_exp

    def w_map(i, te, *_):
        return (e0 + te[i], 0, 0)

    grid_spec = pltpu.PrefetchScalarGridSpec(
        num_scalar_prefetch=6,
        grid=(n_tiles,),
        in_specs=[pl.BlockSpec(memory_space=pl.ANY),
                  pl.BlockSpec((1, d, ff), w_map),
                  pl.BlockSpec((1, d, ff), w_map),
                  pl.BlockSpec((1, ff, d), w_map)],
        out_specs=pl.BlockSpec(memory_space=pl.ANY),
        scratch_shapes=[pltpu.VMEM((2, tm, d), F32), pltpu.VMEM((2, tm, d), F32),
                        pltpu.SemaphoreType.DMA((2,)), pltpu.SemaphoreType.DMA((2,))],
    )
    return pl.pallas_call(
        functools.partial(_moe_kernel, tm=tm, n_pairs=pairs),
        grid_spec=grid_spec,
        out_shape=jax.ShapeDtypeStruct((pairs + 2 * tm, d), F32),
        compiler_params=_cparams(("arbitrary",)),
    )(tile_e, tile_r0, tile_valid, n_used.reshape(1).astype(jnp.int32), tok_sorted, dst_sorted, h, wg, wu, wd)


def _rope_tables(pos):
    half = ROPE_DIM // 2
    inv_freq = ROPE_THETA ** (-2.0 * jnp.arange(half, dtype=F32) / ROPE_DIM)
    ang = pos.astype(F32)[:, None] * inv_freq[None, :]
    cos, sin = jnp.cos(ang), jnp.sin(ang)
    t = pos.shape[0]
    ones = jnp.ones((t, SWA_HEAD_DIM - ROPE_DIM), F32)
    zeros = jnp.zeros((t, half), F32)
    zrest = jnp.zeros((t, SWA_HEAD_DIM - ROPE_DIM), F32)
    c_head = jnp.concatenate([cos, cos, ones], axis=1)
    plus_head = jnp.concatenate([zeros, sin, zrest], axis=1)
    minus_head = jnp.concatenate([-sin, zeros, zrest], axis=1)
    rep = LANES // SWA_HEAD_DIM
    return tuple(jnp.concatenate([a] * rep, axis=1) for a in (c_head, plus_head, minus_head))


def kernel(x_prompt, x_sample, mem_prompt, cache_swa_k, cache_swa_v, cache_sb_k, cache_sb_v, cache_mem_k, cache_mem_v, page_table, norm_mix, w_in_swa, sinks_swa, w_out_swa, w_in_sb, sb_bias, w_out_sb, norm_mem_q, norm_mem_kv, w_mem_q, w_mem_kv, w_mem_o, norm_ffn, w_group, b_group, w_router, b_router, w_gate, w_up, w_down, norm_final):
    b_p, s_p, d = x_prompt.shape
    b_s, n_s, _ = x_sample.shape
    depth = norm_mix.shape[0]
    t_p, t_s = b_p * s_p, b_s * n_s
    t = t_p + t_s
    past_len = page_table.shape[1] * cache_sb_k.shape[2]
    assert s_p >= WINDOW and s_p % WINDOW == 0

    x = (x_prompt.reshape(t_p, d), x_sample.reshape(t_s, d))
    pos = jnp.concatenate([jnp.tile(jnp.arange(s_p), b_p), jnp.tile(past_len + jnp.arange(n_s), b_s)])
    rope = _rope_tables(pos)
    mem_flat = mem_prompt.reshape(-1, d)
    mem_m = mem_prompt.shape[1]
    mem_w = MEM_HEADS * MEM_HEAD_DIM
    sb_pool_k = cache_sb_k.reshape(-1, cache_sb_k.shape[2] * SB_KV_HEADS, SB_HEAD_DIM)
    sb_pool_v = cache_sb_v.reshape(-1, cache_sb_v.shape[2] * SB_KV_HEADS, SB_HEAD_DIM)
    mem_cache_k = cache_mem_k.reshape(-1, cache_mem_k.shape[2] * MEM_HEADS, MEM_HEAD_DIM)
    mem_cache_v = cache_mem_v.reshape(-1, cache_mem_v.shape[2] * MEM_HEADS, MEM_HEAD_DIM)

    swa_kp, swa_vp, swa_ks, swa_vs = [], [], [], []
    sb_kp, sb_vp, sb_ks, sb_vs = [], [], [], []
    mem_kp, mem_vp = [], []
    moe = None
    for i in range(depth):
        j = i // 2
        first = moe is None
        if i % 2 == 0:
            nq = SWA_HEADS * SWA_HEAD_DIM
            nk = SWA_KV_HEADS * SWA_HEAD_DIM
            res = _norm_matmul(x, norm_mix[i], w_in_swa[j].astype(BF16), moe=moe, rope=rope,
                               rope_cols=nq + nk, emit_x=not first)
            qkv, x = (res, x) if first else res
            o_p = _swa_prompt(qkv, sinks_swa[j], b_p, s_p)
            qs = qkv[t_p:, :nq].reshape(b_s, n_s, SWA_KV_HEADS, SWA_GROUP, SWA_HEAD_DIM)
            qs = qs.transpose(0, 2, 3, 1, 4).reshape(b_s, SWA_KV_HEADS, SWA_GROUP * n_s, SWA_HEAD_DIM)
            kn = qkv[t_p:, nq:nq + nk].reshape(b_s, n_s, nk)
            vn = qkv[t_p:, nq + nk:].reshape(b_s, n_s, nk)
            sink_rows = jnp.broadcast_to(
                jnp.repeat(sinks_swa[j].reshape(SWA_KV_HEADS, SWA_GROUP), n_s, axis=1)[:, :, None],
                (SWA_KV_HEADS, SWA_GROUP * n_s, LANES))
            o_s, kbuf, vbuf = _swa_sample(qs, kn, vn, cache_swa_k[j].reshape(b_s, -1, nk),
                                          cache_swa_v[j].reshape(b_s, -1, nk), sink_rows)
            o_s = o_s.reshape(b_s, SWA_KV_HEADS, SWA_GROUP, n_s, SWA_HEAD_DIM)
            o_s = o_s.transpose(0, 3, 1, 2, 4).reshape(t_s, nq)
            kp = qkv[:t_p, nq:nq + nk].reshape(b_p, s_p, SWA_KV_HEADS, SWA_HEAD_DIM)
            vp = qkv[:t_p, nq + nk:].reshape(b_p, s_p, SWA_KV_HEADS, SWA_HEAD_DIM)
            swa_kp.append(kp[:, -WINDOW:])
            swa_vp.append(vp[:, -WINDOW:])
            swa_ks.append(kbuf.reshape(b_s, -1, SWA_KV_HEADS, SWA_HEAD_DIM))
            swa_vs.append(vbuf.reshape(b_s, -1, SWA_KV_HEADS, SWA_HEAD_DIM))
            w_out = w_out_swa[j]
        else:
            nq = SB_HEADS * SB_HEAD_DIM
            nk = SB_KV_HEADS * SB_HEAD_DIM
            res = _norm_matmul(x, norm_mix[i], w_in_sb[j].astype(BF16), moe=moe, emit_x=not first)
            qkv, x = (res, x) if first else res
            o_p = _sb_prompt(qkv, sb_bias[j], b_p, s_p)
            qs = qkv[t_p:, :nq].reshape(b_s, n_s, SB_KV_HEADS, SB_GROUP, SB_HEAD_DIM)
            qs = qs.transpose(0, 2, 3, 1, 4).reshape(b_s, SB_HEADS * n_s, SB_HEAD_DIM).astype(BF16)
            kn = qkv[t_p:, nq:nq + nk].reshape(b_s, n_s * SB_KV_HEADS, SB_HEAD_DIM)
            vn = qkv[t_p:, nq + nk:].reshape(b_s, n_s * SB_KV_HEADS, SB_HEAD_DIM)
            bias_rows = jnp.broadcast_to(jnp.repeat(sb_bias[j], n_s)[:, None], (SB_HEADS * n_s, LANES))
            o_s = _sb_sample(qs, kn, vn, sb_pool_k, sb_pool_v, page_table, bias_rows,
                             j * cache_sb_k.shape[1])
            o_s = o_s.reshape(b_s, SB_KV_HEADS, SB_GROUP, n_s, SB_HEAD_DIM)
            o_s = o_s.transpose(0, 3, 1, 2, 4).reshape(t_s, nq)
            sb_kp.append(qkv[:t_p, nq:nq + nk].reshape(b_p, s_p, SB_KV_HEADS, SB_HEAD_DIM))
            sb_vp.append(qkv[:t_p, nq + nk:].reshape(b_p, s_p, SB_KV_HEADS, SB_HEAD_DIM))
            sb_ks.append(kn.reshape(b_s, n_s, SB_KV_HEADS, SB_HEAD_DIM))
            sb_vs.append(vn.reshape(b_s, n_s, SB_KV_HEADS, SB_HEAD_DIM))
            w_out = w_out_sb[j]
        x = _resid_matmul((o_p, o_s), w_out.astype(BF16), x)

        mkv = _norm_matmul(mem_flat, norm_mem_kv[i], w_mem_kv[i].astype(BF16))
        mem_kp.append(mkv[:, :mem_w].reshape(b_p, mem_m, MEM_HEADS, MEM_HEAD_DIM))
        mem_vp.append(mkv[:, mem_w:].reshape(b_p, mem_m, MEM_HEADS, MEM_HEAD_DIM))
        qm = _norm_matmul(x, norm_mem_q[i], w_mem_q[i].astype(BF16), out_dtype=BF16)
        om_p = _mem_prompt(qm, mkv, b_p, s_p)
        row_pad = 16 - n_s % 16 if n_s % 16 else 0
        qm_s = jnp.pad(qm[t_p:].reshape(b_s, n_s, mem_w), ((0, 0), (0, row_pad), (0, 0)))
        om_s = _mem_sample(qm_s, mem_cache_k, mem_cache_v, i)
        x = _resid_matmul((om_p, om_s[:, :n_s].reshape(t_s, mem_w)), w_mem_o[i].astype(BF16), x)

        h, info = _router(x, norm_ffn[i], w_router[i], b_router[i], w_group[i], b_group[i])
        moe = (_moe(h, info, w_gate, w_up, w_down, i), info)

    y_prompt = _final_norm(x, moe[0], moe[1], norm_final, 0, t_p).reshape(b_p, s_p, d)
    y_sample = _final_norm(x, moe[0], moe[1], norm_final, t_p, t_s).reshape(b_s, n_s, d)
    return (y_prompt, y_sample,
            jnp.stack(swa_kp), jnp.stack(swa_vp), jnp.stack(swa_ks), jnp.stack(swa_vs),
            jnp.stack(sb_kp), jnp.stack(sb_vp), jnp.stack(sb_ks), jnp.stack(sb_vs),
            jnp.stack(mem_kp), jnp.stack(mem_vp))
```

```python
import functools
import math

import jax
import jax.numpy as jnp
from jax import lax
from jax.experimental import pallas as pl
from jax.experimental.pallas import tpu as pltpu

F32 = jnp.float32
BF16 = jnp.bfloat16

SWA_HEADS, SWA_KV_HEADS, SWA_HEAD_DIM = 32, 4, 64
SWA_GROUP = SWA_HEADS // SWA_KV_HEADS
WINDOW = 128
ROPE_THETA = 500000.0
ROPE_DIM = SWA_HEAD_DIM // 4
SWA_SCALE = SWA_HEAD_DIM ** -0.5
SB_HEADS, SB_KV_HEADS, SB_HEAD_DIM = 16, 4, 128
SB_GROUP = SB_HEADS // SB_KV_HEADS
SB_BLOCK = 128
SB_SCALE = SB_HEAD_DIM ** -0.5
MEM_HEADS, MEM_HEAD_DIM = 4, 128
MEM_SCALE = MEM_HEAD_DIM ** -0.5
N_GROUPS, EXPERTS_PER_GROUP = 4, 8
N_EXPERTS = N_GROUPS * EXPERTS_PER_GROUP
RMS_EPS = 1e-6
NEG_INF = -1e30

LANES = 128
VMEM_LIMIT = 56 * 1024 * 1024
MOE_TILE = 256
SB_PAGES_PER_STEP = 16
SB_PAGE_CHAIN = 4


def _cparams(sem):
    return pltpu.CompilerParams(dimension_semantics=sem, vmem_limit_bytes=VMEM_LIMIT)


def _pick(n, prefs):
    for p in prefs:
        if n % p == 0:
            return p
    return n


def _dot_t(a, b):
    return lax.dot_general(a, b, (((1,), (1,)), ((), ())), preferred_element_type=F32)


def _dot(a, b):
    return jnp.dot(a, b, preferred_element_type=F32)


def _rows(refs, n_first):
    if len(refs) == 1:
        return refs[0][...]
    return jnp.where(pl.program_id(0) < n_first, refs[0][...], refs[1][...])


def _row_specs(srcs, tm, n_first):
    if not isinstance(srcs, tuple):
        return [pl.BlockSpec((tm, srcs.shape[1]), lambda i, *_: (i, 0))], [srcs]
    width = srcs[0].shape[1]
    return ([pl.BlockSpec((tm, width), lambda i, *_: (jnp.minimum(i, n_first - 1), 0)),
             pl.BlockSpec((tm, width), lambda i, *_: (jnp.maximum(i - n_first, 0), 0))], list(srcs))


def _moe_combine(xv, ya_ref, yb_ref, info_ref):
    info = info_ref[...]
    return xv + ya_ref[...] * info[:, 2:3] + yb_ref[...] * info[:, 3:4]


def _norm_matmul_kernel(*refs, n_x, n_first, add, emit_x, rope_cols, tn):
    it = iter(refs)
    x_refs = [next(it) for _ in range(n_x)]
    ya_ref = next(it) if add else None
    yb_ref = next(it) if add else None
    info_ref = next(it) if add else None
    g_ref = next(it)
    w_ref = next(it)
    if rope_cols:
        cos_ref, sp_ref, sm_ref = next(it), next(it), next(it)
    o_ref = next(it)
    xo_ref = next(it) if emit_x else None
    h_scr = next(it)
    j = pl.program_id(1)

    @pl.when(j == 0)
    def _():
        xv = _rows(x_refs, n_first)
        if add:
            xv = _moe_combine(xv, ya_ref, yb_ref, info_ref)
        if emit_x:
            xo_ref[...] = xv
        ms = jnp.mean(xv * xv, axis=-1, keepdims=True)
        h_scr[...] = (xv * lax.rsqrt(ms + RMS_EPS) * g_ref[...]).astype(BF16)

    acc = _dot(h_scr[...], w_ref[...])
    if not rope_cols:
        o_ref[...] = acc.astype(o_ref.dtype)
        return

    @pl.when(j * tn < rope_cols)
    def _():
        reps = tn // LANES
        cos = jnp.concatenate([cos_ref[...]] * reps, axis=1)
        s_plus = jnp.concatenate([sp_ref[...]] * reps, axis=1)
        s_minus = jnp.concatenate([sm_ref[...]] * reps, axis=1)
        half = ROPE_DIM // 2
        roped = (acc * cos + pltpu.roll(acc, half, axis=1) * s_plus
                 + pltpu.roll(acc, tn - half, axis=1) * s_minus)
        col = j * tn + lax.broadcasted_iota(jnp.int32, acc.shape, 1)
        o_ref[...] = jnp.where(col < rope_cols, roped, acc).astype(o_ref.dtype)

    @pl.when(j * tn >= rope_cols)
    def _():
        o_ref[...] = acc.astype(o_ref.dtype)


def _norm_matmul(x, g, w, *, moe=None, rope=None, rope_cols=0, emit_x=False,
                 out_dtype=F32):
    pair = isinstance(x, tuple)
    t = x[0].shape[0] + x[1].shape[0] if pair else x.shape[0]
    d = x[0].shape[1] if pair else x.shape[1]
    n = w.shape[1]
    add = moe is not None
    tm = _pick(math.gcd(x[0].shape[0], x[1].shape[0]) if pair else t, (512, 256))
    tn = _pick(n, (1024, 512, 256, 128) if add else (1280, 1024, 512, 256, 128))
    n_i = t // tm
    n_first = x[0].shape[0] // tm if pair else n_i
    in_specs, args = _row_specs(x, tm, n_first)
    n_x = len(args)
    if add:
        in_specs += [pl.BlockSpec((tm, d), lambda i, j: (i, 0)),
                     pl.BlockSpec((tm, d), lambda i, j: (i + n_i, 0)),
                     pl.BlockSpec((tm, LANES), lambda i, j: (i, 0))]
        args += [moe[0], moe[0], moe[1]]
    in_specs += [pl.BlockSpec((1, d), lambda i, j: (0, 0)),
                 pl.BlockSpec((d, tn), lambda i, j: (0, j))]
    args += [g.reshape(1, d), w]
    if rope_cols:
        in_specs += [pl.BlockSpec((tm, LANES), lambda i, j: (i, 0))] * 3
        args += list(rope)
    out_shape = [jax.ShapeDtypeStruct((t, n), out_dtype)]
    out_specs = [pl.BlockSpec((tm, tn), lambda i, j: (i, j))]
    if emit_x:
        out_shape.append(jax.ShapeDtypeStruct((t, d), F32))
        out_specs.append(pl.BlockSpec((tm, d), lambda i, j: (i, 0)))
    res = pl.pallas_call(
        functools.partial(_norm_matmul_kernel, n_x=n_x, n_first=n_first,
                          add=add, emit_x=emit_x, rope_cols=rope_cols, tn=tn),
        grid=(n_i, n // tn),
        in_specs=in_specs,
        out_specs=out_specs,
        out_shape=out_shape,
        scratch_shapes=[pltpu.VMEM((tm, d), BF16)],
        compiler_params=_cparams(("parallel", "arbitrary")),
    )(*args)
    return res if emit_x else res[0]


def _resid_matmul_kernel(*refs, n_a, n_r, n_first):
    a_refs, w_ref, r_refs, o_ref = refs[:n_a], refs[n_a], refs[n_a + 1:n_a + 1 + n_r], refs[-1]
    o_ref[...] = _rows(r_refs, n_first) + _dot(_rows(a_refs, n_first), w_ref[...])


def _resid_matmul(a, w, resid):
    first = a[0] if isinstance(a, tuple) else resid[0] if isinstance(resid, tuple) else None
    t = sum(p.shape[0] for p in a) if isinstance(a, tuple) else a.shape[0]
    n = w.shape[1]
    tm = _pick(t if first is None else math.gcd(first.shape[0], t - first.shape[0]), (512, 256))
    n_first = t // tm if first is None else first.shape[0] // tm
    a_specs, a_args = _row_specs(a, tm, n_first)
    r_specs, r_args = _row_specs(resid, tm, n_first)
    return pl.pallas_call(
        functools.partial(_resid_matmul_kernel, n_a=len(a_args), n_r=len(r_args), n_first=n_first),
        grid=(t // tm,),
        in_specs=a_specs + [pl.BlockSpec(w.shape, lambda i: (0, 0))] + r_specs,
        out_specs=pl.BlockSpec((tm, n), lambda i: (i, 0)),
        out_shape=jax.ShapeDtypeStruct((t, n), F32),
        compiler_params=_cparams(("parallel",)),
    )(*a_args, w, *r_args)


def _final_norm_kernel(x_ref, ya_ref, yb_ref, info_ref, g_ref, o_ref):
    xv = _moe_combine(x_ref[...], ya_ref, yb_ref, info_ref)
    ms = jnp.mean(xv * xv, axis=-1, keepdims=True)
    o_ref[...] = xv * lax.rsqrt(ms + RMS_EPS) * g_ref[...]


def _final_norm(x, y2, info, g, row0, n_rows):
    t, d = x.shape
    tm = _pick(n_rows, (256, 128, 8))
    assert row0 % tm == 0 and t % tm == 0
    b0, bt = row0 // tm, t // tm
    return pl.pallas_call(
        _final_norm_kernel,
        grid=(n_rows // tm,),
        in_specs=[pl.BlockSpec((tm, d), lambda i: (i + b0, 0)),
                  pl.BlockSpec((tm, d), lambda i: (i + b0, 0)),
                  pl.BlockSpec((tm, d), lambda i: (i + b0 + bt, 0)),
                  pl.BlockSpec((tm, LANES), lambda i: (i + b0, 0)),
                  pl.BlockSpec((1, d), lambda i: (0, 0))],
        out_specs=pl.BlockSpec((tm, d), lambda i: (i, 0)),
        out_shape=jax.ShapeDtypeStruct((n_rows, d), F32),
        compiler_params=_cparams(("parallel",)),
    )(x, y2, y2, info, g.reshape(1, d))


def _swa_prompt_kernel(sink_ref, q_ref, kp_ref, kc_ref, vp_ref, vc_ref, o_ref):
    n = pl.program_id(1)
    blk = WINDOW
    qi = lax.broadcasted_iota(jnp.int32, (blk, 2 * blk), 0)
    sj = lax.broadcasted_iota(jnp.int32, (blk, 2 * blk), 1)
    rel = qi + blk - sj
    first_key = jnp.where(n > 0, 0, blk)
    valid = (rel >= 0) & (rel < WINDOW) & (sj >= first_key)
    k = jnp.concatenate([kp_ref[...], kc_ref[...]], axis=0).astype(BF16)
    v = jnp.concatenate([vp_ref[...], vc_ref[...]], axis=0).astype(BF16)
    hd = SWA_HEAD_DIM
    for kh in range(SWA_KV_HEADS):
        k_h = k[:, kh * hd:(kh + 1) * hd]
        v_h = v[:, kh * hd:(kh + 1) * hd]
        for g2 in range(SWA_GROUP // 2):
            pair = []
            for gg in range(2):
                h = kh * SWA_GROUP + g2 * 2 + gg
                q_h = q_ref[:, h * hd:(h + 1) * hd].astype(BF16)
                s = _dot_t(q_h, k_h) * SWA_SCALE
                s = jnp.where(valid, s, NEG_INF)
                sink = sink_ref[h]
                m = jnp.maximum(jnp.max(s, axis=-1, keepdims=True), sink)
                p = jnp.exp(s - m)
                denom = jnp.sum(p, axis=-1, keepdims=True) + jnp.exp(sink - m)
                pair.append(_dot(p.astype(BF16), v_h) / denom)
            h0 = kh * SWA_GROUP + g2 * 2
            o_ref[:, h0 * hd:(h0 + 2) * hd] = jnp.concatenate(pair, axis=1).astype(o_ref.dtype)


def _swa_prompt(qkv, sinks, batch, seq):
    nq = SWA_HEADS * SWA_HEAD_DIM
    nk = SWA_KV_HEADS * SWA_HEAD_DIM
    nb = seq // WINDOW
    kcol, vcol = nq // nk, nq // nk + 1

    def prev(b, n):
        return b * nb + jnp.maximum(n - 1, 0)

    return pl.pallas_call(
        _swa_prompt_kernel,
        grid=(batch, nb),
        in_specs=[pl.BlockSpec(memory_space=pltpu.SMEM),
                  pl.BlockSpec((WINDOW, nq), lambda b, n: (b * nb + n, 0)),
                  pl.BlockSpec((WINDOW, nk), lambda b, n: (prev(b, n), kcol)),
                  pl.BlockSpec((WINDOW, nk), lambda b, n: (b * nb + n, kcol)),
                  pl.BlockSpec((WINDOW, nk), lambda b, n: (prev(b, n), vcol)),
                  pl.BlockSpec((WINDOW, nk), lambda b, n: (b * nb + n, vcol))],
        out_specs=pl.BlockSpec((WINDOW, nq), lambda b, n: (b * nb + n, 0)),
        out_shape=jax.ShapeDtypeStruct((batch * seq, nq), BF16),
        compiler_params=_cparams(("parallel", "arbitrary")),
    )(sinks, qkv, qkv, qkv, qkv, qkv)


def _swa_sample_kernel(q_ref, kn_ref, vn_ref, ck_ref, cv_ref, sink_ref, o_ref, nk_ref, nv_ref,
                       kall, vall, *, n_new, seqs):
    w = WINDOW
    rows = SWA_GROUP * n_new
    hd = SWA_HEAD_DIM
    zeros = jnp.zeros((w, kall.shape[1]), F32)
    kall[w:2 * w, :] = zeros
    vall[w:2 * w, :] = zeros
    r = lax.broadcasted_iota(jnp.int32, (rows, 2 * w), 0)
    j = lax.broadcasted_iota(jnp.int32, (rows, 2 * w), 1)
    qn = r % n_new
    valid = ((j < w) & (j > qn)) | ((j >= w) & (j - w <= qn))

    def body(s, carry):
        kall[0:w, :] = ck_ref[s]
        vall[0:w, :] = cv_ref[s]
        kall[w:w + n_new, :] = kn_ref[s]
        vall[w:w + n_new, :] = vn_ref[s]
        nk_ref[s] = kall[n_new:n_new + w, :]
        nv_ref[s] = vall[n_new:n_new + w, :]
        k = kall[...].astype(BF16)
        v = vall[...].astype(BF16)
        for kh in range(SWA_KV_HEADS):
            q = q_ref[s, kh].astype(BF16)
            lg = _dot_t(q, k[:, kh * hd:(kh + 1) * hd]) * SWA_SCALE
            lg = jnp.where(valid, lg, NEG_INF)
            sink = sink_ref[kh][:, 0:1]
            m = jnp.maximum(jnp.max(lg, axis=-1, keepdims=True), sink)
            p = jnp.exp(lg - m)
            denom = jnp.sum(p, axis=-1, keepdims=True) + jnp.exp(sink - m)
            o_ref[s, kh] = (_dot(p.astype(BF16), v[:, kh * hd:(kh + 1) * hd]) / denom).astype(o_ref.dtype)
        return carry

    lax.fori_loop(0, seqs, body, 0)


def _swa_sample(q, k_new, v_new, cache_k, cache_v, sink_rows):
    b, kvh, rows, hd = q.shape
    n_new = k_new.shape[1]
    w, kw = cache_k.shape[1], cache_k.shape[2]
    assert w == WINDOW
    seqs = _pick(b, (8, 4, 2, 1))
    return pl.pallas_call(
        functools.partial(_swa_sample_kernel, n_new=n_new, seqs=seqs),
        grid=(b // seqs,),
        in_specs=[pl.BlockSpec((seqs, kvh, rows, hd), lambda i: (i, 0, 0, 0)),
                  pl.BlockSpec((seqs, n_new, kw), lambda i: (i, 0, 0)),
                  pl.BlockSpec((seqs, n_new, kw), lambda i: (i, 0, 0)),
                  pl.BlockSpec((seqs, w, kw), lambda i: (i, 0, 0)),
                  pl.BlockSpec((seqs, w, kw), lambda i: (i, 0, 0)),
                  pl.BlockSpec((kvh, rows, LANES), lambda i: (0, 0, 0))],
        out_specs=[pl.BlockSpec((seqs, kvh, rows, hd), lambda i: (i, 0, 0, 0)),
                   pl.BlockSpec((seqs, w, kw), lambda i: (i, 0, 0)),
                   pl.BlockSpec((seqs, w, kw), lambda i: (i, 0, 0))],
        out_shape=[jax.ShapeDtypeStruct((b, kvh, rows, hd), BF16),
                   jax.ShapeDtypeStruct((b, w, kw), F32),
                   jax.ShapeDtypeStruct((b, w, kw), F32)],
        scratch_shapes=[pltpu.VMEM((2 * w, kw), F32), pltpu.VMEM((2 * w, kw), F32)],
        compiler_params=_cparams(("parallel",)),
    )(q, k_new, v_new, cache_k, cache_v, sink_rows)


def _mem_heads(q, k_of, v_of):
    hd = MEM_HEAD_DIM
    outs = []
    for h in range(MEM_HEADS):
        s = _dot_t(q[:, h * hd:(h + 1) * hd], k_of(h)) * MEM_SCALE
        m = jnp.max(s, axis=-1, keepdims=True)
        p = jnp.exp(s - m)
        denom = jnp.sum(p, axis=-1, keepdims=True)
        outs.append(_dot(p.astype(BF16), v_of(h)) / denom)
    return jnp.concatenate(outs, axis=1)


def _mem_prompt_kernel(q_ref, k_ref, v_ref, o_ref):
    hd = MEM_HEAD_DIM
    k = k_ref[...].astype(BF16)
    v = v_ref[...].astype(BF16)
    o_ref[...] = _mem_heads(q_ref[...], lambda h: k[:, h * hd:(h + 1) * hd],
                            lambda h: v[:, h * hd:(h + 1) * hd]).astype(o_ref.dtype)


def _mem_prompt(q, kv, batch, seq):
    width = MEM_HEADS * MEM_HEAD_DIM
    m = kv.shape[0] // batch
    tq = _pick(seq, (512, 256, 128))
    nq = seq // tq
    return pl.pallas_call(
        _mem_prompt_kernel,
        grid=(batch, nq),
        in_specs=[pl.BlockSpec((tq, width), lambda b, i: (b * nq + i, 0)),
                  pl.BlockSpec((m, width), lambda b, i: (b, 0)),
                  pl.BlockSpec((m, width), lambda b, i: (b, 1))],
        out_specs=pl.BlockSpec((tq, width), lambda b, i: (b * nq + i, 0)),
        out_shape=jax.ShapeDtypeStruct((batch * seq, width), BF16),
        compiler_params=_cparams(("parallel", "arbitrary")),
    )(q, kv, kv)


def _mem_sample_kernel(q_ref, k_ref, v_ref, o_ref, *, seqs, m):
    nh = MEM_HEADS

    def body(s, carry):
        o_ref[s] = _mem_heads(q_ref[s],
                              lambda h: k_ref[s, pl.ds(h, m, stride=nh), :].astype(BF16),
                              lambda h: v_ref[s, pl.ds(h, m, stride=nh), :].astype(BF16)).astype(o_ref.dtype)
        return carry

    lax.fori_loop(0, seqs, body, 0)


def _mem_sample(q, k, v, layer):
    b, rows, width = q.shape
    mh, hd = k.shape[1], k.shape[2]
    seqs = _pick(b, (8, 4, 2, 1))
    nb = b // seqs
    return pl.pallas_call(
        functools.partial(_mem_sample_kernel, seqs=seqs, m=mh // MEM_HEADS),
        grid=(nb,),
        in_specs=[pl.BlockSpec((seqs, rows, width), lambda i: (i, 0, 0)),
                  pl.BlockSpec((seqs, mh, hd), lambda i: (layer * nb + i, 0, 0)),
                  pl.BlockSpec((seqs, mh, hd), lambda i: (layer * nb + i, 0, 0))],
        out_specs=pl.BlockSpec((seqs, rows, width), lambda i: (i, 0, 0)),
        out_shape=jax.ShapeDtypeStruct((b, rows, width), BF16),
        compiler_params=_cparams(("parallel",)),
    )(q, k, v)


def _sb_local(z, mask, tri2):
    sp = jnp.maximum(z, 0.0) + jnp.log(1.0 + jnp.exp(-jnp.abs(z)))
    spm = sp if mask is None else jnp.where(mask, sp, 0.0)
    cum = _dot(spm.astype(BF16), tri2)
    kb = z.shape[1]
    a = z - sp - cum[:, :kb]
    if mask is not None:
        a = jnp.where(mask, a, NEG_INF)
    return a, -cum[:, kb:]


def _tri2(kb):
    r = lax.broadcasted_iota(jnp.int32, (kb, 2 * kb), 0)
    cidx = lax.broadcasted_iota(jnp.int32, (kb, 2 * kb), 1)
    return jnp.where((cidx >= kb) | (r > cidx), 1.0, 0.0).astype(BF16)


def _sb_prompt_kernel(bias_ref, q_ref, k_ref, v_ref, o_ref, kb_scr, vb_scr, c_scr, acc_scr):
    kh = pl.program_id(1)
    n = pl.program_id(2)
    blk = SB_BLOCK
    hd = SB_HEAD_DIM

    @pl.when(n == 0)
    def _():
        kb_scr[...] = k_ref[...].astype(BF16)
        vb_scr[...] = v_ref[...].astype(BF16)

    q = q_ref[...]
    qs = jnp.concatenate([q[:, g * hd:(g + 1) * hd] for g in range(SB_GROUP)], axis=0).astype(BF16)
    bias = jnp.concatenate(
        [jnp.full((blk, LANES), bias_ref[kh * SB_GROUP + g], F32) for g in range(SB_GROUP)], axis=0)
    tri2 = _tri2(blk)
    rows = SB_GROUP * blk
    qi = lax.broadcasted_iota(jnp.int32, (rows, blk), 0) % blk
    sj = lax.broadcasted_iota(jnp.int32, (rows, blk), 1)
    diag_mask = sj < qi

    def local(kblk, mask):
        start = pl.multiple_of(kblk * blk, blk)
        z = _dot_t(qs, kb_scr[pl.ds(start, blk), :]) * SB_SCALE + bias
        return _sb_local(z, mask, tri2)

    def accumulate(kblk, a, tot):
        start = pl.multiple_of(kblk * blk, blk)
        wgt = jnp.exp(a + c_scr[...])
        acc_scr[...] += _dot(wgt.astype(BF16), vb_scr[pl.ds(start, blk), :])
        c_scr[...] += tot

    c_scr[...] = jnp.zeros_like(c_scr)
    acc_scr[...] = jnp.zeros_like(acc_scr)
    accumulate(n, *local(n, diag_mask))

    def body(i, carry):
        k1 = n - 1 - 2 * i
        l1 = local(k1, None)
        l2 = local(k1 - 1, None)
        accumulate(k1, *l1)
        accumulate(k1 - 1, *l2)
        return carry

    lax.fori_loop(0, n // 2, body, 0)

    @pl.when(n % 2 == 1)
    def _():
        accumulate(0, *local(0, None))

    acc = acc_scr[...]
    o_ref[...] = jnp.concatenate([acc[g * blk:(g + 1) * blk] for g in range(SB_GROUP)],
                                 axis=1).astype(o_ref.dtype)


def _sb_prompt(qkv, bias, batch, seq):
    hd = SB_HEAD_DIM
    nb = seq // SB_BLOCK
    gw = SB_GROUP * hd
    kcol0 = SB_HEADS
    vcol0 = SB_HEADS + SB_KV_HEADS
    return pl.pallas_call(
        _sb_prompt_kernel,
        grid=(batch, SB_KV_HEADS, nb),
        in_specs=[pl.BlockSpec(memory_space=pltpu.SMEM),
                  pl.BlockSpec((SB_BLOCK, gw), lambda b, kh, n: (b * nb + n, kh)),
                  pl.BlockSpec((seq, hd), lambda b, kh, n: (b, kcol0 + kh)),
                  pl.BlockSpec((seq, hd), lambda b, kh, n: (b, vcol0 + kh))],
        out_specs=pl.BlockSpec((SB_BLOCK, gw), lambda b, kh, n: (b * nb + n, kh)),
        out_shape=jax.ShapeDtypeStruct((batch * seq, SB_HEADS * hd), BF16),
        scratch_shapes=[pltpu.VMEM((seq, hd), BF16), pltpu.VMEM((seq, hd), BF16),
                        pltpu.VMEM((SB_GROUP * SB_BLOCK, LANES), F32),
                        pltpu.VMEM((SB_GROUP * SB_BLOCK, hd), F32)],
        compiler_params=_cparams(("parallel", "parallel", "arbitrary")),
    )(bias, qkv, qkv, qkv)


def _sb_sample_kernel(pt_ref, q_ref, kn_ref, vn_ref, bias_ref, kpool, vpool, o_ref,
                      kbuf, vbuf, kpad, vpad, c_scr, acc_scr, ksem, vsem, *, n_new, pps, n_pages, page0):
    seq = pl.program_id(0)
    step = pl.program_id(1)
    steps = pl.num_programs(1)
    gstep = seq * steps + step
    slot = gstep % 2
    nkv = SB_KV_HEADS
    page = SB_BLOCK
    rows = SB_GROUP * n_new
    qr = nkv * rows
    tri2 = _tri2(page)
    q = q_ref[0]
    bias = bias_ref[...]

    def start_fetch(sq, chunk, sl):
        for i in range(pps):
            pg = page0 + pt_ref[sq * n_pages + (n_pages - 1 - (chunk * pps + i))]
            pltpu.make_async_copy(kpool.at[pg], kbuf.at[sl, i], ksem.at[sl]).start()
            pltpu.make_async_copy(vpool.at[pg], vbuf.at[sl, i], vsem.at[sl]).start()

    def wait_fetch(sl):
        pltpu.make_async_copy(kpool.at[pl.ds(0, pps)], kbuf.at[sl], ksem.at[sl]).wait()
        pltpu.make_async_copy(vpool.at[pl.ds(0, pps)], vbuf.at[sl], vsem.at[sl]).wait()

    @pl.when(gstep == 0)
    def _():
        start_fetch(0, 0, 0)

    @pl.when(gstep + 1 < pl.num_programs(0) * steps)
    def _():
        wrap = step + 1 == steps
        start_fetch(jnp.where(wrap, seq + 1, seq), jnp.where(wrap, 0, step + 1), 1 - slot)

    def head(ref, kh):
        return ref[pl.ds(kh, page, stride=nkv), :].astype(BF16)

    def local(page_refs, mask):
        npg = len(page_refs)
        z = jnp.concatenate(
            [_dot_t(q[kh * rows:(kh + 1) * rows], head(k_ref, kh))
             for k_ref, _ in page_refs for kh in range(nkv)], axis=0)
        z = z * SB_SCALE + jnp.concatenate([bias] * npg, axis=0)
        a, tot = _sb_local(z, mask, tri2)
        wb = jnp.exp(a).astype(BF16)
        outs = [jnp.concatenate(
            [_dot(wb[p * qr + kh * rows:p * qr + (kh + 1) * rows], head(v_ref, kh))
             for kh in range(nkv)], axis=0) for p, (_, v_ref) in enumerate(page_refs)]
        return outs, [tot[p * qr:(p + 1) * qr] for p in range(npg)]

    def combine(parts):
        off = c_scr[...]
        acc = acc_scr[...]
        for outs, tots in parts:
            for o_p, t_p in zip(outs, tots):
                acc = acc + jnp.exp(off) * o_p
                off = off + t_p
        acc_scr[...] = acc
        c_scr[...] = off

    @pl.when(step == 0)
    def _():
        c_scr[...] = jnp.zeros_like(c_scr)
        acc_scr[...] = jnp.zeros_like(acc_scr)
        kpad[...] = jnp.zeros_like(kpad)
        vpad[...] = jnp.zeros_like(vpad)
        kpad[0:nkv * n_new, :] = kn_ref[0]
        vpad[0:nkv * n_new, :] = vn_ref[0]
        r = lax.broadcasted_iota(jnp.int32, (qr, page), 0)
        j = lax.broadcasted_iota(jnp.int32, (qr, page), 1)
        combine([local([(kpad, vpad)], j < (r % n_new))])

    wait_fetch(slot)
    pages = [(kbuf.at[slot, i], vbuf.at[slot, i]) for i in range(pps)]
    chain = min(SB_PAGE_CHAIN, pps)
    combine([local(pages[c:c + chain], None) for c in range(0, pps, chain)])

    @pl.when(step == steps - 1)
    def _():
        o_ref[0] = acc_scr[...].astype(o_ref.dtype)


def _sb_sample(q, k_new, v_new, k_pool, v_pool, page_table, bias_rows, page0):
    b, qrows, hd = q.shape
    new_rows = k_new.shape[1]
    n_new = new_rows // SB_KV_HEADS
    n_pages = page_table.shape[1]
    prow = k_pool.shape[1]
    assert prow == SB_BLOCK * SB_KV_HEADS
    pps = _pick(n_pages, (SB_PAGES_PER_STEP, 8, 4, 2, 1))
    grid_spec = pltpu.PrefetchScalarGridSpec(
        num_scalar_prefetch=1,
        grid=(b, n_pages // pps),
        in_specs=[pl.BlockSpec((1, qrows, hd), lambda s, c, pt: (s, 0, 0)),
                  pl.BlockSpec((1, new_rows, hd), lambda s, c, pt: (s, 0, 0)),
                  pl.BlockSpec((1, new_rows, hd), lambda s, c, pt: (s, 0, 0)),
                  pl.BlockSpec((qrows, LANES), lambda s, c, pt: (0, 0)),
                  pl.BlockSpec(memory_space=pl.ANY),
                  pl.BlockSpec(memory_space=pl.ANY)],
        out_specs=pl.BlockSpec((1, qrows, hd), lambda s, c, pt: (s, 0, 0)),
        scratch_shapes=[pltpu.VMEM((2, pps, prow, hd), F32), pltpu.VMEM((2, pps, prow, hd), F32),
                        pltpu.VMEM((prow, hd), F32), pltpu.VMEM((prow, hd), F32),
                        pltpu.VMEM((qrows, LANES), F32), pltpu.VMEM((qrows, hd), F32),
                        pltpu.SemaphoreType.DMA((2,)), pltpu.SemaphoreType.DMA((2,))],
    )
    return pl.pallas_call(
        functools.partial(_sb_sample_kernel, n_new=n_new, pps=pps, n_pages=n_pages, page0=page0),
        grid_spec=grid_spec,
        out_shape=jax.ShapeDtypeStruct((b, qrows, hd), BF16),
        compiler_params=_cparams(("arbitrary", "arbitrary")),
    )(page_table.reshape(-1), q, k_new, v_new, bias_rows, k_pool, v_pool)


def _router_kernel(x_ref, g_ref, w_ref, b_ref, h_ref, info_ref):
    xv = x_ref[...]
    ms = jnp.mean(xv * xv, axis=-1, keepdims=True)
    h = xv * lax.rsqrt(ms + RMS_EPS) * g_ref[...]
    h_ref[...] = h
    logits = jnp.dot(h, w_ref[...], preferred_element_type=F32,
                     precision=lax.Precision.HIGHEST) + b_ref[...]
    lane = lax.broadcasted_iota(jnp.int32, logits.shape, 1)
    big = jnp.int32(1 << 20)
    is_g = (lane >= N_EXPERTS) & (lane < N_EXPERTS + N_GROUPS)
    gl = jnp.where(is_g, logits, -jnp.inf)
    gmax = jnp.max(gl, axis=-1, keepdims=True)
    gidx = jnp.min(jnp.where(gl == gmax, lane - N_EXPERTS, big), axis=-1, keepdims=True)
    p_group = 1.0 / jnp.sum(jnp.exp(gl - gmax), axis=-1, keepdims=True)
    lo = gidx * EXPERTS_PER_GROUP
    in_grp = (lane >= lo) & (lane < lo + EXPERTS_PER_GROUP)
    el = jnp.where(in_grp, logits, -jnp.inf)
    emax = jnp.max(el, axis=-1, keepdims=True)
    pe = jnp.exp(el - emax)
    prob = pe / jnp.sum(pe, axis=-1, keepdims=True)
    prob = jnp.where(in_grp, prob, -1.0)
    p1 = jnp.max(prob, axis=-1, keepdims=True)
    i1 = jnp.min(jnp.where(prob == p1, lane, big), axis=-1, keepdims=True)
    rest = jnp.where(lane == i1, -1.0, prob)
    p2 = jnp.max(rest, axis=-1, keepdims=True)
    i2 = jnp.min(jnp.where(rest == p2, lane, big), axis=-1, keepdims=True)
    tot = p1 + p2
    w1 = p1 / tot * p_group
    w2 = p2 / tot * p_group
    info = jnp.where(lane == 0, i1.astype(F32),
                     jnp.where(lane == 1, i2.astype(F32),
                               jnp.where(lane == 2, w1, jnp.where(lane == 3, w2, 0.0))))
    info_ref[...] = info


def _router(x, g, w_router, b_router, w_group, b_group):
    t, d = x.shape
    pad = LANES - N_EXPERTS - N_GROUPS
    w = jnp.concatenate([w_router, w_group, jnp.zeros((d, pad), F32)], axis=1)
    bias = jnp.concatenate([b_router, b_group, jnp.zeros((pad,), F32)]).reshape(1, LANES)
    tm = _pick(t, (256, 128, 8))
    return pl.pallas_call(
        _router_kernel,
        grid=(t // tm,),
        in_specs=[pl.BlockSpec((tm, d), lambda i: (i, 0)),
                  pl.BlockSpec((1, d), lambda i: (0, 0)),
                  pl.BlockSpec((d, LANES), lambda i: (0, 0)),
                  pl.BlockSpec((1, LANES), lambda i: (0, 0))],
        out_specs=[pl.BlockSpec((tm, d), lambda i: (i, 0)),
                   pl.BlockSpec((tm, LANES), lambda i: (i, 0))],
        out_shape=[jax.ShapeDtypeStruct((t, d), F32), jax.ShapeDtypeStruct((t, LANES), F32)],
        compiler_params=_cparams(("parallel",)),
    )(x, g.reshape(1, d), w, bias)


def _moe_kernel(te_ref, r0_ref, nv_ref, nused_ref, tok_ref, dst_ref, h_hbm, wg_ref, wu_ref, wd_ref, y_hbm,
                xbuf, ybuf, gsem, ssem, *, tm, n_pairs):
    del te_ref
    i = pl.program_id(0)
    slot = i % 2
    n_valid = nv_ref[i]
    n_used = nused_ref[0]

    def start_gather(tile, sl):
        base = r0_ref[tile]
        for r in range(tm):
            pltpu.make_async_copy(h_hbm.at[pl.ds(tok_ref[base + r], 1)],
                                  xbuf.at[sl, pl.ds(r, 1)], gsem.at[sl]).start()

    def wait_gather(sl):
        pltpu.make_async_copy(h_hbm.at[pl.ds(0, tm)], xbuf.at[sl], gsem.at[sl]).wait()

    def start_scatter(tile, sl, valid):
        base = r0_ref[tile]
        spare = n_pairs + sl * tm
        for r in range(tm):
            dst = jnp.where(r < valid, dst_ref[base + r], spare + r)
            pltpu.make_async_copy(ybuf.at[sl, pl.ds(r, 1)], y_hbm.at[pl.ds(dst, 1)], ssem.at[sl]).start()

    def wait_scatter(sl):
        pltpu.make_async_copy(ybuf.at[sl], y_hbm.at[pl.ds(0, tm)], ssem.at[sl]).wait()

    @pl.when(i == 0)
    def _():
        ybuf[1] = jnp.zeros(ybuf.shape[1:], ybuf.dtype)
        pltpu.make_async_copy(ybuf.at[1], y_hbm.at[pl.ds(n_pairs, tm)], ssem.at[0]).start()
        start_gather(0, 0)

    @pl.when(n_valid > 0)
    def _():
        wait_gather(slot)
        wait_scatter(slot)
        start_gather(i + 1, 1 - slot)
        prev = jnp.maximum(i - 1, 0)
        start_scatter(prev, 1 - slot, jnp.where(i >= 1, nv_ref[prev], 0))
        x = xbuf[slot].astype(BF16)
        a = _dot(x, wg_ref[0].astype(BF16))
        u = _dot(x, wu_ref[0].astype(BF16))
        act = a * jax.nn.sigmoid(a) * u
        ybuf[slot] = _dot(act.astype(BF16), wd_ref[0].astype(BF16))

    @pl.when(i == n_used)
    def _():
        last = i - 1
        wait_gather(slot)
        wait_scatter(slot)
        start_scatter(last, 1 - slot, nv_ref[last])
        wait_scatter(1 - slot)


def _lookup(table, idx):
    n = table.shape[0]
    hit = idx[:, None] == jnp.arange(n, dtype=jnp.int32)[None, :]
    return jnp.sum(jnp.where(hit, table[None, :], 0), axis=1)


def _moe(h, info, w_gate, w_up, w_down, layer):
    t, d = h.shape
    n_exp, ff = N_EXPERTS, w_gate.shape[-1]
    tm = MOE_TILE
    pairs = 2 * t
    n_tiles = -(-pairs // tm) + n_exp + 1
    eid = jnp.concatenate([info[:, 0], info[:, 1]]).astype(jnp.int32)
    sorted_eid, order = lax.sort_key_val(eid, jnp.arange(pairs, dtype=jnp.int32))
    bounds = jnp.sum((sorted_eid[None, :] < jnp.arange(n_exp + 1, dtype=jnp.int32)[:, None]).astype(jnp.int32),
                     axis=1)
    starts, counts = bounds[:-1], bounds[1:] - bounds[:-1]
    tiles_e = (counts + tm - 1) // tm
    tile_end = jnp.cumsum(tiles_e)
    n_used = tile_end[-1]
    tile = jnp.arange(n_tiles, dtype=jnp.int32)
    tile_e = jnp.minimum(jnp.sum((tile[:, None] >= tile_end[None, :]).astype(jnp.int32), axis=1), n_exp - 1)
    used = tile < n_used
    k_in_e = tile - _lookup(tile_end - tiles_e, tile_e)
    tile_r0 = jnp.where(used, _lookup(starts, tile_e) + k_in_e * tm, 0).astype(jnp.int32)
    tile_valid = jnp.where(used, jnp.clip(_lookup(counts, tile_e) - k_in_e * tm, 0, tm), 0).astype(jnp.int32)
    last_e = jnp.sum(jnp.where(tile == n_used - 1, tile_e, 0))
    tile_e = jnp.where(used, tile_e, last_e).astype(jnp.int32)
    tok_sorted = jnp.pad(order % t, (0, tm))
    dst_sorted = jnp.pad(order, (0, tm))

    wg = w_gate.reshape((-1,) + w_gate.shape[-2:])
    wu = w_up.reshape((-1,) + w_up.shape[-2:])
    wd = w_down.reshape((-1,) + w_down.shape[-2:])
    e0 = layer * n_exp

    def w_map(i, te, *_):
        return (e0 + te[i], 0, 0)

    grid_spec = pltpu.PrefetchScalarGridSpec(
        num_scalar_prefetch=6,
        grid=(n_tiles,),
        in_specs=[pl.BlockSpec(memory_space=pl.ANY),
                  pl.BlockSpec((1, d, ff), w_map),
                  pl.BlockSpec((1, d, ff), w_map),
                  pl.BlockSpec((1, ff, d), w_map)],
        out_specs=pl.BlockSpec(memory_space=pl.ANY),
        scratch_shapes=[pltpu.VMEM((2, tm, d), F32), pltpu.VMEM((2, tm, d), F32),
                        pltpu.SemaphoreType.DMA((2,)), pltpu.SemaphoreType.DMA((2,))],
    )
    return pl.pallas_call(
        functools.partial(_moe_kernel, tm=tm, n_pairs=pairs),
        grid_spec=grid_spec,
        out_shape=jax.ShapeDtypeStruct((pairs + 2 * tm, d), F32),
        compiler_params=_cparams(("arbitrary",)),
    )(tile_e, tile_r0, tile_valid, n_used.reshape(1).astype(jnp.int32), tok_sorted, dst_sorted, h, wg, wu, wd)


def _rope_tables(pos):
    half = ROPE_DIM // 2
    inv_freq = ROPE_THETA ** (-2.0 * jnp.arange(half, dtype=F32) / ROPE_DIM)
    ang = pos.astype(F32)[:, None] * inv_freq[None, :]
    cos, sin = jnp.cos(ang), jnp.sin(ang)
    t = pos.shape[0]
    ones = jnp.ones((t, SWA_HEAD_DIM - ROPE_DIM), F32)
    zeros = jnp.zeros((t, half), F32)
    zrest = jnp.zeros((t, SWA_HEAD_DIM - ROPE_DIM), F32)
    c_head = jnp.concatenate([cos, cos, ones], axis=1)
    plus_head = jnp.concatenate([zeros, sin, zrest], axis=1)
    minus_head = jnp.concatenate([-sin, zeros, zrest], axis=1)
    rep = LANES // SWA_HEAD_DIM
    return tuple(jnp.concatenate([a] * rep, axis=1) for a in (c_head, plus_head, minus_head))


def kernel(x_prompt, x_sample, mem_prompt, cache_swa_k, cache_swa_v, cache_sb_k, cache_sb_v, cache_mem_k, cache_mem_v, page_table, norm_mix, w_in_swa, sinks_swa, w_out_swa, w_in_sb, sb_bias, w_out_sb, norm_mem_q, norm_mem_kv, w_mem_q, w_mem_kv, w_mem_o, norm_ffn, w_group, b_group, w_router, b_router, w_gate, w_up, w_down, norm_final):
    b_p, s_p, d = x_prompt.shape
    b_s, n_s, _ = x_sample.shape
    depth = norm_mix.shape[0]
    t_p, t_s = b_p * s_p, b_s * n_s
    t = t_p + t_s
    past_len = page_table.shape[1] * cache_sb_k.shape[2]
    assert s_p >= WINDOW and s_p % WINDOW == 0

    x = (x_prompt.reshape(t_p, d), x_sample.reshape(t_s, d))
    pos = jnp.concatenate([jnp.tile(jnp.arange(s_p), b_p), jnp.tile(past_len + jnp.arange(n_s), b_s)])
    rope = _rope_tables(pos)
    mem_flat = mem_prompt.reshape(-1, d)
    mem_m = mem_prompt.shape[1]
    mem_w = MEM_HEADS * MEM_HEAD_DIM
    sb_pool_k = cache_sb_k.reshape(-1, cache_sb_k.shape[2] * SB_KV_HEADS, SB_HEAD_DIM)
    sb_pool_v = cache_sb_v.reshape(-1, cache_sb_v.shape[2] * SB_KV_HEADS, SB_HEAD_DIM)
    mem_cache_k = cache_mem_k.reshape(-1, cache_mem_k.shape[2] * MEM_HEADS, MEM_HEAD_DIM)
    mem_cache_v = cache_mem_v.reshape(-1, cache_mem_v.shape[2] * MEM_HEADS, MEM_HEAD_DIM)

    swa_kp, swa_vp, swa_ks, swa_vs = [], [], [], []
    sb_kp, sb_vp, sb_ks, sb_vs = [], [], [], []
    mem_kp, mem_vp = [], []
    moe = None
    for i in range(depth):
        j = i // 2
        first = moe is None
        if i % 2 == 0:
            nq = SWA_HEADS * SWA_HEAD_DIM
            nk = SWA_KV_HEADS * SWA_HEAD_DIM
            res = _norm_matmul(x, norm_mix[i], w_in_swa[j].astype(BF16), moe=moe, rope=rope,
                               rope_cols=nq + nk, emit_x=not first)
            qkv, x = (res, x) if first else res
            o_p = _swa_prompt(qkv, sinks_swa[j], b_p, s_p)
            qs = qkv[t_p:, :nq].reshape(b_s, n_s, SWA_KV_HEADS, SWA_GROUP, SWA_HEAD_DIM)
            qs = qs.transpose(0, 2, 3, 1, 4).reshape(b_s, SWA_KV_HEADS, SWA_GROUP * n_s, SWA_HEAD_DIM)
            kn = qkv[t_p:, nq:nq + nk].reshape(b_s, n_s, nk)
            vn = qkv[t_p:, nq + nk:].reshape(b_s, n_s, nk)
            sink_rows = jnp.broadcast_to(
                jnp.repeat(sinks_swa[j].reshape(SWA_KV_HEADS, SWA_GROUP), n_s, axis=1)[:, :, None],
                (SWA_KV_HEADS, SWA_GROUP * n_s, LANES))
            o_s, kbuf, vbuf = _swa_sample(qs, kn, vn, cache_swa_k[j].reshape(b_s, -1, nk),
                                          cache_swa_v[j].reshape(b_s, -1, nk), sink_rows)
            o_s = o_s.reshape(b_s, SWA_KV_HEADS, SWA_GROUP, n_s, SWA_HEAD_DIM)
            o_s = o_s.transpose(0, 3, 1, 2, 4).reshape(t_s, nq)
            kp = qkv[:t_p, nq:nq + nk].reshape(b_p, s_p, SWA_KV_HEADS, SWA_HEAD_DIM)
            vp = qkv[:t_p, nq + nk:].reshape(b_p, s_p, SWA_KV_HEADS, SWA_HEAD_DIM)
            swa_kp.append(kp[:, -WINDOW:])
            swa_vp.append(vp[:, -WINDOW:])
            swa_ks.append(kbuf.reshape(b_s, -1, SWA_KV_HEADS, SWA_HEAD_DIM))
            swa_vs.append(vbuf.reshape(b_s, -1, SWA_KV_HEADS, SWA_HEAD_DIM))
            w_out = w_out_swa[j]
        else:
            nq = SB_HEADS * SB_HEAD_DIM
            nk = SB_KV_HEADS * SB_HEAD_DIM
            res = _norm_matmul(x, norm_mix[i], w_in_sb[j].astype(BF16), moe=moe, emit_x=not first)
            qkv, x = (res, x) if first else res
            o_p = _sb_prompt(qkv, sb_bias[j], b_p, s_p)
            qs = qkv[t_p:, :nq].reshape(b_s, n_s, SB_KV_HEADS, SB_GROUP, SB_HEAD_DIM)
            qs = qs.transpose(0, 2, 3, 1, 4).reshape(b_s, SB_HEADS * n_s, SB_HEAD_DIM).astype(BF16)
            kn = qkv[t_p:, nq:nq + nk].reshape(b_s, n_s * SB_KV_HEADS, SB_HEAD_DIM)
            vn = qkv[t_p:, nq + nk:].reshape(b_s, n_s * SB_KV_HEADS, SB_HEAD_DIM)
            bias_rows = jnp.broadcast_to(jnp.repeat(sb_bias[j], n_s)[:, None], (SB_HEADS * n_s, LANES))
            o_s = _sb_sample(qs, kn, vn, sb_pool_k, sb_pool_v, page_table, bias_rows,
                             j * cache_sb_k.shape[1])
            o_s = o_s.reshape(b_s, SB_KV_HEADS, SB_GROUP, n_s, SB_HEAD_DIM)
            o_s = o_s.transpose(0, 3, 1, 2, 4).reshape(t_s, nq)
            sb_kp.append(qkv[:t_p, nq:nq + nk].reshape(b_p, s_p, SB_KV_HEADS, SB_HEAD_DIM))
            sb_vp.append(qkv[:t_p, nq + nk:].reshape(b_p, s_p, SB_KV_HEADS, SB_HEAD_DIM))
            sb_ks.append(kn.reshape(b_s, n_s, SB_KV_HEADS, SB_HEAD_DIM))
            sb_vs.append(vn.reshape(b_s, n_s, SB_KV_HEADS, SB_HEAD_DIM))
            w_out = w_out_sb[j]
        x = _resid_matmul((o_p, o_s), w_out.astype(BF16), x)

        mkv = _norm_matmul(mem_flat, norm_mem_kv[i], w_mem_kv[i].astype(BF16))
        mem_kp.append(mkv[:, :mem_w].reshape(b_p, mem_m, MEM_HEADS, MEM_HEAD_DIM))
        mem_vp.append(mkv[:, mem_w:].reshape(b_p, mem_m, MEM_HEADS, MEM_HEAD_DIM))
        qm = _norm_matmul(x, norm_mem_q[i], w_mem_q[i].astype(BF16), out_dtype=BF16)
        om_p = _mem_prompt(qm, mkv, b_p, s_p)
        row_pad = 16 - n_s % 16 if n_s % 16 else 0
        qm_s = jnp.pad(qm[t_p:].reshape(b_s, n_s, mem_w), ((0, 0), (0, row_pad), (0, 0)))
        om_s = _mem_sample(qm_s, mem_cache_k, mem_cache_v, i)
        x = _resid_matmul((om_p, om_s[:, :n_s].reshape(t_s, mem_w)), w_mem_o[i].astype(BF16), x)

        h, info = _router(x, norm_ffn[i], w_router[i], b_router[i], w_group[i], b_group[i])
        moe = (_moe(h, info, w_gate, w_up, w_down, i), info)

    y_prompt = _final_norm(x, moe[0], moe[1], norm_final, 0, t_p).reshape(b_p, s_p, d)
    y_sample = _final_norm(x, moe[0], moe[1], norm_final, t_p, t_s).reshape(b_s, n_s, d)
    return (y_prompt, y_sample,
            jnp.stack(swa_kp), jnp.stack(swa_vp), jnp.stack(swa_ks), jnp.stack(swa_vs),
            jnp.stack(sb_kp), jnp.stack(sb_vp), jnp.stack(sb_ks), jnp.stack(sb_vs),
            jnp.stack(mem_kp), jnp.stack(mem_vp))
```

```python
import functools
import math

import jax
import jax.numpy as jnp
from jax import lax
from jax.experimental import pallas as pl
from jax.experimental.pallas import tpu as pltpu

F32 = jnp.float32
BF16 = jnp.bfloat16

SWA_HEADS, SWA_KV_HEADS, SWA_HEAD_DIM = 32, 4, 64
SWA_GROUP = SWA_HEADS // SWA_KV_HEADS
WINDOW = 128
ROPE_THETA = 500000.0
ROPE_DIM = SWA_HEAD_DIM // 4
SWA_SCALE = SWA_HEAD_DIM ** -0.5
SB_HEADS, SB_KV_HEADS, SB_HEAD_DIM = 16, 4, 128
SB_GROUP = SB_HEADS // SB_KV_HEADS
SB_BLOCK = 128
SB_SCALE = SB_HEAD_DIM ** -0.5
MEM_HEADS, MEM_HEAD_DIM = 4, 128
MEM_SCALE = MEM_HEAD_DIM ** -0.5
N_GROUPS, EXPERTS_PER_GROUP = 4, 8
N_EXPERTS = N_GROUPS * EXPERTS_PER_GROUP
RMS_EPS = 1e-6
NEG_INF = -1e30

LANES = 128
VMEM_LIMIT = 56 * 1024 * 1024
MOE_TILE = 256
SB_PAGES_PER_STEP = 16
SB_PAGE_CHAIN = 16


def _cparams(sem):
    return pltpu.CompilerParams(dimension_semantics=sem, vmem_limit_bytes=VMEM_LIMIT)


def _pick(n, prefs):
    for p in prefs:
        if n % p == 0:
            return p
    return n


def _dot_t(a, b):
    return lax.dot_general(a, b, (((1,), (1,)), ((), ())), preferred_element_type=F32)


def _dot(a, b):
    return jnp.dot(a, b, preferred_element_type=F32)


def _rows(refs, n_first):
    if len(refs) == 1:
        return refs[0][...]
    return jnp.where(pl.program_id(0) < n_first, refs[0][...], refs[1][...])


def _row_specs(srcs, tm, n_first):
    if not isinstance(srcs, tuple):
        return [pl.BlockSpec((tm, srcs.shape[1]), lambda i, *_: (i, 0))], [srcs]
    width = srcs[0].shape[1]
    return ([pl.BlockSpec((tm, width), lambda i, *_: (jnp.minimum(i, n_first - 1), 0)),
             pl.BlockSpec((tm, width), lambda i, *_: (jnp.maximum(i - n_first, 0), 0))], list(srcs))


def _moe_combine(xv, ya_ref, yb_ref, info_ref):
    info = info_ref[...]
    return xv + ya_ref[...] * info[:, 2:3] + yb_ref[...] * info[:, 3:4]


def _norm_matmul_kernel(*refs, n_x, n_first, add, emit_x, rope_cols, tn):
    it = iter(refs)
    x_refs = [next(it) for _ in range(n_x)]
    ya_ref = next(it) if add else None
    yb_ref = next(it) if add else None
    info_ref = next(it) if add else None
    g_ref = next(it)
    w_ref = next(it)
    if rope_cols:
        cos_ref, sp_ref, sm_ref = next(it), next(it), next(it)
    o_ref = next(it)
    xo_ref = next(it) if emit_x else None
    h_scr = next(it)
    j = pl.program_id(1)

    @pl.when(j == 0)
    def _():
        xv = _rows(x_refs, n_first)
        if add:
            xv = _moe_combine(xv, ya_ref, yb_ref, info_ref)
        if emit_x:
            xo_ref[...] = xv
        ms = jnp.mean(xv * xv, axis=-1, keepdims=True)
        h_scr[...] = (xv * lax.rsqrt(ms + RMS_EPS) * g_ref[...]).astype(BF16)

    acc = _dot(h_scr[...], w_ref[...])
    if not rope_cols:
        o_ref[...] = acc.astype(o_ref.dtype)
        return

    @pl.when(j * tn < rope_cols)
    def _():
        reps = tn // LANES
        cos = jnp.concatenate([cos_ref[...]] * reps, axis=1)
        s_plus = jnp.concatenate([sp_ref[...]] * reps, axis=1)
        s_minus = jnp.concatenate([sm_ref[...]] * reps, axis=1)
        half = ROPE_DIM // 2
        roped = (acc * cos + pltpu.roll(acc, half, axis=1) * s_plus
                 + pltpu.roll(acc, tn - half, axis=1) * s_minus)
        col = j * tn + lax.broadcasted_iota(jnp.int32, acc.shape, 1)
        o_ref[...] = jnp.where(col < rope_cols, roped, acc).astype(o_ref.dtype)

    @pl.when(j * tn >= rope_cols)
    def _():
        o_ref[...] = acc.astype(o_ref.dtype)


def _norm_matmul(x, g, w, *, moe=None, rope=None, rope_cols=0, emit_x=False,
                 out_dtype=F32):
    pair = isinstance(x, tuple)
    t = x[0].shape[0] + x[1].shape[0] if pair else x.shape[0]
    d = x[0].shape[1] if pair else x.shape[1]
    n = w.shape[1]
    add = moe is not None
    tm = _pick(math.gcd(x[0].shape[0], x[1].shape[0]) if pair else t, (512, 256))
    tn = _pick(n, (1024, 512, 256, 128) if add else (1280, 1024, 512, 256, 128))
    n_i = t // tm
    n_first = x[0].shape[0] // tm if pair else n_i
    in_specs, args = _row_specs(x, tm, n_first)
    n_x = len(args)
    if add:
        in_specs += [pl.BlockSpec((tm, d), lambda i, j: (i, 0)),
                     pl.BlockSpec((tm, d), lambda i, j: (i + n_i, 0)),
                     pl.BlockSpec((tm, LANES), lambda i, j: (i, 0))]
        args += [moe[0], moe[0], moe[1]]
    in_specs += [pl.BlockSpec((1, d), lambda i, j: (0, 0)),
                 pl.BlockSpec((d, tn), lambda i, j: (0, j))]
    args += [g.reshape(1, d), w]
    if rope_cols:
        in_specs += [pl.BlockSpec((tm, LANES), lambda i, j: (i, 0))] * 3
        args += list(rope)
    out_shape = [jax.ShapeDtypeStruct((t, n), out_dtype)]
    out_specs = [pl.BlockSpec((tm, tn), lambda i, j: (i, j))]
    if emit_x:
        out_shape.append(jax.ShapeDtypeStruct((t, d), F32))
        out_specs.append(pl.BlockSpec((tm, d), lambda i, j: (i, 0)))
    res = pl.pallas_call(
        functools.partial(_norm_matmul_kernel, n_x=n_x, n_first=n_first,
                          add=add, emit_x=emit_x, rope_cols=rope_cols, tn=tn),
        grid=(n_i, n // tn),
        in_specs=in_specs,
        out_specs=out_specs,
        out_shape=out_shape,
        scratch_shapes=[pltpu.VMEM((tm, d), BF16)],
        compiler_params=_cparams(("parallel", "arbitrary")),
    )(*args)
    return res if emit_x else res[0]


def _resid_matmul_kernel(*refs, n_a, n_r, n_first):
    a_refs, w_ref, r_refs, o_ref = refs[:n_a], refs[n_a], refs[n_a + 1:n_a + 1 + n_r], refs[-1]
    o_ref[...] = _rows(r_refs, n_first) + _dot(_rows(a_refs, n_first), w_ref[...])


def _resid_matmul(a, w, resid):
    first = a[0] if isinstance(a, tuple) else resid[0] if isinstance(resid, tuple) else None
    t = sum(p.shape[0] for p in a) if isinstance(a, tuple) else a.shape[0]
    n = w.shape[1]
    tm = _pick(t if first is None else math.gcd(first.shape[0], t - first.shape[0]), (512, 256))
    n_first = t // tm if first is None else first.shape[0] // tm
    a_specs, a_args = _row_specs(a, tm, n_first)
    r_specs, r_args = _row_specs(resid, tm, n_first)
    return pl.pallas_call(
        functools.partial(_resid_matmul_kernel, n_a=len(a_args), n_r=len(r_args), n_first=n_first),
        grid=(t // tm,),
        in_specs=a_specs + [pl.BlockSpec(w.shape, lambda i: (0, 0))] + r_specs,
        out_specs=pl.BlockSpec((tm, n), lambda i: (i, 0)),
        out_shape=jax.ShapeDtypeStruct((t, n), F32),
        compiler_params=_cparams(("parallel",)),
    )(*a_args, w, *r_args)


def _final_norm_kernel(x_ref, ya_ref, yb_ref, info_ref, g_ref, o_ref):
    xv = _moe_combine(x_ref[...], ya_ref, yb_ref, info_ref)
    ms = jnp.mean(xv * xv, axis=-1, keepdims=True)
    o_ref[...] = xv * lax.rsqrt(ms + RMS_EPS) * g_ref[...]


def _final_norm(x, y2, info, g, row0, n_rows):
    t, d = x.shape
    tm = _pick(n_rows, (256, 128, 8))
    assert row0 % tm == 0 and t % tm == 0
    b0, bt = row0 // tm, t // tm
    return pl.pallas_call(
        _final_norm_kernel,
        grid=(n_rows // tm,),
        in_specs=[pl.BlockSpec((tm, d), lambda i: (i + b0, 0)),
                  pl.BlockSpec((tm, d), lambda i: (i + b0, 0)),
                  pl.BlockSpec((tm, d), lambda i: (i + b0 + bt, 0)),
                  pl.BlockSpec((tm, LANES), lambda i: (i + b0, 0)),
                  pl.BlockSpec((1, d), lambda i: (0, 0))],
        out_specs=pl.BlockSpec((tm, d), lambda i: (i, 0)),
        out_shape=jax.ShapeDtypeStruct((n_rows, d), F32),
        compiler_params=_cparams(("parallel",)),
    )(x, y2, y2, info, g.reshape(1, d))


def _swa_prompt_kernel(sink_ref, q_ref, kp_ref, kc_ref, vp_ref, vc_ref, o_ref):
    n = pl.program_id(1)
    blk = WINDOW
    qi = lax.broadcasted_iota(jnp.int32, (blk, 2 * blk), 0)
    sj = lax.broadcasted_iota(jnp.int32, (blk, 2 * blk), 1)
    rel = qi + blk - sj
    first_key = jnp.where(n > 0, 0, blk)
    valid = (rel >= 0) & (rel < WINDOW) & (sj >= first_key)
    k = jnp.concatenate([kp_ref[...], kc_ref[...]], axis=0).astype(BF16)
    v = jnp.concatenate([vp_ref[...], vc_ref[...]], axis=0).astype(BF16)
    hd = SWA_HEAD_DIM
    for kh in range(SWA_KV_HEADS):
        k_h = k[:, kh * hd:(kh + 1) * hd]
        v_h = v[:, kh * hd:(kh + 1) * hd]
        for g2 in range(SWA_GROUP // 2):
            pair = []
            for gg in range(2):
                h = kh * SWA_GROUP + g2 * 2 + gg
                q_h = q_ref[:, h * hd:(h + 1) * hd].astype(BF16)
                s = _dot_t(q_h, k_h) * SWA_SCALE
                s = jnp.where(valid, s, NEG_INF)
                sink = sink_ref[h]
                m = jnp.maximum(jnp.max(s, axis=-1, keepdims=True), sink)
                p = jnp.exp(s - m)
                denom = jnp.sum(p, axis=-1, keepdims=True) + jnp.exp(sink - m)
                pair.append(_dot(p.astype(BF16), v_h) / denom)
            h0 = kh * SWA_GROUP + g2 * 2
            o_ref[:, h0 * hd:(h0 + 2) * hd] = jnp.concatenate(pair, axis=1).astype(o_ref.dtype)


def _swa_prompt(qkv, sinks, batch, seq):
    nq = SWA_HEADS * SWA_HEAD_DIM
    nk = SWA_KV_HEADS * SWA_HEAD_DIM
    nb = seq // WINDOW
    kcol, vcol = nq // nk, nq // nk + 1

    def prev(b, n):
        return b * nb + jnp.maximum(n - 1, 0)

    return pl.pallas_call(
        _swa_prompt_kernel,
        grid=(batch, nb),
        in_specs=[pl.BlockSpec(memory_space=pltpu.SMEM),
                  pl.BlockSpec((WINDOW, nq), lambda b, n: (b * nb + n, 0)),
                  pl.BlockSpec((WINDOW, nk), lambda b, n: (prev(b, n), kcol)),
                  pl.BlockSpec((WINDOW, nk), lambda b, n: (b * nb + n, kcol)),
                  pl.BlockSpec((WINDOW, nk), lambda b, n: (prev(b, n), vcol)),
                  pl.BlockSpec((WINDOW, nk), lambda b, n: (b * nb + n, vcol))],
        out_specs=pl.BlockSpec((WINDOW, nq), lambda b, n: (b * nb + n, 0)),
        out_shape=jax.ShapeDtypeStruct((batch * seq, nq), BF16),
        compiler_params=_cparams(("parallel", "arbitrary")),
    )(sinks, qkv, qkv, qkv, qkv, qkv)


def _swa_sample_kernel(q_ref, kn_ref, vn_ref, ck_ref, cv_ref, sink_ref, o_ref, nk_ref, nv_ref,
                       kall, vall, *, n_new, seqs):
    w = WINDOW
    rows = q_ref.shape[1]
    zeros = jnp.zeros((w, kall.shape[1]), F32)
    kall[w:2 * w, :] = zeros
    vall[w:2 * w, :] = zeros
    r = lax.broadcasted_iota(jnp.int32, (rows, 2 * w), 0)
    j = lax.broadcasted_iota(jnp.int32, (rows, 2 * w), 1)
    qn = r % n_new
    valid = ((j < w) & (j > qn)) | ((j >= w) & (j - w <= qn))
    sink = sink_ref[:, 0:1]

    def body(s, carry):
        kall[0:w, :] = ck_ref[s]
        vall[0:w, :] = cv_ref[s]
        kall[w:w + n_new, :] = kn_ref[s]
        vall[w:w + n_new, :] = vn_ref[s]
        nk_ref[s] = kall[n_new:n_new + w, :]
        nv_ref[s] = vall[n_new:n_new + w, :]
        lg = _dot_t(q_ref[s], kall[...].astype(BF16)) * SWA_SCALE
        lg = jnp.where(valid, lg, NEG_INF)
        m = jnp.maximum(jnp.max(lg, axis=-1, keepdims=True), sink)
        p = jnp.exp(lg - m)
        denom = jnp.sum(p, axis=-1, keepdims=True) + jnp.exp(sink - m)
        o_ref[s] = (_dot(p.astype(BF16), vall[...].astype(BF16)) / denom).astype(o_ref.dtype)
        return carry

    lax.fori_loop(0, seqs, body, 0)


def _swa_sample(q, k_new, v_new, cache_k, cache_v, sink_rows):
    b, rows, _ = q.shape
    n_new = k_new.shape[1]
    w, kw = cache_k.shape[1], cache_k.shape[2]
    assert w == WINDOW
    seqs = _pick(b, (8, 4, 2, 1))
    return pl.pallas_call(
        functools.partial(_swa_sample_kernel, n_new=n_new, seqs=seqs),
        grid=(b // seqs,),
        in_specs=[pl.BlockSpec((seqs, rows, kw), lambda i: (i, 0, 0)),
                  pl.BlockSpec((seqs, n_new, kw), lambda i: (i, 0, 0)),
                  pl.BlockSpec((seqs, n_new, kw), lambda i: (i, 0, 0)),
                  pl.BlockSpec((seqs, w, kw), lambda i: (i, 0, 0)),
                  pl.BlockSpec((seqs, w, kw), lambda i: (i, 0, 0)),
                  pl.BlockSpec((rows, LANES), lambda i: (0, 0))],
        out_specs=[pl.BlockSpec((seqs, rows, kw), lambda i: (i, 0, 0)),
                   pl.BlockSpec((seqs, w, kw), lambda i: (i, 0, 0)),
                   pl.BlockSpec((seqs, w, kw), lambda i: (i, 0, 0))],
        out_shape=[jax.ShapeDtypeStruct((b, rows, kw), BF16),
                   jax.ShapeDtypeStruct((b, w, kw), F32),
                   jax.ShapeDtypeStruct((b, w, kw), F32)],
        scratch_shapes=[pltpu.VMEM((2 * w, kw), F32), pltpu.VMEM((2 * w, kw), F32)],
        compiler_params=_cparams(("parallel",)),
    )(q, k_new, v_new, cache_k, cache_v, sink_rows)


def _mem_heads(q, k_of, v_of):
    hd = MEM_HEAD_DIM
    outs = []
    for h in range(MEM_HEADS):
        s = _dot_t(q[:, h * hd:(h + 1) * hd], k_of(h)) * MEM_SCALE
        m = jnp.max(s, axis=-1, keepdims=True)
        p = jnp.exp(s - m)
        denom = jnp.sum(p, axis=-1, keepdims=True)
        outs.append(_dot(p.astype(BF16), v_of(h)) / denom)
    return jnp.concatenate(outs, axis=1)


def _mem_prompt_kernel(q_ref, k_ref, v_ref, o_ref):
    hd = MEM_HEAD_DIM
    k = k_ref[...].astype(BF16)
    v = v_ref[...].astype(BF16)
    o_ref[...] = _mem_heads(q_ref[...], lambda h: k[:, h * hd:(h + 1) * hd],
                            lambda h: v[:, h * hd:(h + 1) * hd]).astype(o_ref.dtype)


def _mem_prompt(q, kv, batch, seq):
    width = MEM_HEADS * MEM_HEAD_DIM
    m = kv.shape[0] // batch
    tq = _pick(seq, (512, 256, 128))
    nq = seq // tq
    return pl.pallas_call(
        _mem_prompt_kernel,
        grid=(batch, nq),
        in_specs=[pl.BlockSpec((tq, width), lambda b, i: (b * nq + i, 0)),
                  pl.BlockSpec((m, width), lambda b, i: (b, 0)),
                  pl.BlockSpec((m, width), lambda b, i: (b, 1))],
        out_specs=pl.BlockSpec((tq, width), lambda b, i: (b * nq + i, 0)),
        out_shape=jax.ShapeDtypeStruct((batch * seq, width), BF16),
        compiler_params=_cparams(("parallel", "arbitrary")),
    )(q, kv, kv)


def _mem_sample_kernel(q_ref, k_ref, v_ref, o_ref, *, seqs):
    nh = MEM_HEADS
    qrows, cols = q_ref.shape[1], k_ref.shape[1]
    row_h = lax.broadcasted_iota(jnp.int32, (qrows, cols), 0) // (qrows // nh)
    col_h = lax.broadcasted_iota(jnp.int32, (qrows, cols), 1) % nh
    own = row_h == col_h

    def body(s, carry):
        lg = _dot_t(q_ref[s], k_ref[s].astype(BF16)) * MEM_SCALE
        lg = jnp.where(own, lg, NEG_INF)
        mx = jnp.max(lg, axis=-1, keepdims=True)
        p = jnp.exp(lg - mx)
        denom = jnp.sum(p, axis=-1, keepdims=True)
        o_ref[s] = (_dot(p.astype(BF16), v_ref[s].astype(BF16)) / denom).astype(o_ref.dtype)
        return carry

    lax.fori_loop(0, seqs, body, 0)


def _mem_sample(q, k, v, layer):
    b, qrows, hd = q.shape
    mh = k.shape[1]
    seqs = _pick(b, (8, 4, 2, 1))
    nb = b // seqs
    return pl.pallas_call(
        functools.partial(_mem_sample_kernel, seqs=seqs),
        grid=(nb,),
        in_specs=[pl.BlockSpec((seqs, qrows, hd), lambda i: (i, 0, 0)),
                  pl.BlockSpec((seqs, mh, hd), lambda i: (layer * nb + i, 0, 0)),
                  pl.BlockSpec((seqs, mh, hd), lambda i: (layer * nb + i, 0, 0))],
        out_specs=pl.BlockSpec((seqs, qrows, hd), lambda i: (i, 0, 0)),
        out_shape=jax.ShapeDtypeStruct((b, qrows, hd), BF16),
        compiler_params=_cparams(("parallel",)),
    )(q, k, v)


def _sb_local(z, mask, tri2):
    sp = jnp.maximum(z, 0.0) + jnp.log(1.0 + jnp.exp(-jnp.abs(z)))
    spm = sp if mask is None else jnp.where(mask, sp, 0.0)
    cum = _dot(spm.astype(BF16), tri2)
    kb = z.shape[1]
    a = z - sp - cum[:, :kb]
    if mask is not None:
        a = jnp.where(mask, a, NEG_INF)
    return a, -cum[:, kb:]


def _tri2(kb):
    r = lax.broadcasted_iota(jnp.int32, (kb, 2 * kb), 0)
    cidx = lax.broadcasted_iota(jnp.int32, (kb, 2 * kb), 1)
    return jnp.where((cidx >= kb) | (r > cidx), 1.0, 0.0).astype(BF16)


def _sb_prompt_kernel(bias_ref, q_ref, k_ref, v_ref, o_ref, kb_scr, vb_scr, c_scr, acc_scr):
    kh = pl.program_id(1)
    n = pl.program_id(2)
    blk = SB_BLOCK
    hd = SB_HEAD_DIM

    @pl.when(n == 0)
    def _():
        kb_scr[...] = k_ref[...].astype(BF16)
        vb_scr[...] = v_ref[...].astype(BF16)

    q = q_ref[...]
    qs = jnp.concatenate([q[:, g * hd:(g + 1) * hd] for g in range(SB_GROUP)], axis=0).astype(BF16)
    bias = jnp.concatenate(
        [jnp.full((blk, LANES), bias_ref[kh * SB_GROUP + g], F32) for g in range(SB_GROUP)], axis=0)
    tri2 = _tri2(blk)
    rows = SB_GROUP * blk
    qi = lax.broadcasted_iota(jnp.int32, (rows, blk), 0) % blk
    sj = lax.broadcasted_iota(jnp.int32, (rows, blk), 1)
    diag_mask = sj < qi

    def local(kblk, mask):
        start = pl.multiple_of(kblk * blk, blk)
        z = _dot_t(qs, kb_scr[pl.ds(start, blk), :]) * SB_SCALE + bias
        return _sb_local(z, mask, tri2)

    def accumulate(kblk, a, tot):
        start = pl.multiple_of(kblk * blk, blk)
        wgt = jnp.exp(a + c_scr[...])
        acc_scr[...] += _dot(wgt.astype(BF16), vb_scr[pl.ds(start, blk), :])
        c_scr[...] += tot

    c_scr[...] = jnp.zeros_like(c_scr)
    acc_scr[...] = jnp.zeros_like(acc_scr)
    accumulate(n, *local(n, diag_mask))

    def body(i, carry):
        k1 = n - 1 - 2 * i
        l1 = local(k1, None)
        l2 = local(k1 - 1, None)
        accumulate(k1, *l1)
        accumulate(k1 - 1, *l2)
        return carry

    lax.fori_loop(0, n // 2, body, 0)

    @pl.when(n % 2 == 1)
    def _():
        accumulate(0, *local(0, None))

    acc = acc_scr[...]
    o_ref[...] = jnp.concatenate([acc[g * blk:(g + 1) * blk] for g in range(SB_GROUP)],
                                 axis=1).astype(o_ref.dtype)


def _sb_prompt(qkv, bias, batch, seq):
    hd = SB_HEAD_DIM
    nb = seq // SB_BLOCK
    gw = SB_GROUP * hd
    kcol0 = SB_HEADS
    vcol0 = SB_HEADS + SB_KV_HEADS
    return pl.pallas_call(
        _sb_prompt_kernel,
        grid=(batch, SB_KV_HEADS, nb),
        in_specs=[pl.BlockSpec(memory_space=pltpu.SMEM),
                  pl.BlockSpec((SB_BLOCK, gw), lambda b, kh, n: (b * nb + n, kh)),
                  pl.BlockSpec((seq, hd), lambda b, kh, n: (b, kcol0 + kh)),
                  pl.BlockSpec((seq, hd), lambda b, kh, n: (b, vcol0 + kh))],
        out_specs=pl.BlockSpec((SB_BLOCK, gw), lambda b, kh, n: (b * nb + n, kh)),
        out_shape=jax.ShapeDtypeStruct((batch * seq, SB_HEADS * hd), BF16),
        scratch_shapes=[pltpu.VMEM((seq, hd), BF16), pltpu.VMEM((seq, hd), BF16),
                        pltpu.VMEM((SB_GROUP * SB_BLOCK, LANES), F32),
                        pltpu.VMEM((SB_GROUP * SB_BLOCK, hd), F32)],
        compiler_params=_cparams(("parallel", "parallel", "arbitrary")),
    )(bias, qkv, qkv, qkv)


def _sb_sample_kernel(pt_ref, q_ref, kn_ref, vn_ref, bias_ref, kpool, vpool, o_ref,
                      kbuf, vbuf, kpad, vpad, c_scr, acc_scr, ksem, vsem, *, n_new, pps, n_pages, page0):
    seq = pl.program_id(0)
    step = pl.program_id(1)
    steps = pl.num_programs(1)
    gstep = seq * steps + step
    slot = gstep % 2
    nkv = SB_KV_HEADS
    page = SB_BLOCK
    rows = SB_GROUP * n_new
    qr = nkv * rows
    tri2 = _tri2(page)
    q = q_ref[0]
    bias = bias_ref[...]

    def start_fetch(sq, chunk, sl):
        for i in range(pps):
            pg = page0 + pt_ref[sq * n_pages + (n_pages - 1 - (chunk * pps + i))]
            pltpu.make_async_copy(kpool.at[pg], kbuf.at[sl, i], ksem.at[sl]).start()
            pltpu.make_async_copy(vpool.at[pg], vbuf.at[sl, i], vsem.at[sl]).start()

    def wait_fetch(sl):
        pltpu.make_async_copy(kpool.at[pl.ds(0, pps)], kbuf.at[sl], ksem.at[sl]).wait()
        pltpu.make_async_copy(vpool.at[pl.ds(0, pps)], vbuf.at[sl], vsem.at[sl]).wait()

    @pl.when(gstep == 0)
    def _():
        start_fetch(0, 0, 0)

    @pl.when(gstep + 1 < pl.num_programs(0) * steps)
    def _():
        wrap = step + 1 == steps
        start_fetch(jnp.where(wrap, seq + 1, seq), jnp.where(wrap, 0, step + 1), 1 - slot)

    def head(ref, kh):
        return ref[pl.ds(kh, page, stride=nkv), :].astype(BF16)

    def local(page_refs, mask):
        npg = len(page_refs)
        z = jnp.concatenate(
            [_dot_t(q[kh * rows:(kh + 1) * rows], head(k_ref, kh))
             for k_ref, _ in page_refs for kh in range(nkv)], axis=0)
        z = z * SB_SCALE + jnp.concatenate([bias] * npg, axis=0)
        a, tot = _sb_local(z, mask, tri2)
        wb = jnp.exp(a).astype(BF16)
        outs = [jnp.concatenate(
            [_dot(wb[p * qr + kh * rows:p * qr + (kh + 1) * rows], head(v_ref, kh))
             for kh in range(nkv)], axis=0) for p, (_, v_ref) in enumerate(page_refs)]
        return outs, [tot[p * qr:(p + 1) * qr] for p in range(npg)]

    def combine(parts):
        off = c_scr[...]
        acc = acc_scr[...]
        for outs, tots in parts:
            for o_p, t_p in zip(outs, tots):
                acc = acc + jnp.exp(off) * o_p
                off = off + t_p
        acc_scr[...] = acc
        c_scr[...] = off

    @pl.when(step == 0)
    def _():
        c_scr[...] = jnp.zeros_like(c_scr)
        acc_scr[...] = jnp.zeros_like(acc_scr)
        kpad[...] = jnp.zeros_like(kpad)
        vpad[...] = jnp.zeros_like(vpad)
        kpad[0:nkv * n_new, :] = kn_ref[0]
        vpad[0:nkv * n_new, :] = vn_ref[0]
        r = lax.broadcasted_iota(jnp.int32, (qr, page), 0)
        j = lax.broadcasted_iota(jnp.int32, (qr, page), 1)
        combine([local([(kpad, vpad)], j < (r % n_new))])

    wait_fetch(slot)
    pages = [(kbuf.at[slot, i], vbuf.at[slot, i]) for i in range(pps)]
    chain = min(SB_PAGE_CHAIN, pps)
    combine([local(pages[c:c + chain], None) for c in range(0, pps, chain)])

    @pl.when(step == steps - 1)
    def _():
        o_ref[0] = acc_scr[...].astype(o_ref.dtype)


def _sb_sample(q, k_new, v_new, k_pool, v_pool, page_table, bias_rows, page0):
    b, qrows, hd = q.shape
    new_rows = k_new.shape[1]
    n_new = new_rows // SB_KV_HEADS
    n_pages = page_table.shape[1]
    prow = k_pool.shape[1]
    assert prow == SB_BLOCK * SB_KV_HEADS
    pps = _pick(n_pages, (SB_PAGES_PER_STEP, 8, 4, 2, 1))
    grid_spec = pltpu.PrefetchScalarGridSpec(
        num_scalar_prefetch=1,
        grid=(b, n_pages // pps),
        in_specs=[pl.BlockSpec((1, qrows, hd), lambda s, c, pt: (s, 0, 0)),
                  pl.BlockSpec((1, new_rows, hd), lambda s, c, pt: (s, 0, 0)),
                  pl.BlockSpec((1, new_rows, hd), lambda s, c, pt: (s, 0, 0)),
                  pl.BlockSpec((qrows, LANES), lambda s, c, pt: (0, 0)),
                  pl.BlockSpec(memory_space=pl.ANY),
                  pl.BlockSpec(memory_space=pl.ANY)],
        out_specs=pl.BlockSpec((1, qrows, hd), lambda s, c, pt: (s, 0, 0)),
        scratch_shapes=[pltpu.VMEM((2, pps, prow, hd), F32), pltpu.VMEM((2, pps, prow, hd), F32),
                        pltpu.VMEM((prow, hd), F32), pltpu.VMEM((prow, hd), F32),
                        pltpu.VMEM((qrows, LANES), F32), pltpu.VMEM((qrows, hd), F32),
                        pltpu.SemaphoreType.DMA((2,)), pltpu.SemaphoreType.DMA((2,))],
    )
    return pl.pallas_call(
        functools.partial(_sb_sample_kernel, n_new=n_new, pps=pps, n_pages=n_pages, page0=page0),
        grid_spec=grid_spec,
        out_shape=jax.ShapeDtypeStruct((b, qrows, hd), BF16),
        compiler_params=_cparams(("arbitrary", "arbitrary")),
    )(page_table.reshape(-1), q, k_new, v_new, bias_rows, k_pool, v_pool)


def _router_kernel(x_ref, g_ref, wh_ref, wl_ref, b_ref, h_ref, info_ref):
    xv = x_ref[...]
    ms = jnp.mean(xv * xv, axis=-1, keepdims=True)
    h = xv * lax.rsqrt(ms + RMS_EPS) * g_ref[...]
    h_ref[...] = h
    h_hi = h.astype(BF16)
    h_lo = (h - h_hi.astype(F32)).astype(BF16)
    logits = _dot(h_hi, wh_ref[...]) + _dot(h_hi, wl_ref[...]) + _dot(h_lo, wh_ref[...]) + b_ref[...]
    lane = lax.broadcasted_iota(jnp.int32, logits.shape, 1)
    big = jnp.int32(1 << 20)
    is_g = (lane >= N_EXPERTS) & (lane < N_EXPERTS + N_GROUPS)
    gl = jnp.where(is_g, logits, -jnp.inf)
    gmax = jnp.max(gl, axis=-1, keepdims=True)
    gidx = jnp.min(jnp.where(gl == gmax, lane - N_EXPERTS, big), axis=-1, keepdims=True)
    p_group = 1.0 / jnp.sum(jnp.exp(gl - gmax), axis=-1, keepdims=True)
    lo = gidx * EXPERTS_PER_GROUP
    in_grp = (lane >= lo) & (lane < lo + EXPERTS_PER_GROUP)
    el = jnp.where(in_grp, logits, -jnp.inf)
    emax = jnp.max(el, axis=-1, keepdims=True)
    pe = jnp.exp(el - emax)
    prob = pe / jnp.sum(pe, axis=-1, keepdims=True)
    prob = jnp.where(in_grp, prob, -1.0)
    p1 = jnp.max(prob, axis=-1, keepdims=True)
    i1 = jnp.min(jnp.where(prob == p1, lane, big), axis=-1, keepdims=True)
    rest = jnp.where(lane == i1, -1.0, prob)
    p2 = jnp.max(rest, axis=-1, keepdims=True)
    i2 = jnp.min(jnp.where(rest == p2, lane, big), axis=-1, keepdims=True)
    tot = p1 + p2
    w1 = p1 / tot * p_group
    w2 = p2 / tot * p_group
    info = jnp.where(lane == 0, i1.astype(F32),
                     jnp.where(lane == 1, i2.astype(F32),
                               jnp.where(lane == 2, w1, jnp.where(lane == 3, w2, 0.0))))
    info_ref[...] = info


def _router(x, g, w_router, b_router, w_group, b_group):
    t, d = x.shape
    pad = LANES - N_EXPERTS - N_GROUPS
    w = jnp.concatenate([w_router, w_group, jnp.zeros((d, pad), F32)], axis=1)
    bias = jnp.concatenate([b_router, b_group, jnp.zeros((pad,), F32)]).reshape(1, LANES)
    w_hi = w.astype(BF16)
    w_lo = (w - w_hi.astype(F32)).astype(BF16)
    tm = _pick(t, (256, 128, 8))
    return pl.pallas_call(
        _router_kernel,
        grid=(t // tm,),
        in_specs=[pl.BlockSpec((tm, d), lambda i: (i, 0)),
                  pl.BlockSpec((1, d), lambda i: (0, 0)),
                  pl.BlockSpec((d, LANES), lambda i: (0, 0)),
                  pl.BlockSpec((d, LANES), lambda i: (0, 0)),
                  pl.BlockSpec((1, LANES), lambda i: (0, 0))],
        out_specs=[pl.BlockSpec((tm, d), lambda i: (i, 0)),
                   pl.BlockSpec((tm, LANES), lambda i: (i, 0))],
        out_shape=[jax.ShapeDtypeStruct((t, d), F32), jax.ShapeDtypeStruct((t, LANES), F32)],
        compiler_params=_cparams(("parallel",)),
    )(x, g.reshape(1, d), w_hi, w_lo, bias)


def _moe_kernel(te_ref, r0_ref, nv_ref, nused_ref, tok_ref, dst_ref, h_hbm, wg_ref, wu_ref, wd_ref, y_hbm,
                xbuf0, xbuf1, ybuf0, ybuf1, gsem, ssem, *, tm, n_pairs):
    del te_ref
    i = pl.program_id(0)
    n_valid = nv_ref[i]
    n_used = nused_ref[0]
    xbuf = (xbuf0, xbuf1)
    ybuf = (ybuf0, ybuf1)

    def start_gather(tile, sl):
        base = r0_ref[tile]
        for r in range(tm):
            pltpu.make_async_copy(h_hbm.at[pl.ds(tok_ref[base + r], 1)],
                                  xbuf[sl].at[pl.ds(r, 1)], gsem.at[sl]).start()

    def wait_gather(sl):
        pltpu.make_async_copy(h_hbm.at[pl.ds(0, tm)], xbuf[sl], gsem.at[sl]).wait()

    def start_scatter(tile, sl, valid):
        base = r0_ref[tile]
        spare = n_pairs + sl * tm
        for r in range(tm):
            dst = jnp.where(r < valid, dst_ref[base + r], spare + r)
            pltpu.make_async_copy(ybuf[sl].at[pl.ds(r, 1)], y_hbm.at[pl.ds(dst, 1)], ssem.at[sl]).start()

    def wait_scatter(sl):
        pltpu.make_async_copy(ybuf[sl], y_hbm.at[pl.ds(0, tm)], ssem.at[sl]).wait()

    @pl.when(i == 0)
    def _():
        ybuf1[...] = jnp.zeros(ybuf1.shape, ybuf1.dtype)
        pltpu.make_async_copy(ybuf1, y_hbm.at[pl.ds(n_pairs, tm)], ssem.at[0]).start()
        start_gather(0, 0)

    def used_step(sl):
        wait_gather(sl)
        wait_scatter(sl)
        start_gather(i + 1, 1 - sl)
        prev = jnp.maximum(i - 1, 0)
        start_scatter(prev, 1 - sl, jnp.where(i >= 1, nv_ref[prev], 0))
        x = xbuf[sl][...].astype(BF16)
        a = _dot(x, wg_ref[0].astype(BF16))
        u = _dot(x, wu_ref[0].astype(BF16))
        act = a * jax.nn.sigmoid(a) * u
        ybuf[sl][...] = _dot(act.astype(BF16), wd_ref[0].astype(BF16))

    def flush_step(sl):
        last = i - 1
        wait_gather(sl)
        wait_scatter(sl)
        start_scatter(last, 1 - sl, nv_ref[last])
        wait_scatter(1 - sl)

    for sl in range(2):
        pl.when((n_valid > 0) & (i % 2 == sl))(functools.partial(used_step, sl))
        pl.when((i == n_used) & (i % 2 == sl))(functools.partial(flush_step, sl))


def _lookup(table, idx):
    n = table.shape[0]
    hit = idx[:, None] == jnp.arange(n, dtype=jnp.int32)[None, :]
    return jnp.sum(jnp.where(hit, table[None, :], 0), axis=1)


def _moe(h, info, w_gate, w_up, w_down, layer):
    t, d = h.shape
    n_exp, ff = N_EXPERTS, w_gate.shape[-1]
    tm = MOE_TILE
    pairs = 2 * t
    n_tiles = -(-pairs // tm) + n_exp + 1
    eid = jnp.concatenate([info[:, 0], info[:, 1]]).astype(jnp.int32)
    sorted_eid, order = lax.sort_key_val(eid, jnp.arange(pairs, dtype=jnp.int32))
    bounds = jnp.sum((sorted_eid[None, :] < jnp.arange(n_exp + 1, dtype=jnp.int32)[:, None]).astype(jnp.int32),
                     axis=1)
    starts, counts = bounds[:-1], bounds[1:] - bounds[:-1]
    tiles_e = (counts + tm - 1) // tm
    tile_end = jnp.cumsum(tiles_e)
    n_used = tile_end[-1]
    tile = jnp.arange(n_tiles, dtype=jnp.int32)
    tile_e = jnp.minimum(jnp.sum((tile[:, None] >= tile_end[None, :]).astype(jnp.int32), axis=1), n_exp - 1)
    used = tile < n_used
    k_in_e = tile - _lookup(tile_end - tiles_e, tile_e)
    tile_r0 = jnp.where(used, _lookup(starts, tile_e) + k_in_e * tm, 0).astype(jnp.int32)
    tile_valid = jnp.where(used, jnp.clip(_lookup(counts, tile_e) - k_in_e * tm, 0, tm), 0).astype(jnp.int32)
    last_e = jnp.sum(jnp.where(tile == n_used - 1, tile_e, 0))
    tile_e = jnp.where(used, tile_e, last_e).astype(jnp.int32)
    tok_sorted = jnp.pad(order % t, (0, tm))
    dst_sorted = jnp.pad(order, (0, tm))

    wg = w_gate.reshape((-1,) + w_gate.shape[-2:])
    wu = w_up.reshape((-1,) + w_up.shape[-2:])
    wd = w_down.reshape((-1,) + w_down.shape[-2:])
    e0 = layer * n_exp

    def w_map(i, te, *_):
        return (e0 + te[i], 0, 0)

    grid_spec = pltpu.PrefetchScalarGridSpec(
        num_scalar_prefetch=6,
        grid=(n_tiles,),
        in_specs=[pl.BlockSpec(memory_space=pl.ANY),
                  pl.BlockSpec((1, d, ff), w_map),
                  pl.BlockSpec((1, d, ff), w_map),
                  pl.BlockSpec((1, ff, d), w_map)],
        out_specs=pl.BlockSpec(memory_space=pl.ANY),
        scratch_shapes=[pltpu.VMEM((tm, d), F32)] * 4
                       + [pltpu.SemaphoreType.DMA((2,)), pltpu.SemaphoreType.DMA((2,))],
    )
    return pl.pallas_call(
        functools.partial(_moe_kernel, tm=tm, n_pairs=pairs),
        grid_spec=grid_spec,
        out_shape=jax.ShapeDtypeStruct((pairs + 2 * tm, d), F32),
        compiler_params=_cparams(("arbitrary",)),
    )(tile_e, tile_r0, tile_valid, n_used.reshape(1).astype(jnp.int32), tok_sorted, dst_sorted, h, wg, wu, wd)


def _rope_tables(pos):
    half = ROPE_DIM // 2
    inv_freq = ROPE_THETA ** (-2.0 * jnp.arange(half, dtype=F32) / ROPE_DIM)
    ang = pos.astype(F32)[:, None] * inv_freq[None, :]
    cos, sin = jnp.cos(ang), jnp.sin(ang)
    t = pos.shape[0]
    ones = jnp.ones((t, SWA_HEAD_DIM - ROPE_DIM), F32)
    zeros = jnp.zeros((t, half), F32)
    zrest = jnp.zeros((t, SWA_HEAD_DIM - ROPE_DIM), F32)
    c_head = jnp.concatenate([cos, cos, ones], axis=1)
    plus_head = jnp.concatenate([zeros, sin, zrest], axis=1)
    minus_head = jnp.concatenate([-sin, zeros, zrest], axis=1)
    rep = LANES // SWA_HEAD_DIM
    return tuple(jnp.concatenate([a] * rep, axis=1) for a in (c_head, plus_head, minus_head))


def kernel(x_prompt, x_sample, mem_prompt, cache_swa_k, cache_swa_v, cache_sb_k, cache_sb_v, cache_mem_k, cache_mem_v, page_table, norm_mix, w_in_swa, sinks_swa, w_out_swa, w_in_sb, sb_bias, w_out_sb, norm_mem_q, norm_mem_kv, w_mem_q, w_mem_kv, w_mem_o, norm_ffn, w_group, b_group, w_router, b_router, w_gate, w_up, w_down, norm_final):
    b_p, s_p, d = x_prompt.shape
    b_s, n_s, _ = x_sample.shape
    depth = norm_mix.shape[0]
    t_p, t_s = b_p * s_p, b_s * n_s
    t = t_p + t_s
    past_len = page_table.shape[1] * cache_sb_k.shape[2]
    assert s_p >= WINDOW and s_p % WINDOW == 0

    x = (x_prompt.reshape(t_p, d), x_sample.reshape(t_s, d))
    pos = jnp.concatenate([jnp.tile(jnp.arange(s_p), b_p), jnp.tile(past_len + jnp.arange(n_s), b_s)])
    rope = _rope_tables(pos)
    mem_flat = mem_prompt.reshape(-1, d)
    mem_m = mem_prompt.shape[1]
    mem_w = MEM_HEADS * MEM_HEAD_DIM
    sb_pool_k = cache_sb_k.reshape(-1, cache_sb_k.shape[2] * SB_KV_HEADS, SB_HEAD_DIM)
    sb_pool_v = cache_sb_v.reshape(-1, cache_sb_v.shape[2] * SB_KV_HEADS, SB_HEAD_DIM)
    mem_cache_k = cache_mem_k.reshape(-1, cache_mem_k.shape[2] * MEM_HEADS, MEM_HEAD_DIM)
    mem_cache_v = cache_mem_v.reshape(-1, cache_mem_v.shape[2] * MEM_HEADS, MEM_HEAD_DIM)

    swa_kp, swa_vp, swa_ks, swa_vs = [], [], [], []
    sb_kp, sb_vp, sb_ks, sb_vs = [], [], [], []
    mem_kp, mem_vp = [], []
    moe = None
    for i in range(depth):
        j = i // 2
        first = moe is None
        if i % 2 == 0:
            nq = SWA_HEADS * SWA_HEAD_DIM
            nk = SWA_KV_HEADS * SWA_HEAD_DIM
            res = _norm_matmul(x, norm_mix[i], w_in_swa[j].astype(BF16), moe=moe, rope=rope,
                               rope_cols=nq + nk, emit_x=not first)
            qkv, x = (res, x) if first else res
            o_p = _swa_prompt(qkv, sinks_swa[j], b_p, s_p)
            qs = qkv[t_p:, :nq].reshape(b_s, n_s, SWA_KV_HEADS, SWA_GROUP, SWA_HEAD_DIM)
            qs = qs.transpose(0, 2, 3, 1, 4).reshape(b_s, SWA_KV_HEADS, SWA_GROUP * n_s, 1, SWA_HEAD_DIM)
            own = jnp.eye(SWA_KV_HEADS, dtype=F32)[None, :, None, :, None]
            qs = (qs * own).astype(BF16).reshape(b_s, SWA_HEADS * n_s, nk)
            kn = qkv[t_p:, nq:nq + nk].reshape(b_s, n_s, nk)
            vn = qkv[t_p:, nq + nk:].reshape(b_s, n_s, nk)
            sink_rows = jnp.broadcast_to(jnp.repeat(sinks_swa[j], n_s)[:, None], (SWA_HEADS * n_s, LANES))
            o_s, kbuf, vbuf = _swa_sample(qs, kn, vn, cache_swa_k[j].reshape(b_s, -1, nk),
                                          cache_swa_v[j].reshape(b_s, -1, nk), sink_rows)
            o_s = o_s.reshape(b_s, SWA_KV_HEADS, SWA_GROUP, n_s, SWA_KV_HEADS, SWA_HEAD_DIM)
            o_s = jnp.sum(o_s * own[:, :, :, None].astype(BF16), axis=4)
            o_s = o_s.transpose(0, 3, 1, 2, 4).reshape(t_s, nq)
            kp = qkv[:t_p, nq:nq + nk].reshape(b_p, s_p, SWA_KV_HEADS, SWA_HEAD_DIM)
            vp = qkv[:t_p, nq + nk:].reshape(b_p, s_p, SWA_KV_HEADS, SWA_HEAD_DIM)
            swa_kp.append(kp[:, -WINDOW:])
            swa_vp.append(vp[:, -WINDOW:])
            swa_ks.append(kbuf.reshape(b_s, -1, SWA_KV_HEADS, SWA_HEAD_DIM))
            swa_vs.append(vbuf.reshape(b_s, -1, SWA_KV_HEADS, SWA_HEAD_DIM))
            w_out = w_out_swa[j]
        else:
            nq = SB_HEADS * SB_HEAD_DIM
            nk = SB_KV_HEADS * SB_HEAD_DIM
            res = _norm_matmul(x, norm_mix[i], w_in_sb[j].astype(BF16), moe=moe, emit_x=not first)
            qkv, x = (res, x) if first else res
            o_p = _sb_prompt(qkv, sb_bias[j], b_p, s_p)
            qs = qkv[t_p:, :nq].reshape(b_s, n_s, SB_KV_HEADS, SB_GROUP, SB_HEAD_DIM)
            qs = qs.transpose(0, 2, 3, 1, 4).reshape(b_s, SB_HEADS * n_s, SB_HEAD_DIM).astype(BF16)
            kn = qkv[t_p:, nq:nq + nk].reshape(b_s, n_s * SB_KV_HEADS, SB_HEAD_DIM)
            vn = qkv[t_p:, nq + nk:].reshape(b_s, n_s * SB_KV_HEADS, SB_HEAD_DIM)
            bias_rows = jnp.broadcast_to(jnp.repeat(sb_bias[j], n_s)[:, None], (SB_HEADS * n_s, LANES))
            o_s = _sb_sample(qs, kn, vn, sb_pool_k, sb_pool_v, page_table, bias_rows,
                             j * cache_sb_k.shape[1])
            o_s = o_s.reshape(b_s, SB_KV_HEADS, SB_GROUP, n_s, SB_HEAD_DIM)
            o_s = o_s.transpose(0, 3, 1, 2, 4).reshape(t_s, nq)
            sb_kp.append(qkv[:t_p, nq:nq + nk].reshape(b_p, s_p, SB_KV_HEADS, SB_HEAD_DIM))
            sb_vp.append(qkv[:t_p, nq + nk:].reshape(b_p, s_p, SB_KV_HEADS, SB_HEAD_DIM))
            sb_ks.append(kn.reshape(b_s, n_s, SB_KV_HEADS, SB_HEAD_DIM))
            sb_vs.append(vn.reshape(b_s, n_s, SB_KV_HEADS, SB_HEAD_DIM))
            w_out = w_out_sb[j]
        x = _resid_matmul((o_p, o_s), w_out.astype(BF16), x)

        mkv = _norm_matmul(mem_flat, norm_mem_kv[i], w_mem_kv[i].astype(BF16))
        mem_kp.append(mkv[:, :mem_w].reshape(b_p, mem_m, MEM_HEADS, MEM_HEAD_DIM))
        mem_vp.append(mkv[:, mem_w:].reshape(b_p, mem_m, MEM_HEADS, MEM_HEAD_DIM))
        qm = _norm_matmul(x, norm_mem_q[i], w_mem_q[i].astype(BF16), out_dtype=BF16)
        om_p = _mem_prompt(qm, mkv, b_p, s_p)
        row_pad = 16 - n_s % 16 if n_s % 16 else 0
        qm_s = jnp.pad(qm[t_p:].reshape(b_s, n_s, MEM_HEADS, MEM_HEAD_DIM), ((0, 0), (0, row_pad), (0, 0), (0, 0)))
        qm_s = qm_s.transpose(0, 2, 1, 3).reshape(b_s, MEM_HEADS * (n_s + row_pad), MEM_HEAD_DIM)
        om_s = _mem_sample(qm_s, mem_cache_k, mem_cache_v, i)
        om_s = om_s.reshape(b_s, MEM_HEADS, n_s + row_pad, MEM_HEAD_DIM)[:, :, :n_s].transpose(0, 2, 1, 3)
        x = _resid_matmul((om_p, om_s.reshape(t_s, mem_w)), w_mem_o[i].astype(BF16), x)

        h, info = _router(x, norm_ffn[i], w_router[i], b_router[i], w_group[i], b_group[i])
        moe = (_moe(h, info, w_gate, w_up, w_down, i), info)

    y_prompt = _final_norm(x, moe[0], moe[1], norm_final, 0, t_p).reshape(b_p, s_p, d)
    y_sample = _final_norm(x, moe[0], moe[1], norm_final, t_p, t_s).reshape(b_s, n_s, d)
    return (y_prompt, y_sample,
            jnp.stack(swa_kp), jnp.stack(swa_vp), jnp.stack(swa_ks), jnp.stack(swa_vs),
            jnp.stack(sb_kp), jnp.stack(sb_vp), jnp.stack(sb_ks), jnp.stack(sb_vs),
            jnp.stack(mem_kp), jnp.stack(mem_vp))
```

```python
import functools
import math

import jax
import jax.numpy as jnp
from jax import lax
from jax.experimental import pallas as pl
from jax.experimental.pallas import tpu as pltpu

F32 = jnp.float32
BF16 = jnp.bfloat16

SWA_HEADS, SWA_KV_HEADS, SWA_HEAD_DIM = 32, 4, 64
SWA_GROUP = SWA_HEADS // SWA_KV_HEADS
WINDOW = 128
ROPE_THETA = 500000.0
ROPE_DIM = SWA_HEAD_DIM // 4
SWA_SCALE = SWA_HEAD_DIM ** -0.5
SB_HEADS, SB_KV_HEADS, SB_HEAD_DIM = 16, 4, 128
SB_GROUP = SB_HEADS // SB_KV_HEADS
SB_BLOCK = 128
SB_SCALE = SB_HEAD_DIM ** -0.5
MEM_HEADS, MEM_HEAD_DIM = 4, 128
MEM_SCALE = MEM_HEAD_DIM ** -0.5
N_GROUPS, EXPERTS_PER_GROUP = 4, 8
N_EXPERTS = N_GROUPS * EXPERTS_PER_GROUP
RMS_EPS = 1e-6
NEG_INF = -1e30

LANES = 128
VMEM_LIMIT = 56 * 1024 * 1024
MOE_TILE = 256
SB_PAGES_PER_STEP = 16
SB_PAGE_CHAIN = 4


def _cparams(sem):
    return pltpu.CompilerParams(dimension_semantics=sem, vmem_limit_bytes=VMEM_LIMIT)


def _pick(n, prefs):
    for p in prefs:
        if n % p == 0:
            return p
    return n


def _dot_t(a, b):
    return lax.dot_general(a, b, (((1,), (1,)), ((), ())), preferred_element_type=F32)


def _dot(a, b):
    return jnp.dot(a, b, preferred_element_type=F32)


def _rows(refs, n_first):
    if len(refs) == 1:
        return refs[0][...]
    return jnp.where(pl.program_id(0) < n_first, refs[0][...], refs[1][...])


def _row_specs(srcs, tm, n_first):
    if not isinstance(srcs, tuple):
        return [pl.BlockSpec((tm, srcs.shape[1]), lambda i, *_: (i, 0))], [srcs]
    width = srcs[0].shape[1]
    return ([pl.BlockSpec((tm, width), lambda i, *_: (jnp.minimum(i, n_first - 1), 0)),
             pl.BlockSpec((tm, width), lambda i, *_: (jnp.maximum(i - n_first, 0), 0))], list(srcs))


def _moe_combine(xv, ya_ref, yb_ref, info_ref):
    info = info_ref[...]
    return xv + ya_ref[...] * info[:, 2:3] + yb_ref[...] * info[:, 3:4]


def _norm_matmul_kernel(*refs, n_x, n_first, add, emit_x, rope_cols, tn):
    it = iter(refs)
    x_refs = [next(it) for _ in range(n_x)]
    ya_ref = next(it) if add else None
    yb_ref = next(it) if add else None
    info_ref = next(it) if add else None
    g_ref = next(it)
    w_ref = next(it)
    if rope_cols:
        cos_ref, sp_ref, sm_ref = next(it), next(it), next(it)
    o_ref = next(it)
    xo_ref = next(it) if emit_x else None
    h_scr = next(it)
    j = pl.program_id(1)

    @pl.when(j == 0)
    def _():
        xv = _rows(x_refs, n_first)
        if add:
            xv = _moe_combine(xv, ya_ref, yb_ref, info_ref)
        if emit_x:
            xo_ref[...] = xv
        ms = jnp.mean(xv * xv, axis=-1, keepdims=True)
        h_scr[...] = (xv * lax.rsqrt(ms + RMS_EPS) * g_ref[...]).astype(BF16)

    acc = _dot(h_scr[...], w_ref[...])
    if not rope_cols:
        o_ref[...] = acc.astype(o_ref.dtype)
        return

    @pl.when(j * tn < rope_cols)
    def _():
        reps = tn // LANES
        cos = jnp.concatenate([cos_ref[...]] * reps, axis=1)
        s_plus = jnp.concatenate([sp_ref[...]] * reps, axis=1)
        s_minus = jnp.concatenate([sm_ref[...]] * reps, axis=1)
        half = ROPE_DIM // 2
        roped = (acc * cos + pltpu.roll(acc, half, axis=1) * s_plus
                 + pltpu.roll(acc, tn - half, axis=1) * s_minus)
        col = j * tn + lax.broadcasted_iota(jnp.int32, acc.shape, 1)
        o_ref[...] = jnp.where(col < rope_cols, roped, acc).astype(o_ref.dtype)

    @pl.when(j * tn >= rope_cols)
    def _():
        o_ref[...] = acc.astype(o_ref.dtype)


def _norm_matmul(x, g, w, *, moe=None, rope=None, rope_cols=0, emit_x=False,
                 out_dtype=F32):
    pair = isinstance(x, tuple)
    t = x[0].shape[0] + x[1].shape[0] if pair else x.shape[0]
    d = x[0].shape[1] if pair else x.shape[1]
    n = w.shape[1]
    add = moe is not None
    tm = _pick(math.gcd(x[0].shape[0], x[1].shape[0]) if pair else t, (512, 256))
    tn = _pick(n, (1024, 512, 256, 128) if add else (1280, 1024, 512, 256, 128))
    n_i = t // tm
    n_first = x[0].shape[0] // tm if pair else n_i
    in_specs, args = _row_specs(x, tm, n_first)
    n_x = len(args)
    if add:
        in_specs += [pl.BlockSpec((tm, d), lambda i, j: (i, 0)),
                     pl.BlockSpec((tm, d), lambda i, j: (i + n_i, 0)),
                     pl.BlockSpec((tm, LANES), lambda i, j: (i, 0))]
        args += [moe[0], moe[0], moe[1]]
    in_specs += [pl.BlockSpec((1, d), lambda i, j: (0, 0)),
                 pl.BlockSpec((d, tn), lambda i, j: (0, j))]
    args += [g.reshape(1, d), w]
    if rope_cols:
        seq, pos0, n_new = rope
        assert pair and seq % tm == 0 and tm % n_new == 0
        period = seq // tm
        pos = jnp.concatenate([jnp.arange(seq), pos0 + jnp.arange(tm) % n_new])
        in_specs += [pl.BlockSpec((tm, LANES), lambda i, j: (jnp.where(i < n_first, i % period, period), 0))] * 3
        args += list(_rope_tables(pos))
    out_shape = [jax.ShapeDtypeStruct((t, n), out_dtype)]
    out_specs = [pl.BlockSpec((tm, tn), lambda i, j: (i, j))]
    if emit_x:
        out_shape.append(jax.ShapeDtypeStruct((t, d), F32))
        out_specs.append(pl.BlockSpec((tm, d), lambda i, j: (i, 0)))
    res = pl.pallas_call(
        functools.partial(_norm_matmul_kernel, n_x=n_x, n_first=n_first,
                          add=add, emit_x=emit_x, rope_cols=rope_cols, tn=tn),
        grid=(n_i, n // tn),
        in_specs=in_specs,
        out_specs=out_specs,
        out_shape=out_shape,
        scratch_shapes=[pltpu.VMEM((tm, d), BF16)],
        compiler_params=_cparams(("parallel", "arbitrary")),
    )(*args)
    return res if emit_x else res[0]


def _resid_matmul_kernel(*refs, n_a, n_r, n_first):
    a_refs, w_ref, r_refs, o_ref = refs[:n_a], refs[n_a], refs[n_a + 1:n_a + 1 + n_r], refs[-1]
    o_ref[...] = _rows(r_refs, n_first) + _dot(_rows(a_refs, n_first), w_ref[...])


def _resid_matmul(a, w, resid):
    first = a[0] if isinstance(a, tuple) else resid[0] if isinstance(resid, tuple) else None
    t = sum(p.shape[0] for p in a) if isinstance(a, tuple) else a.shape[0]
    n = w.shape[1]
    tm = _pick(t if first is None else math.gcd(first.shape[0], t - first.shape[0]), (512, 256))
    n_first = t // tm if first is None else first.shape[0] // tm
    a_specs, a_args = _row_specs(a, tm, n_first)
    r_specs, r_args = _row_specs(resid, tm, n_first)
    return pl.pallas_call(
        functools.partial(_resid_matmul_kernel, n_a=len(a_args), n_r=len(r_args), n_first=n_first),
        grid=(t // tm,),
        in_specs=a_specs + [pl.BlockSpec(w.shape, lambda i: (0, 0))] + r_specs,
        out_specs=pl.BlockSpec((tm, n), lambda i: (i, 0)),
        out_shape=jax.ShapeDtypeStruct((t, n), F32),
        compiler_params=_cparams(("parallel",)),
    )(*a_args, w, *r_args)


def _final_norm_kernel(x_ref, ya_ref, yb_ref, info_ref, g_ref, o_ref):
    xv = _moe_combine(x_ref[...], ya_ref, yb_ref, info_ref)
    ms = jnp.mean(xv * xv, axis=-1, keepdims=True)
    o_ref[...] = xv * lax.rsqrt(ms + RMS_EPS) * g_ref[...]


def _final_norm(x, y2, info, g, row0, n_rows):
    t, d = x.shape
    tm = _pick(n_rows, (256, 128, 8))
    assert row0 % tm == 0 and t % tm == 0
    b0, bt = row0 // tm, t // tm
    return pl.pallas_call(
        _final_norm_kernel,
        grid=(n_rows // tm,),
        in_specs=[pl.BlockSpec((tm, d), lambda i: (i + b0, 0)),
                  pl.BlockSpec((tm, d), lambda i: (i + b0, 0)),
                  pl.BlockSpec((tm, d), lambda i: (i + b0 + bt, 0)),
                  pl.BlockSpec((tm, LANES), lambda i: (i + b0, 0)),
                  pl.BlockSpec((1, d), lambda i: (0, 0))],
        out_specs=pl.BlockSpec((tm, d), lambda i: (i, 0)),
        out_shape=jax.ShapeDtypeStruct((n_rows, d), F32),
        compiler_params=_cparams(("parallel",)),
    )(x, y2, y2, info, g.reshape(1, d))


def _swa_prompt_kernel(sink_ref, q_ref, kp_ref, kc_ref, vp_ref, vc_ref, o_ref):
    n = pl.program_id(1)
    blk = WINDOW
    qi = lax.broadcasted_iota(jnp.int32, (blk, 2 * blk), 0)
    sj = lax.broadcasted_iota(jnp.int32, (blk, 2 * blk), 1)
    rel = qi + blk - sj
    first_key = jnp.where(n > 0, 0, blk)
    valid = (rel >= 0) & (rel < WINDOW) & (sj >= first_key)
    k = jnp.concatenate([kp_ref[...], kc_ref[...]], axis=0).astype(BF16)
    v = jnp.concatenate([vp_ref[...], vc_ref[...]], axis=0).astype(BF16)
    hd = SWA_HEAD_DIM
    for kh in range(SWA_KV_HEADS):
        k_h = k[:, kh * hd:(kh + 1) * hd]
        v_h = v[:, kh * hd:(kh + 1) * hd]
        for g2 in range(SWA_GROUP // 2):
            pair = []
            for gg in range(2):
                h = kh * SWA_GROUP + g2 * 2 + gg
                q_h = q_ref[:, h * hd:(h + 1) * hd].astype(BF16)
                s = _dot_t(q_h, k_h) * SWA_SCALE
                s = jnp.where(valid, s, NEG_INF)
                sink = sink_ref[h]
                m = jnp.maximum(jnp.max(s, axis=-1, keepdims=True), sink)
                p = jnp.exp(s - m)
                denom = jnp.sum(p, axis=-1, keepdims=True) + jnp.exp(sink - m)
                pair.append(_dot(p.astype(BF16), v_h) / denom)
            h0 = kh * SWA_GROUP + g2 * 2
            o_ref[:, h0 * hd:(h0 + 2) * hd] = jnp.concatenate(pair, axis=1).astype(o_ref.dtype)


def _swa_prompt(qkv, sinks, batch, seq):
    nq = SWA_HEADS * SWA_HEAD_DIM
    nk = SWA_KV_HEADS * SWA_HEAD_DIM
    nb = seq // WINDOW
    kcol, vcol = nq // nk, nq // nk + 1

    def prev(b, n):
        return b * nb + jnp.maximum(n - 1, 0)

    return pl.pallas_call(
        _swa_prompt_kernel,
        grid=(batch, nb),
        in_specs=[pl.BlockSpec(memory_space=pltpu.SMEM),
                  pl.BlockSpec((WINDOW, nq), lambda b, n: (b * nb + n, 0)),
                  pl.BlockSpec((WINDOW, nk), lambda b, n: (prev(b, n), kcol)),
                  pl.BlockSpec((WINDOW, nk), lambda b, n: (b * nb + n, kcol)),
                  pl.BlockSpec((WINDOW, nk), lambda b, n: (prev(b, n), vcol)),
                  pl.BlockSpec((WINDOW, nk), lambda b, n: (b * nb + n, vcol))],
        out_specs=pl.BlockSpec((WINDOW, nq), lambda b, n: (b * nb + n, 0)),
        out_shape=jax.ShapeDtypeStruct((batch * seq, nq), BF16),
        compiler_params=_cparams(("parallel", "arbitrary")),
    )(sinks, qkv, qkv, qkv, qkv, qkv)


def _swa_sample_kernel(q_ref, kn_ref, vn_ref, ck_ref, cv_ref, sink_ref, o_ref, nk_ref, nv_ref,
                       kall, vall, *, n_new, seqs):
    w = WINDOW
    rows = q_ref.shape[1]
    zeros = jnp.zeros((w, kall.shape[1]), F32)
    kall[w:2 * w, :] = zeros
    vall[w:2 * w, :] = zeros
    r = lax.broadcasted_iota(jnp.int32, (rows, 2 * w), 0)
    j = lax.broadcasted_iota(jnp.int32, (rows, 2 * w), 1)
    qn = r % n_new
    valid = ((j < w) & (j > qn)) | ((j >= w) & (j - w <= qn))
    sink = sink_ref[:, 0:1]

    def body(s, carry):
        kall[0:w, :] = ck_ref[s]
        vall[0:w, :] = cv_ref[s]
        kall[w:w + n_new, :] = kn_ref[s]
        vall[w:w + n_new, :] = vn_ref[s]
        nk_ref[s] = kall[n_new:n_new + w, :]
        nv_ref[s] = vall[n_new:n_new + w, :]
        lg = _dot_t(q_ref[s], kall[...].astype(BF16)) * SWA_SCALE
        lg = jnp.where(valid, lg, NEG_INF)
        m = jnp.maximum(jnp.max(lg, axis=-1, keepdims=True), sink)
        p = jnp.exp(lg - m)
        denom = jnp.sum(p, axis=-1, keepdims=True) + jnp.exp(sink - m)
        o_ref[s] = (_dot(p.astype(BF16), vall[...].astype(BF16)) / denom).astype(o_ref.dtype)
        return carry

    lax.fori_loop(0, seqs, body, 0)


def _swa_sample(q, k_new, v_new, cache_k, cache_v, sink_rows):
    b, rows, _ = q.shape
    n_new = k_new.shape[1]
    w, kw = cache_k.shape[1], cache_k.shape[2]
    assert w == WINDOW
    seqs = _pick(b, (8, 4, 2, 1))
    return pl.pallas_call(
        functools.partial(_swa_sample_kernel, n_new=n_new, seqs=seqs),
        grid=(b // seqs,),
        in_specs=[pl.BlockSpec((seqs, rows, kw), lambda i: (i, 0, 0)),
                  pl.BlockSpec((seqs, n_new, kw), lambda i: (i, 0, 0)),
                  pl.BlockSpec((seqs, n_new, kw), lambda i: (i, 0, 0)),
                  pl.BlockSpec((seqs, w, kw), lambda i: (i, 0, 0)),
                  pl.BlockSpec((seqs, w, kw), lambda i: (i, 0, 0)),
                  pl.BlockSpec((rows, LANES), lambda i: (0, 0))],
        out_specs=[pl.BlockSpec((seqs, rows, kw), lambda i: (i, 0, 0)),
                   pl.BlockSpec((seqs, w, kw), lambda i: (i, 0, 0)),
                   pl.BlockSpec((seqs, w, kw), lambda i: (i, 0, 0))],
        out_shape=[jax.ShapeDtypeStruct((b, rows, kw), BF16),
                   jax.ShapeDtypeStruct((b, w, kw), F32),
                   jax.ShapeDtypeStruct((b, w, kw), F32)],
        scratch_shapes=[pltpu.VMEM((2 * w, kw), F32), pltpu.VMEM((2 * w, kw), F32)],
        compiler_params=_cparams(("parallel",)),
    )(q, k_new, v_new, cache_k, cache_v, sink_rows)


def _mem_heads(q, k_of, v_of):
    hd = MEM_HEAD_DIM
    outs = []
    for h in range(MEM_HEADS):
        s = _dot_t(q[:, h * hd:(h + 1) * hd], k_of(h)) * MEM_SCALE
        m = jnp.max(s, axis=-1, keepdims=True)
        p = jnp.exp(s - m)
        denom = jnp.sum(p, axis=-1, keepdims=True)
        outs.append(_dot(p.astype(BF16), v_of(h)) / denom)
    return jnp.concatenate(outs, axis=1)


def _mem_prompt_kernel(q_ref, k_ref, v_ref, o_ref):
    hd = MEM_HEAD_DIM
    k = k_ref[...].astype(BF16)
    v = v_ref[...].astype(BF16)
    o_ref[...] = _mem_heads(q_ref[...], lambda h: k[:, h * hd:(h + 1) * hd],
                            lambda h: v[:, h * hd:(h + 1) * hd]).astype(o_ref.dtype)


def _mem_prompt(q, kv, batch, seq):
    width = MEM_HEADS * MEM_HEAD_DIM
    m = kv.shape[0] // batch
    tq = _pick(seq, (512, 256, 128))
    nq = seq // tq
    return pl.pallas_call(
        _mem_prompt_kernel,
        grid=(batch, nq),
        in_specs=[pl.BlockSpec((tq, width), lambda b, i: (b * nq + i, 0)),
                  pl.BlockSpec((m, width), lambda b, i: (b, 0)),
                  pl.BlockSpec((m, width), lambda b, i: (b, 1))],
        out_specs=pl.BlockSpec((tq, width), lambda b, i: (b * nq + i, 0)),
        out_shape=jax.ShapeDtypeStruct((batch * seq, width), BF16),
        compiler_params=_cparams(("parallel", "arbitrary")),
    )(q, kv, kv)


def _mem_sample_kernel(q_ref, k_ref, v_ref, o_ref, *, seqs):
    nh = MEM_HEADS
    qrows, cols = q_ref.shape[1], k_ref.shape[1]
    row_h = lax.broadcasted_iota(jnp.int32, (qrows, cols), 0) // (qrows // nh)
    col_h = lax.broadcasted_iota(jnp.int32, (qrows, cols), 1) % nh
    own = row_h == col_h

    def body(s, carry):
        lg = _dot_t(q_ref[s], k_ref[s].astype(BF16)) * MEM_SCALE
        lg = jnp.where(own, lg, NEG_INF)
        mx = jnp.max(lg, axis=-1, keepdims=True)
        p = jnp.exp(lg - mx)
        denom = jnp.sum(p, axis=-1, keepdims=True)
        o_ref[s] = (_dot(p.astype(BF16), v_ref[s].astype(BF16)) / denom).astype(o_ref.dtype)
        return carry

    lax.fori_loop(0, seqs, body, 0)


def _mem_sample(q, k, v, layer):
    b, qrows, hd = q.shape
    mh = k.shape[1]
    seqs = _pick(b, (8, 4, 2, 1))
    nb = b // seqs
    return pl.pallas_call(
        functools.partial(_mem_sample_kernel, seqs=seqs),
        grid=(nb,),
        in_specs=[pl.BlockSpec((seqs, qrows, hd), lambda i: (i, 0, 0)),
                  pl.BlockSpec((seqs, mh, hd), lambda i: (layer * nb + i, 0, 0)),
                  pl.BlockSpec((seqs, mh, hd), lambda i: (layer * nb + i, 0, 0))],
        out_specs=pl.BlockSpec((seqs, qrows, hd), lambda i: (i, 0, 0)),
        out_shape=jax.ShapeDtypeStruct((b, qrows, hd), BF16),
        compiler_params=_cparams(("parallel",)),
    )(q, k, v)


def _sb_local(z, mask, tri2):
    sp = jnp.maximum(z, 0.0) + jnp.log(1.0 + jnp.exp(-jnp.abs(z)))
    spm = sp if mask is None else jnp.where(mask, sp, 0.0)
    cum = _dot(spm.astype(BF16), tri2)
    kb = z.shape[1]
    a = z - sp - cum[:, :kb]
    if mask is not None:
        a = jnp.where(mask, a, NEG_INF)
    return a, cum[:, kb:]


def _tri2(kb):
    r = lax.broadcasted_iota(jnp.int32, (kb, 2 * kb), 0)
    cidx = lax.broadcasted_iota(jnp.int32, (kb, 2 * kb), 1)
    return jnp.where((cidx >= kb) | (r > cidx), 1.0, 0.0).astype(BF16)


def _sb_prompt_kernel(bias_ref, q_ref, k_ref, v_ref, o_ref, kb_scr, vb_scr, c_scr, acc_scr):
    kh = pl.program_id(1)
    n = pl.program_id(2)
    blk = SB_BLOCK
    hd = SB_HEAD_DIM

    @pl.when(n == 0)
    def _():
        kb_scr[...] = k_ref[...].astype(BF16)
        vb_scr[...] = v_ref[...].astype(BF16)

    q = q_ref[...]
    qs = (jnp.concatenate([q[:, g * hd:(g + 1) * hd] for g in range(SB_GROUP)], axis=0) * SB_SCALE).astype(BF16)
    bias = jnp.concatenate(
        [jnp.full((blk, LANES), bias_ref[kh * SB_GROUP + g], F32) for g in range(SB_GROUP)], axis=0)
    tri2 = _tri2(blk)
    rows = SB_GROUP * blk
    qi = lax.broadcasted_iota(jnp.int32, (rows, blk), 0) % blk
    sj = lax.broadcasted_iota(jnp.int32, (rows, blk), 1)
    diag_mask = sj < qi

    def local(kblk, mask):
        start = pl.multiple_of(kblk * blk, blk)
        z = _dot_t(qs, kb_scr[pl.ds(start, blk), :]) + bias
        return _sb_local(z, mask, tri2)

    def accumulate(kblk, a, tot):
        start = pl.multiple_of(kblk * blk, blk)
        wgt = jnp.exp(a + c_scr[...])
        acc_scr[...] += _dot(wgt.astype(BF16), vb_scr[pl.ds(start, blk), :])
        c_scr[...] -= tot

    c_scr[...] = jnp.zeros_like(c_scr)
    acc_scr[...] = jnp.zeros_like(acc_scr)
    accumulate(n, *local(n, diag_mask))

    def body(i, carry):
        k1 = n - 1 - 2 * i
        l1 = local(k1, None)
        l2 = local(k1 - 1, None)
        accumulate(k1, *l1)
        accumulate(k1 - 1, *l2)
        return carry

    lax.fori_loop(0, n // 2, body, 0)

    @pl.when(n % 2 == 1)
    def _():
        accumulate(0, *local(0, None))

    acc = acc_scr[...]
    o_ref[...] = jnp.concatenate([acc[g * blk:(g + 1) * blk] for g in range(SB_GROUP)],
                                 axis=1).astype(o_ref.dtype)


def _sb_prompt(qkv, bias, batch, seq):
    hd = SB_HEAD_DIM
    nb = seq // SB_BLOCK
    gw = SB_GROUP * hd
    kcol0 = SB_HEADS
    vcol0 = SB_HEADS + SB_KV_HEADS
    return pl.pallas_call(
        _sb_prompt_kernel,
        grid=(batch, SB_KV_HEADS, nb),
        in_specs=[pl.BlockSpec(memory_space=pltpu.SMEM),
                  pl.BlockSpec((SB_BLOCK, gw), lambda b, kh, n: (b * nb + n, kh)),
                  pl.BlockSpec((seq, hd), lambda b, kh, n: (b, kcol0 + kh)),
                  pl.BlockSpec((seq, hd), lambda b, kh, n: (b, vcol0 + kh))],
        out_specs=pl.BlockSpec((SB_BLOCK, gw), lambda b, kh, n: (b * nb + n, kh)),
        out_shape=jax.ShapeDtypeStruct((batch * seq, SB_HEADS * hd), BF16),
        scratch_shapes=[pltpu.VMEM((seq, hd), BF16), pltpu.VMEM((seq, hd), BF16),
                        pltpu.VMEM((SB_GROUP * SB_BLOCK, LANES), F32),
                        pltpu.VMEM((SB_GROUP * SB_BLOCK, hd), F32)],
        compiler_params=_cparams(("parallel", "parallel", "arbitrary")),
    )(bias, qkv, qkv, qkv)


def _sb_sample_kernel(pt_ref, q_ref, kn_ref, vn_ref, bias_ref, kpool, vpool, o_ref,
                      kbuf, vbuf, kpad, vpad, c_scr, acc_scr, ksem, vsem, *, n_new, pps, n_pages, page0):
    seq = pl.program_id(0)
    step = pl.program_id(1)
    steps = pl.num_programs(1)
    gstep = seq * steps + step
    slot = gstep % 2
    nkv = SB_KV_HEADS
    page = SB_BLOCK
    rows = SB_GROUP * n_new
    qr = nkv * rows
    tri2 = _tri2(page)
    q = q_ref[0]
    bias = bias_ref[...]

    def start_fetch(sq, chunk, sl):
        for i in range(pps):
            pg = page0 + pt_ref[sq * n_pages + (n_pages - 1 - (chunk * pps + i))]
            pltpu.make_async_copy(kpool.at[pg], kbuf.at[sl, i], ksem.at[sl]).start()
            pltpu.make_async_copy(vpool.at[pg], vbuf.at[sl, i], vsem.at[sl]).start()

    def wait_fetch(sl):
        pltpu.make_async_copy(kpool.at[pl.ds(0, pps)], kbuf.at[sl], ksem.at[sl]).wait()
        pltpu.make_async_copy(vpool.at[pl.ds(0, pps)], vbuf.at[sl], vsem.at[sl]).wait()

    @pl.when(gstep == 0)
    def _():
        start_fetch(0, 0, 0)

    @pl.when(gstep + 1 < pl.num_programs(0) * steps)
    def _():
        wrap = step + 1 == steps
        start_fetch(jnp.where(wrap, seq + 1, seq), jnp.where(wrap, 0, step + 1), 1 - slot)

    def head(ref, kh):
        return ref[pl.ds(kh, page, stride=nkv), :].astype(BF16)

    def logits(page_refs):
        z = jnp.concatenate(
            [_dot_t(q[kh * rows:(kh + 1) * rows], head(k_ref, kh))
             for k_ref, _ in page_refs for kh in range(nkv)], axis=0)
        return z + jnp.concatenate([bias] * len(page_refs), axis=0)

    def weights(z, mask):
        a, tot = _sb_local(z, mask, tri2)
        return jnp.exp(a).astype(BF16), tot

    def outputs(page_refs, wb, tot):
        outs = [jnp.concatenate(
            [_dot(wb[p * qr + kh * rows:p * qr + (kh + 1) * rows], head(v_ref, kh))
             for kh in range(nkv)], axis=0) for p, (_, v_ref) in enumerate(page_refs)]
        return outs, [tot[p * qr:(p + 1) * qr] for p in range(len(page_refs))]

    def local(groups, mask):
        zs, ws, parts = {}, {}, []
        for t in range(len(groups) + 2):
            if t < len(groups):
                zs[t] = logits(groups[t])
            if 0 <= t - 1 < len(groups):
                ws[t - 1] = weights(zs.pop(t - 1), mask)
            if 0 <= t - 2 < len(groups):
                parts.append(outputs(groups[t - 2], *ws.pop(t - 2)))
        return parts

    def combine(parts):
        off = c_scr[...]
        acc = acc_scr[...]
        for outs, tots in parts:
            for o_p, t_p in zip(outs, tots):
                acc = acc + jnp.exp(off) * o_p
                off = off - t_p
        acc_scr[...] = acc
        c_scr[...] = off

    @pl.when(step == 0)
    def _():
        c_scr[...] = jnp.zeros_like(c_scr)
        acc_scr[...] = jnp.zeros_like(acc_scr)
        kpad[...] = jnp.zeros_like(kpad)
        vpad[...] = jnp.zeros_like(vpad)
        kpad[0:nkv * n_new, :] = kn_ref[0]
        vpad[0:nkv * n_new, :] = vn_ref[0]
        r = lax.broadcasted_iota(jnp.int32, (qr, page), 0)
        j = lax.broadcasted_iota(jnp.int32, (qr, page), 1)
        combine(local([[(kpad, vpad)]], j < (r % n_new)))

    wait_fetch(slot)
    pages = [(kbuf.at[slot, i], vbuf.at[slot, i]) for i in range(pps)]
    chain = min(SB_PAGE_CHAIN, pps)
    combine(local([pages[c:c + chain] for c in range(0, pps, chain)], None))

    @pl.when(step == steps - 1)
    def _():
        o_ref[0] = acc_scr[...].astype(o_ref.dtype)


def _sb_sample(q, k_new, v_new, k_pool, v_pool, page_table, bias_rows, page0):
    b, qrows, hd = q.shape
    new_rows = k_new.shape[1]
    n_new = new_rows // SB_KV_HEADS
    n_pages = page_table.shape[1]
    prow = k_pool.shape[1]
    assert prow == SB_BLOCK * SB_KV_HEADS
    pps = _pick(n_pages, (SB_PAGES_PER_STEP, 8, 4, 2, 1))
    grid_spec = pltpu.PrefetchScalarGridSpec(
        num_scalar_prefetch=1,
        grid=(b, n_pages // pps),
        in_specs=[pl.BlockSpec((1, qrows, hd), lambda s, c, pt: (s, 0, 0)),
                  pl.BlockSpec((1, new_rows, hd), lambda s, c, pt: (s, 0, 0)),
                  pl.BlockSpec((1, new_rows, hd), lambda s, c, pt: (s, 0, 0)),
                  pl.BlockSpec((qrows, LANES), lambda s, c, pt: (0, 0)),
                  pl.BlockSpec(memory_space=pl.ANY),
                  pl.BlockSpec(memory_space=pl.ANY)],
        out_specs=pl.BlockSpec((1, qrows, hd), lambda s, c, pt: (s, 0, 0)),
        scratch_shapes=[pltpu.VMEM((2, pps, prow, hd), F32), pltpu.VMEM((2, pps, prow, hd), F32),
                        pltpu.VMEM((prow, hd), F32), pltpu.VMEM((prow, hd), F32),
                        pltpu.VMEM((qrows, LANES), F32), pltpu.VMEM((qrows, hd), F32),
                        pltpu.SemaphoreType.DMA((2,)), pltpu.SemaphoreType.DMA((2,))],
    )
    return pl.pallas_call(
        functools.partial(_sb_sample_kernel, n_new=n_new, pps=pps, n_pages=n_pages, page0=page0),
        grid_spec=grid_spec,
        out_shape=jax.ShapeDtypeStruct((b, qrows, hd), BF16),
        compiler_params=_cparams(("arbitrary", "arbitrary")),
    )(page_table.reshape(-1), q, k_new, v_new, bias_rows, k_pool, v_pool)


def _router_kernel(x_ref, g_ref, wh_ref, wl_ref, b_ref, h_ref, info_ref):
    xv = x_ref[...]
    ms = jnp.mean(xv * xv, axis=-1, keepdims=True)
    h = xv * lax.rsqrt(ms + RMS_EPS) * g_ref[...]
    h_ref[...] = h
    h_hi = h.astype(BF16)
    h_lo = (h - h_hi.astype(F32)).astype(BF16)
    logits = _dot(h_hi, wh_ref[...]) + _dot(h_hi, wl_ref[...]) + _dot(h_lo, wh_ref[...]) + b_ref[...]
    lane = lax.broadcasted_iota(jnp.int32, logits.shape, 1)
    big = jnp.int32(1 << 20)
    is_g = (lane >= N_EXPERTS) & (lane < N_EXPERTS + N_GROUPS)
    gl = jnp.where(is_g, logits, -jnp.inf)
    gmax = jnp.max(gl, axis=-1, keepdims=True)
    gidx = jnp.min(jnp.where(gl == gmax, lane - N_EXPERTS, big), axis=-1, keepdims=True)
    p_group = 1.0 / jnp.sum(jnp.exp(gl - gmax), axis=-1, keepdims=True)
    lo = gidx * EXPERTS_PER_GROUP
    in_grp = (lane >= lo) & (lane < lo + EXPERTS_PER_GROUP)
    el = jnp.where(in_grp, logits, -jnp.inf)
    emax = jnp.max(el, axis=-1, keepdims=True)
    pe = jnp.exp(el - emax)
    prob = pe / jnp.sum(pe, axis=-1, keepdims=True)
    prob = jnp.where(in_grp, prob, -1.0)
    p1 = jnp.max(prob, axis=-1, keepdims=True)
    i1 = jnp.min(jnp.where(prob == p1, lane, big), axis=-1, keepdims=True)
    rest = jnp.where(lane == i1, -1.0, prob)
    p2 = jnp.max(rest, axis=-1, keepdims=True)
    i2 = jnp.min(jnp.where(rest == p2, lane, big), axis=-1, keepdims=True)
    tot = p1 + p2
    w1 = p1 / tot * p_group
    w2 = p2 / tot * p_group
    info = jnp.where(lane == 0, i1.astype(F32),
                     jnp.where(lane == 1, i2.astype(F32),
                               jnp.where(lane == 2, w1, jnp.where(lane == 3, w2, 0.0))))
    info_ref[...] = info


def _router(x, g, w_router, b_router, w_group, b_group):
    t, d = x.shape
    pad = LANES - N_EXPERTS - N_GROUPS
    w = jnp.concatenate([w_router, w_group, jnp.zeros((d, pad), F32)], axis=1)
    bias = jnp.concatenate([b_router, b_group, jnp.zeros((pad,), F32)]).reshape(1, LANES)
    w_hi = w.astype(BF16)
    w_lo = (w - w_hi.astype(F32)).astype(BF16)
    tm = _pick(t, (256, 128, 8))
    return pl.pallas_call(
        _router_kernel,
        grid=(t // tm,),
        in_specs=[pl.BlockSpec((tm, d), lambda i: (i, 0)),
                  pl.BlockSpec((1, d), lambda i: (0, 0)),
                  pl.BlockSpec((d, LANES), lambda i: (0, 0)),
                  pl.BlockSpec((d, LANES), lambda i: (0, 0)),
                  pl.BlockSpec((1, LANES), lambda i: (0, 0))],
        out_specs=[pl.BlockSpec((tm, d), lambda i: (i, 0)),
                   pl.BlockSpec((tm, LANES), lambda i: (i, 0))],
        out_shape=[jax.ShapeDtypeStruct((t, d), F32), jax.ShapeDtypeStruct((t, LANES), F32)],
        compiler_params=_cparams(("parallel",)),
    )(x, g.reshape(1, d), w_hi, w_lo, bias)


def _moe_kernel(te_ref, r0_ref, nv_ref, nused_ref, tok_ref, dst_ref, h_hbm, wg_ref, wu_ref, wd_ref, y_hbm,
                xbuf0, xbuf1, ybuf0, ybuf1, gsem, ssem, *, tm, n_pairs):
    del te_ref
    i = pl.program_id(0)
    n_valid = nv_ref[i]
    n_used = nused_ref[0]
    xbuf = (xbuf0, xbuf1)
    ybuf = (ybuf0, ybuf1)

    def start_gather(tile, sl):
        base = r0_ref[tile]
        for r in range(tm):
            pltpu.make_async_copy(h_hbm.at[pl.ds(tok_ref[base + r], 1)],
                                  xbuf[sl].at[pl.ds(r, 1)], gsem.at[sl]).start()

    def wait_gather(sl):
        pltpu.make_async_copy(h_hbm.at[pl.ds(0, tm)], xbuf[sl], gsem.at[sl]).wait()

    def start_scatter(tile, sl, valid):
        base = r0_ref[tile]
        spare = n_pairs + sl * tm
        for r in range(tm):
            dst = jnp.where(r < valid, dst_ref[base + r], spare + r)
            pltpu.make_async_copy(ybuf[sl].at[pl.ds(r, 1)], y_hbm.at[pl.ds(dst, 1)], ssem.at[sl]).start()

    def wait_scatter(sl):
        pltpu.make_async_copy(ybuf[sl], y_hbm.at[pl.ds(0, tm)], ssem.at[sl]).wait()

    @pl.when(i == 0)
    def _():
        ybuf1[...] = jnp.zeros(ybuf1.shape, ybuf1.dtype)
        pltpu.make_async_copy(ybuf1, y_hbm.at[pl.ds(n_pairs, tm)], ssem.at[0]).start()
        start_gather(0, 0)

    def used_step(sl):
        wait_gather(sl)
        wait_scatter(sl)
        start_gather(i + 1, 1 - sl)
        prev = jnp.maximum(i - 1, 0)
        start_scatter(prev, 1 - sl, jnp.where(i >= 1, nv_ref[prev], 0))
        x = xbuf[sl][...].astype(BF16)
        a = _dot(x, wg_ref[0].astype(BF16))
        u = _dot(x, wu_ref[0].astype(BF16))
        act = a * jax.nn.sigmoid(a) * u
        ybuf[sl][...] = _dot(act.astype(BF16), wd_ref[0].astype(BF16))

    def flush_step(sl):
        last = i - 1
        wait_gather(sl)
        wait_scatter(sl)
        start_scatter(last, 1 - sl, nv_ref[last])
        wait_scatter(1 - sl)

    for sl in range(2):
        pl.when((n_valid > 0) & (i % 2 == sl))(functools.partial(used_step, sl))
        pl.when((i == n_used) & (i % 2 == sl))(functools.partial(flush_step, sl))


def _lookup(table, idx):
    n = table.shape[0]
    hit = idx[:, None] == jnp.arange(n, dtype=jnp.int32)[None, :]
    return jnp.sum(jnp.where(hit, table[None, :], 0), axis=1)


def _moe(h, info, w_gate, w_up, w_down, layer):
    t, d = h.shape
    n_exp, ff = N_EXPERTS, w_gate.shape[-1]
    tm = MOE_TILE
    pairs = 2 * t
    n_tiles = -(-pairs // tm) + n_exp + 1
    eid = jnp.concatenate([info[:, 0], info[:, 1]]).astype(jnp.int32)
    sorted_eid, order = lax.sort_key_val(eid, jnp.arange(pairs, dtype=jnp.int32))
    bounds = jnp.sum((sorted_eid[None, :] < jnp.arange(n_exp + 1, dtype=jnp.int32)[:, None]).astype(jnp.int32),
                     axis=1)
    starts, counts = bounds[:-1], bounds[1:] - bounds[:-1]
    tiles_e = (counts + tm - 1) // tm
    tile_end = jnp.cumsum(tiles_e)
    n_used = tile_end[-1]
    tile = jnp.arange(n_tiles, dtype=jnp.int32)
    tile_e = jnp.minimum(jnp.sum((tile[:, None] >= tile_end[None, :]).astype(jnp.int32), axis=1), n_exp - 1)
    used = tile < n_used
    k_in_e = tile - _lookup(tile_end - tiles_e, tile_e)
    tile_r0 = jnp.where(used, _lookup(starts, tile_e) + k_in_e * tm, 0).astype(jnp.int32)
    tile_valid = jnp.where(used, jnp.clip(_lookup(counts, tile_e) - k_in_e * tm, 0, tm), 0).astype(jnp.int32)
    last_e = jnp.sum(jnp.where(tile == n_used - 1, tile_e, 0))
    tile_e = jnp.where(used, tile_e, last_e).astype(jnp.int32)
    tok_sorted = jnp.pad(order % t, (0, tm))
    dst_sorted = jnp.pad(order, (0, tm))

    wg = w_gate.reshape((-1,) + w_gate.shape[-2:])
    wu = w_up.reshape((-1,) + w_up.shape[-2:])
    wd = w_down.reshape((-1,) + w_down.shape[-2:])
    e0 = layer * n_exp

    def w_map(i, te, *_):
        return (e0 + te[i], 0, 0)

    grid_spec = pltpu.PrefetchScalarGridSpec(
        num_scalar_prefetch=6,
        grid=(n_tiles,),
        in_specs=[pl.BlockSpec(memory_space=pl.ANY),
                  pl.BlockSpec((1, d, ff), w_map),
                  pl.BlockSpec((1, d, ff), w_map),
                  pl.BlockSpec((1, ff, d), w_map)],
        out_specs=pl.BlockSpec(memory_space=pl.ANY),
        scratch_shapes=[pltpu.VMEM((tm, d), F32)] * 4
                       + [pltpu.SemaphoreType.DMA((2,)), pltpu.SemaphoreType.DMA((2,))],
    )
    return pl.pallas_call(
        functools.partial(_moe_kernel, tm=tm, n_pairs=pairs),
        grid_spec=grid_spec,
        out_shape=jax.ShapeDtypeStruct((pairs + 2 * tm, d), F32),
        compiler_params=_cparams(("arbitrary",)),
    )(tile_e, tile_r0, tile_valid, n_used.reshape(1).astype(jnp.int32), tok_sorted, dst_sorted, h, wg, wu, wd)


def _rope_tables(pos):
    half = ROPE_DIM // 2
    inv_freq = ROPE_THETA ** (-2.0 * jnp.arange(half, dtype=F32) / ROPE_DIM)
    ang = pos.astype(F32)[:, None] * inv_freq[None, :]
    cos, sin = jnp.cos(ang), jnp.sin(ang)
    t = pos.shape[0]
    ones = jnp.ones((t, SWA_HEAD_DIM - ROPE_DIM), F32)
    zeros = jnp.zeros((t, half), F32)
    zrest = jnp.zeros((t, SWA_HEAD_DIM - ROPE_DIM), F32)
    c_head = jnp.concatenate([cos, cos, ones], axis=1)
    plus_head = jnp.concatenate([zeros, sin, zrest], axis=1)
    minus_head = jnp.concatenate([-sin, zeros, zrest], axis=1)
    rep = LANES // SWA_HEAD_DIM
    return tuple(jnp.concatenate([a] * rep, axis=1) for a in (c_head, plus_head, minus_head))


def kernel(x_prompt, x_sample, mem_prompt, cache_swa_k, cache_swa_v, cache_sb_k, cache_sb_v, cache_mem_k, cache_mem_v, page_table, norm_mix, w_in_swa, sinks_swa, w_out_swa, w_in_sb, sb_bias, w_out_sb, norm_mem_q, norm_mem_kv, w_mem_q, w_mem_kv, w_mem_o, norm_ffn, w_group, b_group, w_router, b_router, w_gate, w_up, w_down, norm_final):
    b_p, s_p, d = x_prompt.shape
    b_s, n_s, _ = x_sample.shape
    depth = norm_mix.shape[0]
    t_p, t_s = b_p * s_p, b_s * n_s
    t = t_p + t_s
    past_len = page_table.shape[1] * cache_sb_k.shape[2]
    assert s_p >= WINDOW and s_p % WINDOW == 0

    x = (x_prompt.reshape(t_p, d), x_sample.reshape(t_s, d))
    rope = (s_p, past_len, n_s)
    mem_flat = mem_prompt.reshape(-1, d)
    mem_m = mem_prompt.shape[1]
    mem_w = MEM_HEADS * MEM_HEAD_DIM
    sb_pool_k = cache_sb_k.reshape(-1, cache_sb_k.shape[2] * SB_KV_HEADS, SB_HEAD_DIM)
    sb_pool_v = cache_sb_v.reshape(-1, cache_sb_v.shape[2] * SB_KV_HEADS, SB_HEAD_DIM)
    mem_cache_k = cache_mem_k.reshape(-1, cache_mem_k.shape[2] * MEM_HEADS, MEM_HEAD_DIM)
    mem_cache_v = cache_mem_v.reshape(-1, cache_mem_v.shape[2] * MEM_HEADS, MEM_HEAD_DIM)

    swa_kp, swa_vp, swa_ks, swa_vs = [], [], [], []
    sb_kp, sb_vp, sb_ks, sb_vs = [], [], [], []
    mem_kp, mem_vp = [], []
    moe = None
    for i in range(depth):
        j = i // 2
        first = moe is None
        if i % 2 == 0:
            nq = SWA_HEADS * SWA_HEAD_DIM
            nk = SWA_KV_HEADS * SWA_HEAD_DIM
            res = _norm_matmul(x, norm_mix[i], w_in_swa[j].astype(BF16), moe=moe, rope=rope,
                               rope_cols=nq + nk, emit_x=not first)
            qkv, x = (res, x) if first else res
            o_p = _swa_prompt(qkv, sinks_swa[j], b_p, s_p)
            qs = qkv[t_p:, :nq].reshape(b_s, n_s, SWA_KV_HEADS, SWA_GROUP, SWA_HEAD_DIM)
            qs = qs.transpose(0, 2, 3, 1, 4).reshape(b_s, SWA_KV_HEADS, SWA_GROUP * n_s, 1, SWA_HEAD_DIM)
            own = jnp.eye(SWA_KV_HEADS, dtype=F32)[None, :, None, :, None]
            qs = (qs * own).astype(BF16).reshape(b_s, SWA_HEADS * n_s, nk)
            kn = qkv[t_p:, nq:nq + nk].reshape(b_s, n_s, nk)
            vn = qkv[t_p:, nq + nk:].reshape(b_s, n_s, nk)
            sink_rows = jnp.broadcast_to(jnp.repeat(sinks_swa[j], n_s)[:, None], (SWA_HEADS * n_s, LANES))
            o_s, kbuf, vbuf = _swa_sample(qs, kn, vn, cache_swa_k[j].reshape(b_s, -1, nk),
                                          cache_swa_v[j].reshape(b_s, -1, nk), sink_rows)
            o_s = o_s.reshape(b_s, SWA_KV_HEADS, SWA_GROUP, n_s, SWA_KV_HEADS, SWA_HEAD_DIM)
            o_s = jnp.sum(o_s * own[:, :, :, None].astype(BF16), axis=4)
            o_s = o_s.transpose(0, 3, 1, 2, 4).reshape(t_s, nq)
            last = qkv[:t_p].reshape(b_p, s_p, -1)[:, s_p - WINDOW:, nq:]
            swa_kp.append(last[:, :, :nk].reshape(b_p, WINDOW, SWA_KV_HEADS, SWA_HEAD_DIM))
            swa_vp.append(last[:, :, nk:].reshape(b_p, WINDOW, SWA_KV_HEADS, SWA_HEAD_DIM))
            swa_ks.append(kbuf.reshape(b_s, -1, SWA_KV_HEADS, SWA_HEAD_DIM))
            swa_vs.append(vbuf.reshape(b_s, -1, SWA_KV_HEADS, SWA_HEAD_DIM))
            w_out = w_out_swa[j]
        else:
            nq = SB_HEADS * SB_HEAD_DIM
            nk = SB_KV_HEADS * SB_HEAD_DIM
            res = _norm_matmul(x, norm_mix[i], w_in_sb[j].astype(BF16), moe=moe, emit_x=not first)
            qkv, x = (res, x) if first else res
            o_p = _sb_prompt(qkv, sb_bias[j], b_p, s_p)
            qs = qkv[t_p:, :nq].reshape(b_s, n_s, SB_KV_HEADS, SB_GROUP, SB_HEAD_DIM)
            qs = (qs.transpose(0, 2, 3, 1, 4).reshape(b_s, SB_HEADS * n_s, SB_HEAD_DIM) * SB_SCALE).astype(BF16)
            kn = qkv[t_p:, nq:nq + nk].reshape(b_s, n_s * SB_KV_HEADS, SB_HEAD_DIM)
            vn = qkv[t_p:, nq + nk:].reshape(b_s, n_s * SB_KV_HEADS, SB_HEAD_DIM)
            bias_rows = jnp.broadcast_to(jnp.repeat(sb_bias[j], n_s)[:, None], (SB_HEADS * n_s, LANES))
            o_s = _sb_sample(qs, kn, vn, sb_pool_k, sb_pool_v, page_table, bias_rows,
                             j * cache_sb_k.shape[1])
            o_s = o_s.reshape(b_s, SB_KV_HEADS, SB_GROUP, n_s, SB_HEAD_DIM)
            o_s = o_s.transpose(0, 3, 1, 2, 4).reshape(t_s, nq)
            sb_kp.append(qkv[:t_p, nq:nq + nk].reshape(b_p, s_p, SB_KV_HEADS, SB_HEAD_DIM))
            sb_vp.append(qkv[:t_p, nq + nk:].reshape(b_p, s_p, SB_KV_HEADS, SB_HEAD_DIM))
            sb_ks.append(kn.reshape(b_s, n_s, SB_KV_HEADS, SB_HEAD_DIM))
            sb_vs.append(vn.reshape(b_s, n_s, SB_KV_HEADS, SB_HEAD_DIM))
            w_out = w_out_sb[j]
        x = _resid_matmul((o_p, o_s), w_out.astype(BF16), x)

        mkv = _norm_matmul(mem_flat, norm_mem_kv[i], w_mem_kv[i].astype(BF16))
        mem_kp.append(mkv[:, :mem_w].reshape(b_p, mem_m, MEM_HEADS, MEM_HEAD_DIM))
        mem_vp.append(mkv[:, mem_w:].reshape(b_p, mem_m, MEM_HEADS, MEM_HEAD_DIM))
        qm = _norm_matmul(x, norm_mem_q[i], w_mem_q[i].astype(BF16), out_dtype=BF16)
        om_p = _mem_prompt(qm, mkv, b_p, s_p)
        row_pad = 16 - n_s % 16 if n_s % 16 else 0
        qm_s = jnp.pad(qm[t_p:].reshape(b_s, n_s, MEM_HEADS, MEM_HEAD_DIM), ((0, 0), (0, row_pad), (0, 0), (0, 0)))
        qm_s = qm_s.transpose(0, 2, 1, 3).reshape(b_s, MEM_HEADS * (n_s + row_pad), MEM_HEAD_DIM)
        om_s = _mem_sample(qm_s, mem_cache_k, mem_cache_v, i)
        om_s = om_s.reshape(b_s, MEM_HEADS, n_s + row_pad, MEM_HEAD_DIM)[:, :, :n_s].transpose(0, 2, 1, 3)
        x = _resid_matmul((om_p, om_s.reshape(t_s, mem_w)), w_mem_o[i].astype(BF16), x)

        h, info = _router(x, norm_ffn[i], w_router[i], b_router[i], w_group[i], b_group[i])
        moe = (_moe(h, info, w_gate, w_up, w_down, i), info)

    y_prompt = _final_norm(x, moe[0], moe[1], norm_final, 0, t_p).reshape(b_p, s_p, d)
    y_sample = _final_norm(x, moe[0], moe[1], norm_final, t_p, t_s).reshape(b_s, n_s, d)
    return (y_prompt, y_sample,
            jnp.stack(swa_kp), jnp.stack(swa_vp), jnp.stack(swa_ks), jnp.stack(swa_vs),
            jnp.stack(sb_kp), jnp.stack(sb_vp), jnp.stack(sb_ks), jnp.stack(sb_vs),
            jnp.stack(mem_kp), jnp.stack(mem_vp))
```

```python
import functools
import math

import jax
import jax.numpy as jnp
from jax import lax
from jax.experimental import pallas as pl
from jax.experimental.pallas import tpu as pltpu

F32 = jnp.float32
BF16 = jnp.bfloat16

SWA_HEADS, SWA_KV_HEADS, SWA_HEAD_DIM = 32, 4, 64
SWA_GROUP = SWA_HEADS // SWA_KV_HEADS
WINDOW = 128
ROPE_THETA = 500000.0
ROPE_DIM = SWA_HEAD_DIM // 4
SWA_SCALE = SWA_HEAD_DIM ** -0.5
SB_HEADS, SB_KV_HEADS, SB_HEAD_DIM = 16, 4, 128
SB_GROUP = SB_HEADS // SB_KV_HEADS
SB_BLOCK = 128
SB_SCALE = SB_HEAD_DIM ** -0.5
MEM_HEADS, MEM_HEAD_DIM = 4, 128
MEM_SCALE = MEM_HEAD_DIM ** -0.5
N_GROUPS, EXPERTS_PER_GROUP = 4, 8
N_EXPERTS = N_GROUPS * EXPERTS_PER_GROUP
RMS_EPS = 1e-6
NEG_INF = -1e30

LANES = 128
VMEM_LIMIT = 56 * 1024 * 1024
MOE_TILE = 256
SB_PAGES_PER_STEP = 16
SB_PAGE_CHAIN = 4


def _cparams(sem):
    return pltpu.CompilerParams(dimension_semantics=sem, vmem_limit_bytes=VMEM_LIMIT)


def _pick(n, prefs):
    for p in prefs:
        if n % p == 0:
            return p
    return n


def _dot_t(a, b):
    return lax.dot_general(a, b, (((1,), (1,)), ((), ())), preferred_element_type=F32)


def _dot(a, b):
    return jnp.dot(a, b, preferred_element_type=F32)


def _rows(refs, n_first):
    if len(refs) == 1:
        return refs[0][...]
    return jnp.where(pl.program_id(0) < n_first, refs[0][...], refs[1][...])


def _row_specs(srcs, tm, n_first):
    if not isinstance(srcs, tuple):
        return [pl.BlockSpec((tm, srcs.shape[1]), lambda i, *_: (i, 0))], [srcs]
    width = srcs[0].shape[1]
    return ([pl.BlockSpec((tm, width), lambda i, *_: (jnp.minimum(i, n_first - 1), 0)),
             pl.BlockSpec((tm, width), lambda i, *_: (jnp.maximum(i - n_first, 0), 0))], list(srcs))


def _moe_combine(xv, ya_ref, yb_ref, info_ref):
    info = info_ref[...]
    return xv + ya_ref[...] * info[:, 2:3] + yb_ref[...] * info[:, 3:4]


def _norm_matmul_kernel(*refs, n_x, n_first, add, emit_x, rope_cols, tn):
    it = iter(refs)
    x_refs = [next(it) for _ in range(n_x)]
    ya_ref = next(it) if add else None
    yb_ref = next(it) if add else None
    info_ref = next(it) if add else None
    g_ref = next(it)
    w_ref = next(it)
    if rope_cols:
        cos_ref, sp_ref, sm_ref = next(it), next(it), next(it)
    o_ref = next(it)
    xo_ref = next(it) if emit_x else None
    h_scr = next(it)
    j = pl.program_id(1)

    @pl.when(j == 0)
    def _():
        xv = _rows(x_refs, n_first)
        if add:
            xv = _moe_combine(xv, ya_ref, yb_ref, info_ref)
        if emit_x:
            xo_ref[...] = xv
        ms = jnp.mean(xv * xv, axis=-1, keepdims=True)
        h_scr[...] = (xv * lax.rsqrt(ms + RMS_EPS) * g_ref[...]).astype(BF16)

    acc = _dot(h_scr[...], w_ref[...])
    if not rope_cols:
        o_ref[...] = acc.astype(o_ref.dtype)
        return

    @pl.when(j * tn < rope_cols)
    def _():
        reps = tn // LANES
        cos = jnp.concatenate([cos_ref[...]] * reps, axis=1)
        s_plus = jnp.concatenate([sp_ref[...]] * reps, axis=1)
        s_minus = jnp.concatenate([sm_ref[...]] * reps, axis=1)
        half = ROPE_DIM // 2
        roped = (acc * cos + pltpu.roll(acc, half, axis=1) * s_plus
                 + pltpu.roll(acc, tn - half, axis=1) * s_minus)
        col = j * tn + lax.broadcasted_iota(jnp.int32, acc.shape, 1)
        o_ref[...] = jnp.where(col < rope_cols, roped, acc).astype(o_ref.dtype)

    @pl.when(j * tn >= rope_cols)
    def _():
        o_ref[...] = acc.astype(o_ref.dtype)


def _norm_matmul(x, g, w, *, moe=None, rope=None, rope_cols=0, emit_x=False,
                 out_dtype=F32):
    pair = isinstance(x, tuple)
    t = x[0].shape[0] + x[1].shape[0] if pair else x.shape[0]
    d = x[0].shape[1] if pair else x.shape[1]
    n = w.shape[1]
    add = moe is not None
    tm = _pick(math.gcd(x[0].shape[0], x[1].shape[0]) if pair else t, (512, 256))
    tn = _pick(n, (1024, 512, 256, 128) if add else (1280, 1024, 512, 256, 128))
    n_i = t // tm
    n_first = x[0].shape[0] // tm if pair else n_i
    in_specs, args = _row_specs(x, tm, n_first)
    n_x = len(args)
    if add:
        in_specs += [pl.BlockSpec((tm, d), lambda i, j: (i, 0)),
                     pl.BlockSpec((tm, d), lambda i, j: (i + n_i, 0)),
                     pl.BlockSpec((tm, LANES), lambda i, j: (i, 0))]
        args += [moe[0], moe[0], moe[1]]
    in_specs += [pl.BlockSpec((1, d), lambda i, j: (0, 0)),
                 pl.BlockSpec((d, tn), lambda i, j: (0, j))]
    args += [g.reshape(1, d), w]
    if rope_cols:
        seq, pos0, n_new = rope
        assert pair and seq % tm == 0 and tm % n_new == 0
        period = seq // tm
        pos = jnp.concatenate([jnp.arange(seq), pos0 + jnp.arange(tm) % n_new])
        in_specs += [pl.BlockSpec((tm, LANES), lambda i, j: (jnp.where(i < n_first, i % period, period), 0))] * 3
        args += list(_rope_tables(pos))
    out_shape = [jax.ShapeDtypeStruct((t, n), out_dtype)]
    out_specs = [pl.BlockSpec((tm, tn), lambda i, j: (i, j))]
    if emit_x:
        out_shape.append(jax.ShapeDtypeStruct((t, d), F32))
        out_specs.append(pl.BlockSpec((tm, d), lambda i, j: (i, 0)))
    res = pl.pallas_call(
        functools.partial(_norm_matmul_kernel, n_x=n_x, n_first=n_first,
                          add=add, emit_x=emit_x, rope_cols=rope_cols, tn=tn),
        grid=(n_i, n // tn),
        in_specs=in_specs,
        out_specs=out_specs,
        out_shape=out_shape,
        scratch_shapes=[pltpu.VMEM((tm, d), BF16)],
        compiler_params=_cparams(("parallel", "arbitrary")),
    )(*args)
    return res if emit_x else res[0]


def _resid_matmul_kernel(*refs, n_a, n_r, n_first):
    a_refs, w_ref, r_refs, o_ref = refs[:n_a], refs[n_a], refs[n_a + 1:n_a + 1 + n_r], refs[-1]
    o_ref[...] = _rows(r_refs, n_first) + _dot(_rows(a_refs, n_first), w_ref[...])


def _resid_matmul(a, w, resid):
    first = a[0] if isinstance(a, tuple) else resid[0] if isinstance(resid, tuple) else None
    t = sum(p.shape[0] for p in a) if isinstance(a, tuple) else a.shape[0]
    n = w.shape[1]
    tm = _pick(t if first is None else math.gcd(first.shape[0], t - first.shape[0]), (512, 256))
    n_first = t // tm if first is None else first.shape[0] // tm
    a_specs, a_args = _row_specs(a, tm, n_first)
    r_specs, r_args = _row_specs(resid, tm, n_first)
    return pl.pallas_call(
        functools.partial(_resid_matmul_kernel, n_a=len(a_args), n_r=len(r_args), n_first=n_first),
        grid=(t // tm,),
        in_specs=a_specs + [pl.BlockSpec(w.shape, lambda i: (0, 0))] + r_specs,
        out_specs=pl.BlockSpec((tm, n), lambda i: (i, 0)),
        out_shape=jax.ShapeDtypeStruct((t, n), F32),
        compiler_params=_cparams(("parallel",)),
    )(*a_args, w, *r_args)


def _final_norm_kernel(x_ref, ya_ref, yb_ref, info_ref, g_ref, o_ref):
    xv = _moe_combine(x_ref[...], ya_ref, yb_ref, info_ref)
    ms = jnp.mean(xv * xv, axis=-1, keepdims=True)
    o_ref[...] = xv * lax.rsqrt(ms + RMS_EPS) * g_ref[...]


def _final_norm(x, y2, info, g, row0, n_rows):
    t, d = x.shape
    tm = _pick(n_rows, (256, 128, 8))
    assert row0 % tm == 0 and t % tm == 0
    b0, bt = row0 // tm, t // tm
    return pl.pallas_call(
        _final_norm_kernel,
        grid=(n_rows // tm,),
        in_specs=[pl.BlockSpec((tm, d), lambda i: (i + b0, 0)),
                  pl.BlockSpec((tm, d), lambda i: (i + b0, 0)),
                  pl.BlockSpec((tm, d), lambda i: (i + b0 + bt, 0)),
                  pl.BlockSpec((tm, LANES), lambda i: (i + b0, 0)),
                  pl.BlockSpec((1, d), lambda i: (0, 0))],
        out_specs=pl.BlockSpec((tm, d), lambda i: (i, 0)),
        out_shape=jax.ShapeDtypeStruct((n_rows, d), F32),
        compiler_params=_cparams(("parallel",)),
    )(x, y2, y2, info, g.reshape(1, d))


def _swa_prompt_kernel(sink_ref, q_ref, kp_ref, kc_ref, vp_ref, vc_ref, o_ref):
    n = pl.program_id(1)
    blk = WINDOW
    qi = lax.broadcasted_iota(jnp.int32, (blk, 2 * blk), 0)
    sj = lax.broadcasted_iota(jnp.int32, (blk, 2 * blk), 1)
    rel = qi + blk - sj
    first_key = jnp.where(n > 0, 0, blk)
    valid = (rel >= 0) & (rel < WINDOW) & (sj >= first_key)
    k = jnp.concatenate([kp_ref[...], kc_ref[...]], axis=0).astype(BF16)
    v = jnp.concatenate([vp_ref[...], vc_ref[...]], axis=0).astype(BF16)
    hd = SWA_HEAD_DIM
    for kh in range(SWA_KV_HEADS):
        k_h = k[:, kh * hd:(kh + 1) * hd]
        v_h = v[:, kh * hd:(kh + 1) * hd]
        for g2 in range(SWA_GROUP // 2):
            pair = []
            for gg in range(2):
                h = kh * SWA_GROUP + g2 * 2 + gg
                q_h = q_ref[:, h * hd:(h + 1) * hd].astype(BF16)
                s = _dot_t(q_h, k_h) * SWA_SCALE
                s = jnp.where(valid, s, NEG_INF)
                sink = sink_ref[h]
                m = jnp.maximum(jnp.max(s, axis=-1, keepdims=True), sink)
                p = jnp.exp(s - m)
                denom = jnp.sum(p, axis=-1, keepdims=True) + jnp.exp(sink - m)
                pair.append(_dot(p.astype(BF16), v_h) / denom)
            h0 = kh * SWA_GROUP + g2 * 2
            o_ref[:, h0 * hd:(h0 + 2) * hd] = jnp.concatenate(pair, axis=1).astype(o_ref.dtype)


def _swa_prompt(qkv, sinks, batch, seq):
    nq = SWA_HEADS * SWA_HEAD_DIM
    nk = SWA_KV_HEADS * SWA_HEAD_DIM
    nb = seq // WINDOW
    kcol, vcol = nq // nk, nq // nk + 1

    def prev(b, n):
        return b * nb + jnp.maximum(n - 1, 0)

    return pl.pallas_call(
        _swa_prompt_kernel,
        grid=(batch, nb),
        in_specs=[pl.BlockSpec(memory_space=pltpu.SMEM),
                  pl.BlockSpec((WINDOW, nq), lambda b, n: (b * nb + n, 0)),
                  pl.BlockSpec((WINDOW, nk), lambda b, n: (prev(b, n), kcol)),
                  pl.BlockSpec((WINDOW, nk), lambda b, n: (b * nb + n, kcol)),
                  pl.BlockSpec((WINDOW, nk), lambda b, n: (prev(b, n), vcol)),
                  pl.BlockSpec((WINDOW, nk), lambda b, n: (b * nb + n, vcol))],
        out_specs=pl.BlockSpec((WINDOW, nq), lambda b, n: (b * nb + n, 0)),
        out_shape=jax.ShapeDtypeStruct((batch * seq, nq), BF16),
        compiler_params=_cparams(("parallel", "arbitrary")),
    )(sinks, qkv, qkv, qkv, qkv, qkv)


def _swa_sample_kernel(q_ref, kn_ref, vn_ref, ck_ref, cv_ref, sink_ref, o_ref, nk_ref, nv_ref,
                       kall, vall, *, n_new, seqs):
    w = WINDOW
    rows = q_ref.shape[1]
    zeros = jnp.zeros((w, kall.shape[1]), F32)
    kall[w:2 * w, :] = zeros
    vall[w:2 * w, :] = zeros
    r = lax.broadcasted_iota(jnp.int32, (rows, 2 * w), 0)
    j = lax.broadcasted_iota(jnp.int32, (rows, 2 * w), 1)
    qn = r % n_new
    valid = ((j < w) & (j > qn)) | ((j >= w) & (j - w <= qn))
    sink = sink_ref[:, 0:1]

    def body(s, carry):
        kall[0:w, :] = ck_ref[s]
        vall[0:w, :] = cv_ref[s]
        kall[w:w + n_new, :] = kn_ref[s]
        vall[w:w + n_new, :] = vn_ref[s]
        nk_ref[s] = kall[n_new:n_new + w, :]
        nv_ref[s] = vall[n_new:n_new + w, :]
        lg = _dot_t(q_ref[s], kall[...].astype(BF16)) * SWA_SCALE
        lg = jnp.where(valid, lg, NEG_INF)
        m = jnp.maximum(jnp.max(lg, axis=-1, keepdims=True), sink)
        p = jnp.exp(lg - m)
        denom = jnp.sum(p, axis=-1, keepdims=True) + jnp.exp(sink - m)
        o_ref[s] = (_dot(p.astype(BF16), vall[...].astype(BF16)) / denom).astype(o_ref.dtype)
        return carry

    lax.fori_loop(0, seqs, body, 0)


def _swa_sample(q, k_new, v_new, cache_k, cache_v, sink_rows):
    b, rows, _ = q.shape
    n_new = k_new.shape[1]
    w, kw = cache_k.shape[1], cache_k.shape[2]
    assert w == WINDOW
    seqs = _pick(b, (8, 4, 2, 1))
    return pl.pallas_call(
        functools.partial(_swa_sample_kernel, n_new=n_new, seqs=seqs),
        grid=(b // seqs,),
        in_specs=[pl.BlockSpec((seqs, rows, kw), lambda i: (i, 0, 0)),
                  pl.BlockSpec((seqs, n_new, kw), lambda i: (i, 0, 0)),
                  pl.BlockSpec((seqs, n_new, kw), lambda i: (i, 0, 0)),
                  pl.BlockSpec((seqs, w, kw), lambda i: (i, 0, 0)),
                  pl.BlockSpec((seqs, w, kw), lambda i: (i, 0, 0)),
                  pl.BlockSpec((rows, LANES), lambda i: (0, 0))],
        out_specs=[pl.BlockSpec((seqs, rows, kw), lambda i: (i, 0, 0)),
                   pl.BlockSpec((seqs, w, kw), lambda i: (i, 0, 0)),
                   pl.BlockSpec((seqs, w, kw), lambda i: (i, 0, 0))],
        out_shape=[jax.ShapeDtypeStruct((b, rows, kw), BF16),
                   jax.ShapeDtypeStruct((b, w, kw), F32),
                   jax.ShapeDtypeStruct((b, w, kw), F32)],
        scratch_shapes=[pltpu.VMEM((2 * w, kw), F32), pltpu.VMEM((2 * w, kw), F32)],
        compiler_params=_cparams(("parallel",)),
    )(q, k_new, v_new, cache_k, cache_v, sink_rows)


def _mem_heads(q, k_of, v_of):
    hd = MEM_HEAD_DIM
    outs = []
    for h in range(MEM_HEADS):
        s = _dot_t(q[:, h * hd:(h + 1) * hd], k_of(h)) * MEM_SCALE
        m = jnp.max(s, axis=-1, keepdims=True)
        p = jnp.exp(s - m)
        denom = jnp.sum(p, axis=-1, keepdims=True)
        outs.append(_dot(p.astype(BF16), v_of(h)) / denom)
    return jnp.concatenate(outs, axis=1)


def _mem_prompt_kernel(q_ref, k_ref, v_ref, o_ref):
    hd = MEM_HEAD_DIM
    k = k_ref[...].astype(BF16)
    v = v_ref[...].astype(BF16)
    o_ref[...] = _mem_heads(q_ref[...], lambda h: k[:, h * hd:(h + 1) * hd],
                            lambda h: v[:, h * hd:(h + 1) * hd]).astype(o_ref.dtype)


def _mem_prompt(q, kv, batch, seq):
    width = MEM_HEADS * MEM_HEAD_DIM
    m = kv.shape[0] // batch
    tq = _pick(seq, (512, 256, 128))
    nq = seq // tq
    return pl.pallas_call(
        _mem_prompt_kernel,
        grid=(batch, nq),
        in_specs=[pl.BlockSpec((tq, width), lambda b, i: (b * nq + i, 0)),
                  pl.BlockSpec((m, width), lambda b, i: (b, 0)),
                  pl.BlockSpec((m, width), lambda b, i: (b, 1))],
        out_specs=pl.BlockSpec((tq, width), lambda b, i: (b * nq + i, 0)),
        out_shape=jax.ShapeDtypeStruct((batch * seq, width), BF16),
        compiler_params=_cparams(("parallel", "arbitrary")),
    )(q, kv, kv)


def _mem_sample_kernel(q_ref, k_ref, v_ref, o_ref, *, seqs):
    nh = MEM_HEADS
    qrows, cols = q_ref.shape[1], k_ref.shape[1]
    row_h = lax.broadcasted_iota(jnp.int32, (qrows, cols), 0) // (qrows // nh)
    col_h = lax.broadcasted_iota(jnp.int32, (qrows, cols), 1) % nh
    own = row_h == col_h

    def body(s, carry):
        lg = _dot_t(q_ref[s], k_ref[s].astype(BF16)) * MEM_SCALE
        lg = jnp.where(own, lg, NEG_INF)
        mx = jnp.max(lg, axis=-1, keepdims=True)
        p = jnp.exp(lg - mx)
        denom = jnp.sum(p, axis=-1, keepdims=True)
        o_ref[s] = (_dot(p.astype(BF16), v_ref[s].astype(BF16)) / denom).astype(o_ref.dtype)
        return carry

    lax.fori_loop(0, seqs, body, 0)


def _mem_sample(q, k, v, layer):
    b, qrows, hd = q.shape
    mh = k.shape[1]
    seqs = _pick(b, (8, 4, 2, 1))
    nb = b // seqs
    return pl.pallas_call(
        functools.partial(_mem_sample_kernel, seqs=seqs),
        grid=(nb,),
        in_specs=[pl.BlockSpec((seqs, qrows, hd), lambda i: (i, 0, 0)),
                  pl.BlockSpec((seqs, mh, hd), lambda i: (layer * nb + i, 0, 0)),
                  pl.BlockSpec((seqs, mh, hd), lambda i: (layer * nb + i, 0, 0))],
        out_specs=pl.BlockSpec((seqs, qrows, hd), lambda i: (i, 0, 0)),
        out_shape=jax.ShapeDtypeStruct((b, qrows, hd), BF16),
        compiler_params=_cparams(("parallel",)),
    )(q, k, v)


def _sb_local(z, mask, tri2):
    sp = jnp.maximum(z, 0.0) + jnp.log(1.0 + jnp.exp(-jnp.abs(z)))
    spm = sp if mask is None else jnp.where(mask, sp, 0.0)
    cum = _dot(spm.astype(BF16), tri2)
    kb = z.shape[1]
    a = z - sp - cum[:, :kb]
    if mask is not None:
        a = jnp.where(mask, a, NEG_INF)
    return a, cum[:, kb:]


def _tri2(kb):
    r = lax.broadcasted_iota(jnp.int32, (kb, kb + LANES), 0)
    cidx = lax.broadcasted_iota(jnp.int32, (kb, kb + LANES), 1)
    return jnp.where((cidx >= kb) | (r > cidx), 1.0, 0.0).astype(BF16)


def _sb_prompt_kernel(bias_ref, q_ref, k_ref, v_ref, o_ref, kb_scr, vb_scr, c_scr, acc_scr):
    kh = pl.program_id(1)
    n = pl.program_id(2)
    blk = SB_BLOCK
    hd = SB_HEAD_DIM

    @pl.when(n == 0)
    def _():
        kb_scr[...] = k_ref[...].astype(BF16)
        vb_scr[...] = v_ref[...].astype(BF16)

    q = q_ref[...]
    qs = (jnp.concatenate([q[:, g * hd:(g + 1) * hd] for g in range(SB_GROUP)], axis=0) * SB_SCALE).astype(BF16)
    bias = jnp.concatenate(
        [jnp.full((blk, LANES), bias_ref[kh * SB_GROUP + g], F32) for g in range(SB_GROUP)], axis=0)
    tri2 = _tri2(blk)
    rows = SB_GROUP * blk
    qi = lax.broadcasted_iota(jnp.int32, (rows, blk), 0) % blk
    sj = lax.broadcasted_iota(jnp.int32, (rows, blk), 1)
    diag_mask = sj < qi

    def local(kblk, mask):
        start = pl.multiple_of(kblk * blk, blk)
        z = _dot_t(qs, kb_scr[pl.ds(start, blk), :]) + bias
        return _sb_local(z, mask, tri2)

    def accumulate(kblk, a, tot):
        start = pl.multiple_of(kblk * blk, blk)
        wgt = jnp.exp(a + c_scr[...])
        acc_scr[...] += _dot(wgt.astype(BF16), vb_scr[pl.ds(start, blk), :])
        c_scr[...] -= tot

    c_scr[...] = jnp.zeros_like(c_scr)
    acc_scr[...] = jnp.zeros_like(acc_scr)
    accumulate(n, *local(n, diag_mask))

    def wide(first_blk, width):
        start = pl.multiple_of(first_blk * blk, blk)
        z = _dot_t(qs, kb_scr[pl.ds(start, width * blk), :]) + jnp.concatenate([bias] * width, axis=1)
        a, tot = _sb_local(z, None, _tri2(width * blk))
        c = c_scr[...]
        wgt = jnp.exp(a + jnp.concatenate([c] * width, axis=1))
        acc_scr[...] += _dot(wgt.astype(BF16), vb_scr[pl.ds(start, width * blk), :])
        c_scr[...] = c - tot

    def body(i, carry):
        wide(n - 4 - 4 * i, 4)
        return carry

    lax.fori_loop(0, n // 4, body, 0)

    @pl.when(n % 4 >= 2)
    def _():
        wide(n % 4 - 2, 2)

    @pl.when(n % 2 == 1)
    def _():
        accumulate(0, *local(0, None))

    acc = acc_scr[...]
    o_ref[...] = jnp.concatenate([acc[g * blk:(g + 1) * blk] for g in range(SB_GROUP)],
                                 axis=1).astype(o_ref.dtype)


def _sb_prompt(qkv, bias, batch, seq):
    hd = SB_HEAD_DIM
    nb = seq // SB_BLOCK
    gw = SB_GROUP * hd
    kcol0 = SB_HEADS
    vcol0 = SB_HEADS + SB_KV_HEADS
    return pl.pallas_call(
        _sb_prompt_kernel,
        grid=(batch, SB_KV_HEADS, nb),
        in_specs=[pl.BlockSpec(memory_space=pltpu.SMEM),
                  pl.BlockSpec((SB_BLOCK, gw), lambda b, kh, n: (b * nb + n, kh)),
                  pl.BlockSpec((seq, hd), lambda b, kh, n: (b, kcol0 + kh)),
                  pl.BlockSpec((seq, hd), lambda b, kh, n: (b, vcol0 + kh))],
        out_specs=pl.BlockSpec((SB_BLOCK, gw), lambda b, kh, n: (b * nb + n, kh)),
        out_shape=jax.ShapeDtypeStruct((batch * seq, SB_HEADS * hd), BF16),
        scratch_shapes=[pltpu.VMEM((seq, hd), BF16), pltpu.VMEM((seq, hd), BF16),
                        pltpu.VMEM((SB_GROUP * SB_BLOCK, LANES), F32),
                        pltpu.VMEM((SB_GROUP * SB_BLOCK, hd), F32)],
        compiler_params=_cparams(("parallel", "parallel", "arbitrary")),
    )(bias, qkv, qkv, qkv)


def _sb_sample_kernel(pt_ref, q_ref, kn_ref, vn_ref, bias_ref, kpool, vpool, o_ref,
                      kbuf, vbuf, kpad, vpad, c_scr, acc_scr, ksem, vsem, *, n_new, pps, n_pages, page0):
    seq = pl.program_id(0)
    step = pl.program_id(1)
    steps = pl.num_programs(1)
    gstep = seq * steps + step
    slot = gstep % 2
    nkv = SB_KV_HEADS
    page = SB_BLOCK
    rows = SB_GROUP * n_new
    qr = nkv * rows
    tri2 = _tri2(page)
    q = q_ref[0]
    bias = bias_ref[...]

    def start_fetch(sq, chunk, sl):
        for i in range(pps):
            pg = page0 + pt_ref[sq * n_pages + (n_pages - 1 - (chunk * pps + i))]
            pltpu.make_async_copy(kpool.at[pg], kbuf.at[sl, i], ksem.at[sl]).start()
            pltpu.make_async_copy(vpool.at[pg], vbuf.at[sl, i], vsem.at[sl]).start()

    def wait_fetch(sl):
        pltpu.make_async_copy(kpool.at[pl.ds(0, pps)], kbuf.at[sl], ksem.at[sl]).wait()
        pltpu.make_async_copy(vpool.at[pl.ds(0, pps)], vbuf.at[sl], vsem.at[sl]).wait()

    @pl.when(gstep == 0)
    def _():
        start_fetch(0, 0, 0)

    @pl.when(gstep + 1 < pl.num_programs(0) * steps)
    def _():
        wrap = step + 1 == steps
        start_fetch(jnp.where(wrap, seq + 1, seq), jnp.where(wrap, 0, step + 1), 1 - slot)

    def head(ref, kh):
        return ref[pl.ds(kh, page, stride=nkv), :].astype(BF16)

    def logits(page_refs):
        z = jnp.concatenate(
            [_dot_t(q[kh * rows:(kh + 1) * rows], head(k_ref, kh))
             for k_ref, _ in page_refs for kh in range(nkv)], axis=0)
        return z + jnp.concatenate([bias] * len(page_refs), axis=0)

    def weights(z, mask):
        a, tot = _sb_local(z, mask, tri2)
        return jnp.exp(a).astype(BF16), tot

    def outputs(page_refs, wb, tot):
        outs = [jnp.concatenate(
            [_dot(wb[p * qr + kh * rows:p * qr + (kh + 1) * rows], head(v_ref, kh))
             for kh in range(nkv)], axis=0) for p, (_, v_ref) in enumerate(page_refs)]
        return outs, [tot[p * qr:(p + 1) * qr] for p in range(len(page_refs))]

    def local(groups, mask):
        zs, ws, parts = {}, {}, []
        for t in range(len(groups) + 2):
            if t < len(groups):
                zs[t] = logits(groups[t])
            if 0 <= t - 1 < len(groups):
                ws[t - 1] = weights(zs.pop(t - 1), mask)
            if 0 <= t - 2 < len(groups):
                parts.append(outputs(groups[t - 2], *ws.pop(t - 2)))
        return parts

    def combine(parts):
        off = c_scr[...]
        acc = acc_scr[...]
        for outs, tots in parts:
            for o_p, t_p in zip(outs, tots):
                acc = acc + jnp.exp(off) * o_p
                off = off - t_p
        acc_scr[...] = acc
        c_scr[...] = off

    @pl.when(step == 0)
    def _():
        c_scr[...] = jnp.zeros_like(c_scr)
        acc_scr[...] = jnp.zeros_like(acc_scr)
        kpad[...] = jnp.zeros_like(kpad)
        vpad[...] = jnp.zeros_like(vpad)
        kpad[0:nkv * n_new, :] = kn_ref[0]
        vpad[0:nkv * n_new, :] = vn_ref[0]
        r = lax.broadcasted_iota(jnp.int32, (qr, page), 0)
        j = lax.broadcasted_iota(jnp.int32, (qr, page), 1)
        combine(local([[(kpad, vpad)]], j < (r % n_new)))

    wait_fetch(slot)
    pages = [(kbuf.at[slot, i], vbuf.at[slot, i]) for i in range(pps)]
    chain = min(SB_PAGE_CHAIN, pps)
    combine(local([pages[c:c + chain] for c in range(0, pps, chain)], None))

    @pl.when(step == steps - 1)
    def _():
        o_ref[0] = acc_scr[...].astype(o_ref.dtype)


def _sb_sample(q, k_new, v_new, k_pool, v_pool, page_table, bias_rows, page0):
    b, qrows, hd = q.shape
    new_rows = k_new.shape[1]
    n_new = new_rows // SB_KV_HEADS
    n_pages = page_table.shape[1]
    prow = k_pool.shape[1]
    assert prow == SB_BLOCK * SB_KV_HEADS
    pps = _pick(n_pages, (SB_PAGES_PER_STEP, 8, 4, 2, 1))
    grid_spec = pltpu.PrefetchScalarGridSpec(
        num_scalar_prefetch=1,
        grid=(b, n_pages // pps),
        in_specs=[pl.BlockSpec((1, qrows, hd), lambda s, c, pt: (s, 0, 0)),
                  pl.BlockSpec((1, new_rows, hd), lambda s, c, pt: (s, 0, 0)),
                  pl.BlockSpec((1, new_rows, hd), lambda s, c, pt: (s, 0, 0)),
                  pl.BlockSpec((qrows, LANES), lambda s, c, pt: (0, 0)),
                  pl.BlockSpec(memory_space=pl.ANY),
                  pl.BlockSpec(memory_space=pl.ANY)],
        out_specs=pl.BlockSpec((1, qrows, hd), lambda s, c, pt: (s, 0, 0)),
        scratch_shapes=[pltpu.VMEM((2, pps, prow, hd), F32), pltpu.VMEM((2, pps, prow, hd), F32),
                        pltpu.VMEM((prow, hd), F32), pltpu.VMEM((prow, hd), F32),
                        pltpu.VMEM((qrows, LANES), F32), pltpu.VMEM((qrows, hd), F32),
                        pltpu.SemaphoreType.DMA((2,)), pltpu.SemaphoreType.DMA((2,))],
    )
    return pl.pallas_call(
        functools.partial(_sb_sample_kernel, n_new=n_new, pps=pps, n_pages=n_pages, page0=page0),
        grid_spec=grid_spec,
        out_shape=jax.ShapeDtypeStruct((b, qrows, hd), BF16),
        compiler_params=_cparams(("arbitrary", "arbitrary")),
    )(page_table.reshape(-1), q, k_new, v_new, bias_rows, k_pool, v_pool)


def _router_kernel(x_ref, g_ref, wh_ref, wl_ref, b_ref, h_ref, info_ref):
    xv = x_ref[...]
    ms = jnp.mean(xv * xv, axis=-1, keepdims=True)
    h = xv * lax.rsqrt(ms + RMS_EPS) * g_ref[...]
    h_ref[...] = h
    h_hi = h.astype(BF16)
    h_lo = (h - h_hi.astype(F32)).astype(BF16)
    logits = _dot(h_hi, wh_ref[...]) + _dot(h_hi, wl_ref[...]) + _dot(h_lo, wh_ref[...]) + b_ref[...]
    lane = lax.broadcasted_iota(jnp.int32, logits.shape, 1)
    big = jnp.int32(1 << 20)
    is_g = (lane >= N_EXPERTS) & (lane < N_EXPERTS + N_GROUPS)
    gl = jnp.where(is_g, logits, -jnp.inf)
    gmax = jnp.max(gl, axis=-1, keepdims=True)
    gidx = jnp.min(jnp.where(gl == gmax, lane - N_EXPERTS, big), axis=-1, keepdims=True)
    p_group = 1.0 / jnp.sum(jnp.exp(gl - gmax), axis=-1, keepdims=True)
    lo = gidx * EXPERTS_PER_GROUP
    in_grp = (lane >= lo) & (lane < lo + EXPERTS_PER_GROUP)
    el = jnp.where(in_grp, logits, -jnp.inf)
    emax = jnp.max(el, axis=-1, keepdims=True)
    pe = jnp.exp(el - emax)
    prob = pe / jnp.sum(pe, axis=-1, keepdims=True)
    prob = jnp.where(in_grp, prob, -1.0)
    p1 = jnp.max(prob, axis=-1, keepdims=True)
    i1 = jnp.min(jnp.where(prob == p1, lane, big), axis=-1, keepdims=True)
    rest = jnp.where(lane == i1, -1.0, prob)
    p2 = jnp.max(rest, axis=-1, keepdims=True)
    i2 = jnp.min(jnp.where(rest == p2, lane, big), axis=-1, keepdims=True)
    tot = p1 + p2
    w1 = p1 / tot * p_group
    w2 = p2 / tot * p_group
    info = jnp.where(lane == 0, i1.astype(F32),
                     jnp.where(lane == 1, i2.astype(F32),
                               jnp.where(lane == 2, w1, jnp.where(lane == 3, w2, 0.0))))
    info_ref[...] = info


def _router(x, g, w_router, b_router, w_group, b_group):
    t, d = x.shape
    pad = LANES - N_EXPERTS - N_GROUPS
    w = jnp.concatenate([w_router, w_group, jnp.zeros((d, pad), F32)], axis=1)
    bias = jnp.concatenate([b_router, b_group, jnp.zeros((pad,), F32)]).reshape(1, LANES)
    w_hi = w.astype(BF16)
    w_lo = (w - w_hi.astype(F32)).astype(BF16)
    tm = _pick(t, (256, 128, 8))
    return pl.pallas_call(
        _router_kernel,
        grid=(t // tm,),
        in_specs=[pl.BlockSpec((tm, d), lambda i: (i, 0)),
                  pl.BlockSpec((1, d), lambda i: (0, 0)),
                  pl.BlockSpec((d, LANES), lambda i: (0, 0)),
                  pl.BlockSpec((d, LANES), lambda i: (0, 0)),
                  pl.BlockSpec((1, LANES), lambda i: (0, 0))],
        out_specs=[pl.BlockSpec((tm, d), lambda i: (i, 0)),
                   pl.BlockSpec((tm, LANES), lambda i: (i, 0))],
        out_shape=[jax.ShapeDtypeStruct((t, d), F32), jax.ShapeDtypeStruct((t, LANES), F32)],
        compiler_params=_cparams(("parallel",)),
    )(x, g.reshape(1, d), w_hi, w_lo, bias)


def _moe_kernel(te_ref, r0_ref, nv_ref, nused_ref, tok_ref, dst_ref, h_hbm, wg_ref, wu_ref, wd_ref, y_hbm,
                xbuf0, xbuf1, ybuf0, ybuf1, gsem, ssem, *, tm, n_pairs):
    del te_ref
    i = pl.program_id(0)
    n_valid = nv_ref[i]
    n_used = nused_ref[0]
    xbuf = (xbuf0, xbuf1)
    ybuf = (ybuf0, ybuf1)

    def start_gather(tile, sl):
        base = r0_ref[tile]
        for r in range(tm):
            pltpu.make_async_copy(h_hbm.at[pl.ds(tok_ref[base + r], 1)],
                                  xbuf[sl].at[pl.ds(r, 1)], gsem.at[sl]).start()

    def wait_gather(sl):
        pltpu.make_async_copy(h_hbm.at[pl.ds(0, tm)], xbuf[sl], gsem.at[sl]).wait()

    def start_scatter(tile, sl, valid):
        base = r0_ref[tile]
        spare = n_pairs + sl * tm
        for r in range(tm):
            dst = jnp.where(r < valid, dst_ref[base + r], spare + r)
            pltpu.make_async_copy(ybuf[sl].at[pl.ds(r, 1)], y_hbm.at[pl.ds(dst, 1)], ssem.at[sl]).start()

    def wait_scatter(sl):
        pltpu.make_async_copy(ybuf[sl], y_hbm.at[pl.ds(0, tm)], ssem.at[sl]).wait()

    @pl.when(i == 0)
    def _():
        ybuf1[...] = jnp.zeros(ybuf1.shape, ybuf1.dtype)
        pltpu.make_async_copy(ybuf1, y_hbm.at[pl.ds(n_pairs, tm)], ssem.at[0]).start()
        start_gather(0, 0)

    def used_step(sl):
        wait_gather(sl)
        wait_scatter(sl)
        start_gather(i + 1, 1 - sl)
        prev = jnp.maximum(i - 1, 0)
        start_scatter(prev, 1 - sl, jnp.where(i >= 1, nv_ref[prev], 0))
        x = xbuf[sl][...].astype(BF16)
        a = _dot(x, wg_ref[0].astype(BF16))
        u = _dot(x, wu_ref[0].astype(BF16))
        act = a * jax.nn.sigmoid(a) * u
        ybuf[sl][...] = _dot(act.astype(BF16), wd_ref[0].astype(BF16))

    def flush_step(sl):
        last = i - 1
        wait_gather(sl)
        wait_scatter(sl)
        start_scatter(last, 1 - sl, nv_ref[last])
        wait_scatter(1 - sl)

    for sl in range(2):
        pl.when((n_valid > 0) & (i % 2 == sl))(functools.partial(used_step, sl))
        pl.when((i == n_used) & (i % 2 == sl))(functools.partial(flush_step, sl))


def _lookup(table, idx):
    n = table.shape[0]
    hit = idx[:, None] == jnp.arange(n, dtype=jnp.int32)[None, :]
    return jnp.sum(jnp.where(hit, table[None, :], 0), axis=1)


def _moe(h, info, w_gate, w_up, w_down, layer):
    t, d = h.shape
    n_exp, ff = N_EXPERTS, w_gate.shape[-1]
    tm = MOE_TILE
    pairs = 2 * t
    n_tiles = -(-pairs // tm) + n_exp + 1
    eid = jnp.concatenate([info[:, 0], info[:, 1]]).astype(jnp.int32)
    sorted_eid, order = lax.sort_key_val(eid, jnp.arange(pairs, dtype=jnp.int32))
    bounds = jnp.sum((sorted_eid[None, :] < jnp.arange(n_exp + 1, dtype=jnp.int32)[:, None]).astype(jnp.int32),
                     axis=1)
    starts, counts = bounds[:-1], bounds[1:] - bounds[:-1]
    tiles_e = (counts + tm - 1) // tm
    tile_end = jnp.cumsum(tiles_e)
    n_used = tile_end[-1]
    tile = jnp.arange(n_tiles, dtype=jnp.int32)
    tile_e = jnp.minimum(jnp.sum((tile[:, None] >= tile_end[None, :]).astype(jnp.int32), axis=1), n_exp - 1)
    used = tile < n_used
    k_in_e = tile - _lookup(tile_end - tiles_e, tile_e)
    tile_r0 = jnp.where(used, _lookup(starts, tile_e) + k_in_e * tm, 0).astype(jnp.int32)
    tile_valid = jnp.where(used, jnp.clip(_lookup(counts, tile_e) - k_in_e * tm, 0, tm), 0).astype(jnp.int32)
    last_e = jnp.sum(jnp.where(tile == n_used - 1, tile_e, 0))
    tile_e = jnp.where(used, tile_e, last_e).astype(jnp.int32)
    tok_sorted = jnp.pad(order % t, (0, tm))
    dst_sorted = jnp.pad(order, (0, tm))

    wg = w_gate.reshape((-1,) + w_gate.shape[-2:])
    wu = w_up.reshape((-1,) + w_up.shape[-2:])
    wd = w_down.reshape((-1,) + w_down.shape[-2:])
    e0 = layer * n_exp

    def w_map(i, te, *_):
        return (e0 + te[i], 0, 0)

    grid_spec = pltpu.PrefetchScalarGridSpec(
        num_scalar_prefetch=6,
        grid=(n_tiles,),
        in_specs=[pl.BlockSpec(memory_space=pl.ANY),
                  pl.BlockSpec((1, d, ff), w_map),
                  pl.BlockSpec((1, d, ff), w_map),
                  pl.BlockSpec((1, ff, d), w_map)],
        out_specs=pl.BlockSpec(memory_space=pl.ANY),
        scratch_shapes=[pltpu.VMEM((tm, d), F32)] * 4
                       + [pltpu.SemaphoreType.DMA((2,)), pltpu.SemaphoreType.DMA((2,))],
    )
    return pl.pallas_call(
        functools.partial(_moe_kernel, tm=tm, n_pairs=pairs),
        grid_spec=grid_spec,
        out_shape=jax.ShapeDtypeStruct((pairs + 2 * tm, d), F32),
        compiler_params=_cparams(("arbitrary",)),
    )(tile_e, tile_r0, tile_valid, n_used.reshape(1).astype(jnp.int32), tok_sorted, dst_sorted, h, wg, wu, wd)


def _rope_tables(pos):
    half = ROPE_DIM // 2
    inv_freq = ROPE_THETA ** (-2.0 * jnp.arange(half, dtype=F32) / ROPE_DIM)
    ang = pos.astype(F32)[:, None] * inv_freq[None, :]
    cos, sin = jnp.cos(ang), jnp.sin(ang)
    t = pos.shape[0]
    ones = jnp.ones((t, SWA_HEAD_DIM - ROPE_DIM), F32)
    zeros = jnp.zeros((t, half), F32)
    zrest = jnp.zeros((t, SWA_HEAD_DIM - ROPE_DIM), F32)
    c_head = jnp.concatenate([cos, cos, ones], axis=1)
    plus_head = jnp.concatenate([zeros, sin, zrest], axis=1)
    minus_head = jnp.concatenate([-sin, zeros, zrest], axis=1)
    rep = LANES // SWA_HEAD_DIM
    return tuple(jnp.concatenate([a] * rep, axis=1) for a in (c_head, plus_head, minus_head))


def kernel(x_prompt, x_sample, mem_prompt, cache_swa_k, cache_swa_v, cache_sb_k, cache_sb_v, cache_mem_k, cache_mem_v, page_table, norm_mix, w_in_swa, sinks_swa, w_out_swa, w_in_sb, sb_bias, w_out_sb, norm_mem_q, norm_mem_kv, w_mem_q, w_mem_kv, w_mem_o, norm_ffn, w_group, b_group, w_router, b_router, w_gate, w_up, w_down, norm_final):
    b_p, s_p, d = x_prompt.shape
    b_s, n_s, _ = x_sample.shape
    depth = norm_mix.shape[0]
    t_p, t_s = b_p * s_p, b_s * n_s
    t = t_p + t_s
    past_len = page_table.shape[1] * cache_sb_k.shape[2]
    assert s_p >= WINDOW and s_p % WINDOW == 0

    x = (x_prompt.reshape(t_p, d), x_sample.reshape(t_s, d))
    rope = (s_p, past_len, n_s)
    mem_flat = mem_prompt.reshape(-1, d)
    mem_m = mem_prompt.shape[1]
    mem_w = MEM_HEADS * MEM_HEAD_DIM
    sb_pool_k = cache_sb_k.reshape(-1, cache_sb_k.shape[2] * SB_KV_HEADS, SB_HEAD_DIM)
    sb_pool_v = cache_sb_v.reshape(-1, cache_sb_v.shape[2] * SB_KV_HEADS, SB_HEAD_DIM)
    mem_cache_k = cache_mem_k.reshape(-1, cache_mem_k.shape[2] * MEM_HEADS, MEM_HEAD_DIM)
    mem_cache_v = cache_mem_v.reshape(-1, cache_mem_v.shape[2] * MEM_HEADS, MEM_HEAD_DIM)

    swa_kp, swa_vp, swa_ks, swa_vs = [], [], [], []
    sb_kp, sb_vp, sb_ks, sb_vs = [], [], [], []
    mem_kp, mem_vp = [], []
    moe = None
    for i in range(depth):
        j = i // 2
        first = moe is None
        if i % 2 == 0:
            nq = SWA_HEADS * SWA_HEAD_DIM
            nk = SWA_KV_HEADS * SWA_HEAD_DIM
            res = _norm_matmul(x, norm_mix[i], w_in_swa[j].astype(BF16), moe=moe, rope=rope,
                               rope_cols=nq + nk, emit_x=not first)
            qkv, x = (res, x) if first else res
            o_p = _swa_prompt(qkv, sinks_swa[j], b_p, s_p)
            qs = qkv[t_p:, :nq].reshape(b_s, n_s, SWA_KV_HEADS, SWA_GROUP, SWA_HEAD_DIM)
            qs = qs.transpose(0, 2, 3, 1, 4).reshape(b_s, SWA_KV_HEADS, SWA_GROUP * n_s, 1, SWA_HEAD_DIM)
            own = jnp.eye(SWA_KV_HEADS, dtype=F32)[None, :, None, :, None]
            qs = (qs * own).astype(BF16).reshape(b_s, SWA_HEADS * n_s, nk)
            kn = qkv[t_p:, nq:nq + nk].reshape(b_s, n_s, nk)
            vn = qkv[t_p:, nq + nk:].reshape(b_s, n_s, nk)
            sink_rows = jnp.broadcast_to(jnp.repeat(sinks_swa[j], n_s)[:, None], (SWA_HEADS * n_s, LANES))
            o_s, kbuf, vbuf = _swa_sample(qs, kn, vn, cache_swa_k[j].reshape(b_s, -1, nk),
                                          cache_swa_v[j].reshape(b_s, -1, nk), sink_rows)
            o_s = o_s.reshape(b_s, SWA_KV_HEADS, SWA_GROUP, n_s, SWA_KV_HEADS, SWA_HEAD_DIM)
            o_s = jnp.sum(o_s * own[:, :, :, None].astype(BF16), axis=4)
            o_s = o_s.transpose(0, 3, 1, 2, 4).reshape(t_s, nq)
            last = qkv[:t_p].reshape(b_p, s_p, -1)[:, s_p - WINDOW:, nq:]
            swa_kp.append(last[:, :, :nk].reshape(b_p, WINDOW, SWA_KV_HEADS, SWA_HEAD_DIM))
            swa_vp.append(last[:, :, nk:].reshape(b_p, WINDOW, SWA_KV_HEADS, SWA_HEAD_DIM))
            swa_ks.append(kbuf.reshape(b_s, -1, SWA_KV_HEADS, SWA_HEAD_DIM))
            swa_vs.append(vbuf.reshape(b_s, -1, SWA_KV_HEADS, SWA_HEAD_DIM))
            w_out = w_out_swa[j]
        else:
            nq = SB_HEADS * SB_HEAD_DIM
            nk = SB_KV_HEADS * SB_HEAD_DIM
            res = _norm_matmul(x, norm_mix[i], w_in_sb[j].astype(BF16), moe=moe, emit_x=not first)
            qkv, x = (res, x) if first else res
            o_p = _sb_prompt(qkv, sb_bias[j], b_p, s_p)
            qs = qkv[t_p:, :nq].reshape(b_s, n_s, SB_KV_HEADS, SB_GROUP, SB_HEAD_DIM)
            qs = (qs.transpose(0, 2, 3, 1, 4).reshape(b_s, SB_HEADS * n_s, SB_HEAD_DIM) * SB_SCALE).astype(BF16)
            kn = qkv[t_p:, nq:nq + nk].reshape(b_s, n_s * SB_KV_HEADS, SB_HEAD_DIM)
            vn = qkv[t_p:, nq + nk:].reshape(b_s, n_s * SB_KV_HEADS, SB_HEAD_DIM)
            bias_rows = jnp.broadcast_to(jnp.repeat(sb_bias[j], n_s)[:, None], (SB_HEADS * n_s, LANES))
            o_s = _sb_sample(qs, kn, vn, sb_pool_k, sb_pool_v, page_table, bias_rows,
                             j * cache_sb_k.shape[1])
            o_s = o_s.reshape(b_s, SB_KV_HEADS, SB_GROUP, n_s, SB_HEAD_DIM)
            o_s = o_s.transpose(0, 3, 1, 2, 4).reshape(t_s, nq)
            sb_kp.append(qkv[:t_p, nq:nq + nk].reshape(b_p, s_p, SB_KV_HEADS, SB_HEAD_DIM))
            sb_vp.append(qkv[:t_p, nq + nk:].reshape(b_p, s_p, SB_KV_HEADS, SB_HEAD_DIM))
            sb_ks.append(kn.reshape(b_s, n_s, SB_KV_HEADS, SB_HEAD_DIM))
            sb_vs.append(vn.reshape(b_s, n_s, SB_KV_HEADS, SB_HEAD_DIM))
            w_out = w_out_sb[j]
        x = _resid_matmul((o_p, o_s), w_out.astype(BF16), x)

        mkv = _norm_matmul(mem_flat, norm_mem_kv[i], w_mem_kv[i].astype(BF16))
        mem_kp.append(mkv[:, :mem_w].reshape(b_p, mem_m, MEM_HEADS, MEM_HEAD_DIM))
        mem_vp.append(mkv[:, mem_w:].reshape(b_p, mem_m, MEM_HEADS, MEM_HEAD_DIM))
        qm = _norm_matmul(x, norm_mem_q[i], w_mem_q[i].astype(BF16), out_dtype=BF16)
        om_p = _mem_prompt(qm, mkv, b_p, s_p)
        row_pad = 16 - n_s % 16 if n_s % 16 else 0
        qm_s = jnp.pad(qm[t_p:].reshape(b_s, n_s, MEM_HEADS, MEM_HEAD_DIM), ((0, 0), (0, row_pad), (0, 0), (0, 0)))
        qm_s = qm_s.transpose(0, 2, 1, 3).reshape(b_s, MEM_HEADS * (n_s + row_pad), MEM_HEAD_DIM)
        om_s = _mem_sample(qm_s, mem_cache_k, mem_cache_v, i)
        om_s = om_s.reshape(b_s, MEM_HEADS, n_s + row_pad, MEM_HEAD_DIM)[:, :, :n_s].transpose(0, 2, 1, 3)
        x = _resid_matmul((om_p, om_s.reshape(t_s, mem_w)), w_mem_o[i].astype(BF16), x)

        h, info = _router(x, norm_ffn[i], w_router[i], b_router[i], w_group[i], b_group[i])
        moe = (_moe(h, info, w_gate, w_up, w_down, i), info)

    y_prompt = _final_norm(x, moe[0], moe[1], norm_final, 0, t_p).reshape(b_p, s_p, d)
    y_sample = _final_norm(x, moe[0], moe[1], norm_final, t_p, t_s).reshape(b_s, n_s, d)
    return (y_prompt, y_sample,
            jnp.stack(swa_kp), jnp.stack(swa_vp), jnp.stack(swa_ks), jnp.stack(swa_vs),
            jnp.stack(sb_kp), jnp.stack(sb_vp), jnp.stack(sb_ks), jnp.stack(sb_vs),
            jnp.stack(mem_kp), jnp.stack(mem_vp))
```

```python
import functools
import math

import jax
import jax.numpy as jnp
from jax import lax
from jax.experimental import pallas as pl
from jax.experimental.pallas import tpu as pltpu

F32 = jnp.float32
BF16 = jnp.bfloat16

SWA_HEADS, SWA_KV_HEADS, SWA_HEAD_DIM = 32, 4, 64
SWA_GROUP = SWA_HEADS // SWA_KV_HEADS
WINDOW = 128
ROPE_THETA = 500000.0
ROPE_DIM = SWA_HEAD_DIM // 4
SWA_SCALE = SWA_HEAD_DIM ** -0.5
SB_HEADS, SB_KV_HEADS, SB_HEAD_DIM = 16, 4, 128
SB_GROUP = SB_HEADS // SB_KV_HEADS
SB_BLOCK = 128
SB_SCALE = SB_HEAD_DIM ** -0.5
MEM_HEADS, MEM_HEAD_DIM = 4, 128
MEM_SCALE = MEM_HEAD_DIM ** -0.5
N_GROUPS, EXPERTS_PER_GROUP = 4, 8
N_EXPERTS = N_GROUPS * EXPERTS_PER_GROUP
RMS_EPS = 1e-6
NEG_INF = -1e30

LANES = 128
VMEM_LIMIT = 56 * 1024 * 1024
MOE_TILE = 256
SB_PAGES_PER_STEP = 32
SB_PAGE_CHAIN = 4


def _cparams(sem):
    return pltpu.CompilerParams(dimension_semantics=sem, vmem_limit_bytes=VMEM_LIMIT)


def _pick(n, prefs):
    for p in prefs:
        if n % p == 0:
            return p
    return n


def _dot_t(a, b):
    return lax.dot_general(a, b, (((1,), (1,)), ((), ())), preferred_element_type=F32)


def _dot(a, b):
    return jnp.dot(a, b, preferred_element_type=F32)


def _rows(refs, n_first):
    if len(refs) == 1:
        return refs[0][...]
    return jnp.where(pl.program_id(0) < n_first, refs[0][...], refs[1][...])


def _row_specs(srcs, tm, n_first):
    if not isinstance(srcs, tuple):
        return [pl.BlockSpec((tm, srcs.shape[1]), lambda i, *_: (i, 0))], [srcs]
    width = srcs[0].shape[1]
    return ([pl.BlockSpec((tm, width), lambda i, *_: (jnp.minimum(i, n_first - 1), 0)),
             pl.BlockSpec((tm, width), lambda i, *_: (jnp.maximum(i - n_first, 0), 0))], list(srcs))


def _moe_combine(xv, ya_ref, yb_ref, info_ref):
    info = info_ref[...]
    return xv + ya_ref[...] * info[:, 2:3] + yb_ref[...] * info[:, 3:4]


def _norm_matmul_kernel(*refs, n_x, n_first, add, emit_x, rope_cols, tn):
    it = iter(refs)
    x_refs = [next(it) for _ in range(n_x)]
    ya_ref = next(it) if add else None
    yb_ref = next(it) if add else None
    info_ref = next(it) if add else None
    g_ref = next(it)
    w_ref = next(it)
    if rope_cols:
        cos_ref, sp_ref, sm_ref = next(it), next(it), next(it)
    o_ref = next(it)
    xo_ref = next(it) if emit_x else None
    h_scr = next(it)
    j = pl.program_id(1)

    @pl.when(j == 0)
    def _():
        xv = _rows(x_refs, n_first)
        if add:
            xv = _moe_combine(xv, ya_ref, yb_ref, info_ref)
        if emit_x:
            xo_ref[...] = xv
        ms = jnp.mean(xv * xv, axis=-1, keepdims=True)
        h_scr[...] = (xv * lax.rsqrt(ms + RMS_EPS) * g_ref[...]).astype(BF16)

    acc = _dot(h_scr[...], w_ref[...])
    if not rope_cols:
        o_ref[...] = acc.astype(o_ref.dtype)
        return

    @pl.when(j * tn < rope_cols)
    def _():
        reps = tn // LANES
        cos = jnp.concatenate([cos_ref[...]] * reps, axis=1)
        s_plus = jnp.concatenate([sp_ref[...]] * reps, axis=1)
        s_minus = jnp.concatenate([sm_ref[...]] * reps, axis=1)
        half = ROPE_DIM // 2
        roped = (acc * cos + pltpu.roll(acc, half, axis=1) * s_plus
                 + pltpu.roll(acc, tn - half, axis=1) * s_minus)
        col = j * tn + lax.broadcasted_iota(jnp.int32, acc.shape, 1)
        o_ref[...] = jnp.where(col < rope_cols, roped, acc).astype(o_ref.dtype)

    @pl.when(j * tn >= rope_cols)
    def _():
        o_ref[...] = acc.astype(o_ref.dtype)


def _norm_matmul(x, g, w, *, moe=None, rope=None, rope_cols=0, emit_x=False,
                 out_dtype=F32):
    pair = isinstance(x, tuple)
    t = x[0].shape[0] + x[1].shape[0] if pair else x.shape[0]
    d = x[0].shape[1] if pair else x.shape[1]
    n = w.shape[1]
    add = moe is not None
    tm = _pick(math.gcd(x[0].shape[0], x[1].shape[0]) if pair else t, (512, 256))
    tn = _pick(n, (1024, 512, 256, 128) if add else (1280, 1024, 512, 256, 128))
    n_i = t // tm
    n_first = x[0].shape[0] // tm if pair else n_i
    in_specs, args = _row_specs(x, tm, n_first)
    n_x = len(args)
    if add:
        in_specs += [pl.BlockSpec((tm, d), lambda i, j: (i, 0)),
                     pl.BlockSpec((tm, d), lambda i, j: (i + n_i, 0)),
                     pl.BlockSpec((tm, LANES), lambda i, j: (i, 0))]
        args += [moe[0], moe[0], moe[1]]
    in_specs += [pl.BlockSpec((1, d), lambda i, j: (0, 0)),
                 pl.BlockSpec((d, tn), lambda i, j: (0, j))]
    args += [g.reshape(1, d), w]
    if rope_cols:
        seq, pos0, n_new = rope
        assert pair and seq % tm == 0 and tm % n_new == 0
        period = seq // tm
        pos = jnp.concatenate([jnp.arange(seq), pos0 + jnp.arange(tm) % n_new])
        in_specs += [pl.BlockSpec((tm, LANES), lambda i, j: (jnp.where(i < n_first, i % period, period), 0))] * 3
        args += list(_rope_tables(pos))
    out_shape = [jax.ShapeDtypeStruct((t, n), out_dtype)]
    out_specs = [pl.BlockSpec((tm, tn), lambda i, j: (i, j))]
    if emit_x:
        out_shape.append(jax.ShapeDtypeStruct((t, d), F32))
        out_specs.append(pl.BlockSpec((tm, d), lambda i, j: (i, 0)))
    res = pl.pallas_call(
        functools.partial(_norm_matmul_kernel, n_x=n_x, n_first=n_first,
                          add=add, emit_x=emit_x, rope_cols=rope_cols, tn=tn),
        grid=(n_i, n // tn),
        in_specs=in_specs,
        out_specs=out_specs,
        out_shape=out_shape,
        scratch_shapes=[pltpu.VMEM((tm, d), BF16)],
        compiler_params=_cparams(("parallel", "arbitrary")),
    )(*args)
    return res if emit_x else res[0]


def _resid_matmul_kernel(*refs, n_a, n_r, n_first):
    a_refs, w_ref, r_refs, o_ref = refs[:n_a], refs[n_a], refs[n_a + 1:n_a + 1 + n_r], refs[-1]
    o_ref[...] = _rows(r_refs, n_first) + _dot(_rows(a_refs, n_first), w_ref[...])


def _resid_matmul(a, w, resid):
    first = a[0] if isinstance(a, tuple) else resid[0] if isinstance(resid, tuple) else None
    t = sum(p.shape[0] for p in a) if isinstance(a, tuple) else a.shape[0]
    n = w.shape[1]
    tm = _pick(t if first is None else math.gcd(first.shape[0], t - first.shape[0]), (512, 256))
    n_first = t // tm if first is None else first.shape[0] // tm
    a_specs, a_args = _row_specs(a, tm, n_first)
    r_specs, r_args = _row_specs(resid, tm, n_first)
    return pl.pallas_call(
        functools.partial(_resid_matmul_kernel, n_a=len(a_args), n_r=len(r_args), n_first=n_first),
        grid=(t // tm,),
        in_specs=a_specs + [pl.BlockSpec(w.shape, lambda i: (0, 0))] + r_specs,
        out_specs=pl.BlockSpec((tm, n), lambda i: (i, 0)),
        out_shape=jax.ShapeDtypeStruct((t, n), F32),
        compiler_params=_cparams(("parallel",)),
    )(*a_args, w, *r_args)


def _final_norm_kernel(x_ref, ya_ref, yb_ref, info_ref, g_ref, o_ref):
    xv = _moe_combine(x_ref[...], ya_ref, yb_ref, info_ref)
    ms = jnp.mean(xv * xv, axis=-1, keepdims=True)
    o_ref[...] = xv * lax.rsqrt(ms + RMS_EPS) * g_ref[...]


def _final_norm(x, y2, info, g, row0, n_rows):
    t, d = x.shape
    tm = _pick(n_rows, (256, 128, 8))
    assert row0 % tm == 0 and t % tm == 0
    b0, bt = row0 // tm, t // tm
    return pl.pallas_call(
        _final_norm_kernel,
        grid=(n_rows // tm,),
        in_specs=[pl.BlockSpec((tm, d), lambda i: (i + b0, 0)),
                  pl.BlockSpec((tm, d), lambda i: (i + b0, 0)),
                  pl.BlockSpec((tm, d), lambda i: (i + b0 + bt, 0)),
                  pl.BlockSpec((tm, LANES), lambda i: (i + b0, 0)),
                  pl.BlockSpec((1, d), lambda i: (0, 0))],
        out_specs=pl.BlockSpec((tm, d), lambda i: (i, 0)),
        out_shape=jax.ShapeDtypeStruct((n_rows, d), F32),
        compiler_params=_cparams(("parallel",)),
    )(x, y2, y2, info, g.reshape(1, d))


def _swa_prompt_kernel(sink_ref, q_ref, kp_ref, kc_ref, vp_ref, vc_ref, o_ref):
    n = pl.program_id(1)
    blk = WINDOW
    qi = lax.broadcasted_iota(jnp.int32, (blk, 2 * blk), 0)
    sj = lax.broadcasted_iota(jnp.int32, (blk, 2 * blk), 1)
    rel = qi + blk - sj
    first_key = jnp.where(n > 0, 0, blk)
    valid = (rel >= 0) & (rel < WINDOW) & (sj >= first_key)
    k = jnp.concatenate([kp_ref[...], kc_ref[...]], axis=0).astype(BF16)
    v = jnp.concatenate([vp_ref[...], vc_ref[...]], axis=0).astype(BF16)
    hd = SWA_HEAD_DIM
    for kh in range(SWA_KV_HEADS):
        k_h = k[:, kh * hd:(kh + 1) * hd]
        v_h = v[:, kh * hd:(kh + 1) * hd]
        for g2 in range(SWA_GROUP // 2):
            pair = []
            for gg in range(2):
                h = kh * SWA_GROUP + g2 * 2 + gg
                q_h = q_ref[:, h * hd:(h + 1) * hd].astype(BF16)
                s = _dot_t(q_h, k_h) * SWA_SCALE
                s = jnp.where(valid, s, NEG_INF)
                sink = sink_ref[h]
                m = jnp.maximum(jnp.max(s, axis=-1, keepdims=True), sink)
                p = jnp.exp(s - m)
                denom = jnp.sum(p, axis=-1, keepdims=True) + jnp.exp(sink - m)
                pair.append(_dot(p.astype(BF16), v_h) / denom)
            h0 = kh * SWA_GROUP + g2 * 2
            o_ref[:, h0 * hd:(h0 + 2) * hd] = jnp.concatenate(pair, axis=1).astype(o_ref.dtype)


def _swa_prompt(qkv, sinks, batch, seq):
    nq = SWA_HEADS * SWA_HEAD_DIM
    nk = SWA_KV_HEADS * SWA_HEAD_DIM
    nb = seq // WINDOW
    kcol, vcol = nq // nk, nq // nk + 1

    def prev(b, n):
        return b * nb + jnp.maximum(n - 1, 0)

    return pl.pallas_call(
        _swa_prompt_kernel,
        grid=(batch, nb),
        in_specs=[pl.BlockSpec(memory_space=pltpu.SMEM),
                  pl.BlockSpec((WINDOW, nq), lambda b, n: (b * nb + n, 0)),
                  pl.BlockSpec((WINDOW, nk), lambda b, n: (prev(b, n), kcol)),
                  pl.BlockSpec((WINDOW, nk), lambda b, n: (b * nb + n, kcol)),
                  pl.BlockSpec((WINDOW, nk), lambda b, n: (prev(b, n), vcol)),
                  pl.BlockSpec((WINDOW, nk), lambda b, n: (b * nb + n, vcol))],
        out_specs=pl.BlockSpec((WINDOW, nq), lambda b, n: (b * nb + n, 0)),
        out_shape=jax.ShapeDtypeStruct((batch * seq, nq), BF16),
        compiler_params=_cparams(("parallel", "arbitrary")),
    )(sinks, qkv, qkv, qkv, qkv, qkv)


def _swa_sample_kernel(q_ref, kn_ref, vn_ref, ck_ref, cv_ref, sink_ref, o_ref, nk_ref, nv_ref,
                       kall, vall, *, n_new, seqs):
    w = WINDOW
    rows = q_ref.shape[1]
    zeros = jnp.zeros((w, kall.shape[1]), F32)
    kall[w:2 * w, :] = zeros
    vall[w:2 * w, :] = zeros
    r = lax.broadcasted_iota(jnp.int32, (rows, 2 * w), 0)
    j = lax.broadcasted_iota(jnp.int32, (rows, 2 * w), 1)
    qn = r % n_new
    valid = ((j < w) & (j > qn)) | ((j >= w) & (j - w <= qn))
    sink = sink_ref[:, 0:1]

    def body(s, carry):
        kall[0:w, :] = ck_ref[s]
        vall[0:w, :] = cv_ref[s]
        kall[w:w + n_new, :] = kn_ref[s]
        vall[w:w + n_new, :] = vn_ref[s]
        nk_ref[s] = kall[n_new:n_new + w, :]
        nv_ref[s] = vall[n_new:n_new + w, :]
        lg = _dot_t(q_ref[s], kall[...].astype(BF16)) * SWA_SCALE
        lg = jnp.where(valid, lg, NEG_INF)
        m = jnp.maximum(jnp.max(lg, axis=-1, keepdims=True), sink)
        p = jnp.exp(lg - m)
        denom = jnp.sum(p, axis=-1, keepdims=True) + jnp.exp(sink - m)
        o_ref[s] = (_dot(p.astype(BF16), vall[...].astype(BF16)) / denom).astype(o_ref.dtype)
        return carry

    lax.fori_loop(0, seqs, body, 0)


def _swa_sample(q, k_new, v_new, cache_k, cache_v, sink_rows):
    b, rows, _ = q.shape
    n_new = k_new.shape[1]
    w, kw = cache_k.shape[1], cache_k.shape[2]
    assert w == WINDOW
    seqs = _pick(b, (8, 4, 2, 1))
    return pl.pallas_call(
        functools.partial(_swa_sample_kernel, n_new=n_new, seqs=seqs),
        grid=(b // seqs,),
        in_specs=[pl.BlockSpec((seqs, rows, kw), lambda i: (i, 0, 0)),
                  pl.BlockSpec((seqs, n_new, kw), lambda i: (i, 0, 0)),
                  pl.BlockSpec((seqs, n_new, kw), lambda i: (i, 0, 0)),
                  pl.BlockSpec((seqs, w, kw), lambda i: (i, 0, 0)),
                  pl.BlockSpec((seqs, w, kw), lambda i: (i, 0, 0)),
                  pl.BlockSpec((rows, LANES), lambda i: (0, 0))],
        out_specs=[pl.BlockSpec((seqs, rows, kw), lambda i: (i, 0, 0)),
                   pl.BlockSpec((seqs, w, kw), lambda i: (i, 0, 0)),
                   pl.BlockSpec((seqs, w, kw), lambda i: (i, 0, 0))],
        out_shape=[jax.ShapeDtypeStruct((b, rows, kw), BF16),
                   jax.ShapeDtypeStruct((b, w, kw), F32),
                   jax.ShapeDtypeStruct((b, w, kw), F32)],
        scratch_shapes=[pltpu.VMEM((2 * w, kw), F32), pltpu.VMEM((2 * w, kw), F32)],
        compiler_params=_cparams(("parallel",)),
    )(q, k_new, v_new, cache_k, cache_v, sink_rows)


def _mem_heads(q, k_of, v_of):
    hd = MEM_HEAD_DIM
    outs = []
    for h in range(MEM_HEADS):
        s = _dot_t(q[:, h * hd:(h + 1) * hd], k_of(h)) * MEM_SCALE
        m = jnp.max(s, axis=-1, keepdims=True)
        p = jnp.exp(s - m)
        denom = jnp.sum(p, axis=-1, keepdims=True)
        outs.append(_dot(p.astype(BF16), v_of(h)) / denom)
    return jnp.concatenate(outs, axis=1)


def _mem_prompt_kernel(q_ref, k_ref, v_ref, o_ref):
    hd = MEM_HEAD_DIM
    k = k_ref[...].astype(BF16)
    v = v_ref[...].astype(BF16)
    o_ref[...] = _mem_heads(q_ref[...], lambda h: k[:, h * hd:(h + 1) * hd],
                            lambda h: v[:, h * hd:(h + 1) * hd]).astype(o_ref.dtype)


def _mem_prompt(q, kv, batch, seq):
    width = MEM_HEADS * MEM_HEAD_DIM
    m = kv.shape[0] // batch
    tq = _pick(seq, (512, 256, 128))
    nq = seq // tq
    return pl.pallas_call(
        _mem_prompt_kernel,
        grid=(batch, nq),
        in_specs=[pl.BlockSpec((tq, width), lambda b, i: (b * nq + i, 0)),
                  pl.BlockSpec((m, width), lambda b, i: (b, 0)),
                  pl.BlockSpec((m, width), lambda b, i: (b, 1))],
        out_specs=pl.BlockSpec((tq, width), lambda b, i: (b * nq + i, 0)),
        out_shape=jax.ShapeDtypeStruct((batch * seq, width), BF16),
        compiler_params=_cparams(("parallel", "arbitrary")),
    )(q, kv, kv)


def _mem_sample_kernel(q_ref, k_ref, v_ref, o_ref, *, seqs):
    nh = MEM_HEADS
    qrows, cols = q_ref.shape[1], k_ref.shape[1]
    row_h = lax.broadcasted_iota(jnp.int32, (qrows, cols), 0) // (qrows // nh)
    col_h = lax.broadcasted_iota(jnp.int32, (qrows, cols), 1) % nh
    own = row_h == col_h

    def body(s, carry):
        lg = _dot_t(q_ref[s], k_ref[s].astype(BF16)) * MEM_SCALE
        lg = jnp.where(own, lg, NEG_INF)
        mx = jnp.max(lg, axis=-1, keepdims=True)
        p = jnp.exp(lg - mx)
        denom = jnp.sum(p, axis=-1, keepdims=True)
        o_ref[s] = (_dot(p.astype(BF16), v_ref[s].astype(BF16)) / denom).astype(o_ref.dtype)
        return carry

    lax.fori_loop(0, seqs, body, 0)


def _mem_sample(q, k, v, layer):
    b, qrows, hd = q.shape
    mh = k.shape[1]
    seqs = _pick(b, (8, 4, 2, 1))
    nb = b // seqs
    return pl.pallas_call(
        functools.partial(_mem_sample_kernel, seqs=seqs),
        grid=(nb,),
        in_specs=[pl.BlockSpec((seqs, qrows, hd), lambda i: (i, 0, 0)),
                  pl.BlockSpec((seqs, mh, hd), lambda i: (layer * nb + i, 0, 0)),
                  pl.BlockSpec((seqs, mh, hd), lambda i: (layer * nb + i, 0, 0))],
        out_specs=pl.BlockSpec((seqs, qrows, hd), lambda i: (i, 0, 0)),
        out_shape=jax.ShapeDtypeStruct((b, qrows, hd), BF16),
        compiler_params=_cparams(("parallel",)),
    )(q, k, v)


def _sb_local(z, mask, tri2):
    sp = jnp.maximum(z, 0.0) + jnp.log(1.0 + jnp.exp(-jnp.abs(z)))
    spm = sp if mask is None else jnp.where(mask, sp, 0.0)
    cum = _dot(spm.astype(BF16), tri2)
    kb = z.shape[1]
    a = z - sp - cum[:, :kb]
    if mask is not None:
        a = jnp.where(mask, a, NEG_INF)
    return a, cum[:, kb:]


def _tri2(kb):
    r = lax.broadcasted_iota(jnp.int32, (kb, kb + LANES), 0)
    cidx = lax.broadcasted_iota(jnp.int32, (kb, kb + LANES), 1)
    return jnp.where((cidx >= kb) | (r > cidx), 1.0, 0.0).astype(BF16)


def _sb_prompt_kernel(bias_ref, q_ref, k_ref, v_ref, o_ref, kb_scr, vb_scr, c_scr, acc_scr):
    kh = pl.program_id(1)
    n = pl.program_id(2)
    blk = SB_BLOCK
    hd = SB_HEAD_DIM

    @pl.when(n == 0)
    def _():
        kb_scr[...] = k_ref[...].astype(BF16)
        vb_scr[...] = v_ref[...].astype(BF16)

    q = q_ref[...]
    qs = (jnp.concatenate([q[:, g * hd:(g + 1) * hd] for g in range(SB_GROUP)], axis=0) * SB_SCALE).astype(BF16)
    bias = jnp.concatenate(
        [jnp.full((blk, LANES), bias_ref[kh * SB_GROUP + g], F32) for g in range(SB_GROUP)], axis=0)
    tri2 = _tri2(blk)
    rows = SB_GROUP * blk
    qi = lax.broadcasted_iota(jnp.int32, (rows, blk), 0) % blk
    sj = lax.broadcasted_iota(jnp.int32, (rows, blk), 1)
    diag_mask = sj < qi

    def local(kblk, mask):
        start = pl.multiple_of(kblk * blk, blk)
        z = _dot_t(qs, kb_scr[pl.ds(start, blk), :]) + bias
        return _sb_local(z, mask, tri2)

    def accumulate(kblk, a, tot):
        start = pl.multiple_of(kblk * blk, blk)
        wgt = jnp.exp(a + c_scr[...])
        acc_scr[...] += _dot(wgt.astype(BF16), vb_scr[pl.ds(start, blk), :])
        c_scr[...] -= tot

    c_scr[...] = jnp.zeros_like(c_scr)
    acc_scr[...] = jnp.zeros_like(acc_scr)
    accumulate(n, *local(n, diag_mask))

    def wide(first_blk, width):
        start = pl.multiple_of(first_blk * blk, blk)
        z = _dot_t(qs, kb_scr[pl.ds(start, width * blk), :]) + jnp.concatenate([bias] * width, axis=1)
        a, tot = _sb_local(z, None, _tri2(width * blk))
        c = c_scr[...]
        wgt = jnp.exp(a + jnp.concatenate([c] * width, axis=1))
        acc_scr[...] += _dot(wgt.astype(BF16), vb_scr[pl.ds(start, width * blk), :])
        c_scr[...] = c - tot

    def body(i, carry):
        wide(n - 4 - 4 * i, 4)
        return carry

    lax.fori_loop(0, n // 4, body, 0)

    @pl.when(n % 4 >= 2)
    def _():
        wide(n % 4 - 2, 2)

    @pl.when(n % 2 == 1)
    def _():
        accumulate(0, *local(0, None))

    acc = acc_scr[...]
    o_ref[...] = jnp.concatenate([acc[g * blk:(g + 1) * blk] for g in range(SB_GROUP)],
                                 axis=1).astype(o_ref.dtype)


def _sb_prompt(qkv, bias, batch, seq):
    hd = SB_HEAD_DIM
    nb = seq // SB_BLOCK
    gw = SB_GROUP * hd
    kcol0 = SB_HEADS
    vcol0 = SB_HEADS + SB_KV_HEADS
    return pl.pallas_call(
        _sb_prompt_kernel,
        grid=(batch, SB_KV_HEADS, nb),
        in_specs=[pl.BlockSpec(memory_space=pltpu.SMEM),
                  pl.BlockSpec((SB_BLOCK, gw), lambda b, kh, n: (b * nb + n, kh)),
                  pl.BlockSpec((seq, hd), lambda b, kh, n: (b, kcol0 + kh)),
                  pl.BlockSpec((seq, hd), lambda b, kh, n: (b, vcol0 + kh))],
        out_specs=pl.BlockSpec((SB_BLOCK, gw), lambda b, kh, n: (b * nb + n, kh)),
        out_shape=jax.ShapeDtypeStruct((batch * seq, SB_HEADS * hd), BF16),
        scratch_shapes=[pltpu.VMEM((seq, hd), BF16), pltpu.VMEM((seq, hd), BF16),
                        pltpu.VMEM((SB_GROUP * SB_BLOCK, LANES), F32),
                        pltpu.VMEM((SB_GROUP * SB_BLOCK, hd), F32)],
        compiler_params=_cparams(("parallel", "parallel", "arbitrary")),
    )(bias, qkv, qkv, qkv)


def _sb_sample_kernel(pt_ref, q_ref, kn_ref, vn_ref, bias_ref, kpool, vpool, o_ref,
                      kbuf, vbuf, kpad, vpad, c_scr, acc_scr, ksem, vsem, *, n_new, pps, n_pages, page0):
    seq = pl.program_id(0)
    step = pl.program_id(1)
    steps = pl.num_programs(1)
    gstep = seq * steps + step
    slot = gstep % 2
    nkv = SB_KV_HEADS
    page = SB_BLOCK
    rows = SB_GROUP * n_new
    qr = nkv * rows
    tri2 = _tri2(page)
    q = q_ref[0]
    bias = bias_ref[...]

    def start_fetch(sq, chunk, sl):
        for i in range(pps):
            pg = page0 + pt_ref[sq * n_pages + (n_pages - 1 - (chunk * pps + i))]
            pltpu.make_async_copy(kpool.at[pg], kbuf.at[sl, i], ksem.at[sl]).start()
            pltpu.make_async_copy(vpool.at[pg], vbuf.at[sl, i], vsem.at[sl]).start()

    def wait_fetch(sl):
        pltpu.make_async_copy(kpool.at[pl.ds(0, pps)], kbuf.at[sl], ksem.at[sl]).wait()
        pltpu.make_async_copy(vpool.at[pl.ds(0, pps)], vbuf.at[sl], vsem.at[sl]).wait()

    @pl.when(gstep == 0)
    def _():
        start_fetch(0, 0, 0)

    @pl.when(gstep + 1 < pl.num_programs(0) * steps)
    def _():
        wrap = step + 1 == steps
        start_fetch(jnp.where(wrap, seq + 1, seq), jnp.where(wrap, 0, step + 1), 1 - slot)

    def head(ref, kh):
        return ref[pl.ds(kh, page, stride=nkv), :].astype(BF16)

    def logits(page_refs):
        z = jnp.concatenate(
            [_dot_t(q[kh * rows:(kh + 1) * rows], head(k_ref, kh))
             for k_ref, _ in page_refs for kh in range(nkv)], axis=0)
        return z + jnp.concatenate([bias] * len(page_refs), axis=0)

    def weights(z, mask):
        a, tot = _sb_local(z, mask, tri2)
        return jnp.exp(a).astype(BF16), tot

    def outputs(page_refs, wb, tot):
        outs = [jnp.concatenate(
            [_dot(wb[p * qr + kh * rows:p * qr + (kh + 1) * rows], head(v_ref, kh))
             for kh in range(nkv)], axis=0) for p, (_, v_ref) in enumerate(page_refs)]
        return outs, [tot[p * qr:(p + 1) * qr] for p in range(len(page_refs))]

    def local(groups, mask):
        zs, ws, parts = {}, {}, []
        for t in range(len(groups) + 2):
            if t < len(groups):
                zs[t] = logits(groups[t])
            if 0 <= t - 1 < len(groups):
                ws[t - 1] = weights(zs.pop(t - 1), mask)
            if 0 <= t - 2 < len(groups):
                parts.append(outputs(groups[t - 2], *ws.pop(t - 2)))
        return parts

    def combine(parts):
        off = c_scr[...]
        acc = acc_scr[...]
        for outs, tots in parts:
            for o_p, t_p in zip(outs, tots):
                acc = acc + jnp.exp(off) * o_p
                off = off - t_p
        acc_scr[...] = acc
        c_scr[...] = off

    @pl.when(step == 0)
    def _():
        c_scr[...] = jnp.zeros_like(c_scr)
        acc_scr[...] = jnp.zeros_like(acc_scr)
        kpad[...] = jnp.zeros_like(kpad)
        vpad[...] = jnp.zeros_like(vpad)
        kpad[0:nkv * n_new, :] = kn_ref[0]
        vpad[0:nkv * n_new, :] = vn_ref[0]
        r = lax.broadcasted_iota(jnp.int32, (qr, page), 0)
        j = lax.broadcasted_iota(jnp.int32, (qr, page), 1)
        combine(local([[(kpad, vpad)]], j < (r % n_new)))

    wait_fetch(slot)
    pages = [(kbuf.at[slot, i], vbuf.at[slot, i]) for i in range(pps)]
    chain = min(SB_PAGE_CHAIN, pps)
    combine(local([pages[c:c + chain] for c in range(0, pps, chain)], None))

    @pl.when(step == steps - 1)
    def _():
        o_ref[0] = acc_scr[...].astype(o_ref.dtype)


def _sb_sample(q, k_new, v_new, k_pool, v_pool, page_table, bias_rows, page0):
    b, qrows, hd = q.shape
    new_rows = k_new.shape[1]
    n_new = new_rows // SB_KV_HEADS
    n_pages = page_table.shape[1]
    prow = k_pool.shape[1]
    assert prow == SB_BLOCK * SB_KV_HEADS
    pps = _pick(n_pages, (SB_PAGES_PER_STEP, 8, 4, 2, 1))
    grid_spec = pltpu.PrefetchScalarGridSpec(
        num_scalar_prefetch=1,
        grid=(b, n_pages // pps),
        in_specs=[pl.BlockSpec((1, qrows, hd), lambda s, c, pt: (s, 0, 0)),
                  pl.BlockSpec((1, new_rows, hd), lambda s, c, pt: (s, 0, 0)),
                  pl.BlockSpec((1, new_rows, hd), lambda s, c, pt: (s, 0, 0)),
                  pl.BlockSpec((qrows, LANES), lambda s, c, pt: (0, 0)),
                  pl.BlockSpec(memory_space=pl.ANY),
                  pl.BlockSpec(memory_space=pl.ANY)],
        out_specs=pl.BlockSpec((1, qrows, hd), lambda s, c, pt: (s, 0, 0)),
        scratch_shapes=[pltpu.VMEM((2, pps, prow, hd), F32), pltpu.VMEM((2, pps, prow, hd), F32),
                        pltpu.VMEM((prow, hd), F32), pltpu.VMEM((prow, hd), F32),
                        pltpu.VMEM((qrows, LANES), F32), pltpu.VMEM((qrows, hd), F32),
                        pltpu.SemaphoreType.DMA((2,)), pltpu.SemaphoreType.DMA((2,))],
    )
    return pl.pallas_call(
        functools.partial(_sb_sample_kernel, n_new=n_new, pps=pps, n_pages=n_pages, page0=page0),
        grid_spec=grid_spec,
        out_shape=jax.ShapeDtypeStruct((b, qrows, hd), BF16),
        compiler_params=_cparams(("arbitrary", "arbitrary")),
    )(page_table.reshape(-1), q, k_new, v_new, bias_rows, k_pool, v_pool)


def _router_kernel(x_ref, g_ref, wh_ref, wl_ref, b_ref, h_ref, info_ref):
    xv = x_ref[...]
    ms = jnp.mean(xv * xv, axis=-1, keepdims=True)
    h = xv * lax.rsqrt(ms + RMS_EPS) * g_ref[...]
    h_ref[...] = h
    h_hi = h.astype(BF16)
    h_lo = (h - h_hi.astype(F32)).astype(BF16)
    logits = _dot(h_hi, wh_ref[...]) + _dot(h_hi, wl_ref[...]) + _dot(h_lo, wh_ref[...]) + b_ref[...]
    lane = lax.broadcasted_iota(jnp.int32, logits.shape, 1)
    big = jnp.int32(1 << 20)
    is_g = (lane >= N_EXPERTS) & (lane < N_EXPERTS + N_GROUPS)
    gl = jnp.where(is_g, logits, -jnp.inf)
    gmax = jnp.max(gl, axis=-1, keepdims=True)
    gidx = jnp.min(jnp.where(gl == gmax, lane - N_EXPERTS, big), axis=-1, keepdims=True)
    p_group = 1.0 / jnp.sum(jnp.exp(gl - gmax), axis=-1, keepdims=True)
    lo = gidx * EXPERTS_PER_GROUP
    in_grp = (lane >= lo) & (lane < lo + EXPERTS_PER_GROUP)
    el = jnp.where(in_grp, logits, -jnp.inf)
    emax = jnp.max(el, axis=-1, keepdims=True)
    pe = jnp.exp(el - emax)
    prob = pe / jnp.sum(pe, axis=-1, keepdims=True)
    prob = jnp.where(in_grp, prob, -1.0)
    p1 = jnp.max(prob, axis=-1, keepdims=True)
    i1 = jnp.min(jnp.where(prob == p1, lane, big), axis=-1, keepdims=True)
    rest = jnp.where(lane == i1, -1.0, prob)
    p2 = jnp.max(rest, axis=-1, keepdims=True)
    i2 = jnp.min(jnp.where(rest == p2, lane, big), axis=-1, keepdims=True)
    tot = p1 + p2
    w1 = p1 / tot * p_group
    w2 = p2 / tot * p_group
    info = jnp.where(lane == 0, i1.astype(F32),
                     jnp.where(lane == 1, i2.astype(F32),
                               jnp.where(lane == 2, w1, jnp.where(lane == 3, w2, 0.0))))
    info_ref[...] = info


def _router(x, g, w_router, b_router, w_group, b_group):
    t, d = x.shape
    pad = LANES - N_EXPERTS - N_GROUPS
    w = jnp.concatenate([w_router, w_group, jnp.zeros((d, pad), F32)], axis=1)
    bias = jnp.concatenate([b_router, b_group, jnp.zeros((pad,), F32)]).reshape(1, LANES)
    w_hi = w.astype(BF16)
    w_lo = (w - w_hi.astype(F32)).astype(BF16)
    tm = _pick(t, (256, 128, 8))
    return pl.pallas_call(
        _router_kernel,
        grid=(t // tm,),
        in_specs=[pl.BlockSpec((tm, d), lambda i: (i, 0)),
                  pl.BlockSpec((1, d), lambda i: (0, 0)),
                  pl.BlockSpec((d, LANES), lambda i: (0, 0)),
                  pl.BlockSpec((d, LANES), lambda i: (0, 0)),
                  pl.BlockSpec((1, LANES), lambda i: (0, 0))],
        out_specs=[pl.BlockSpec((tm, d), lambda i: (i, 0)),
                   pl.BlockSpec((tm, LANES), lambda i: (i, 0))],
        out_shape=[jax.ShapeDtypeStruct((t, d), F32), jax.ShapeDtypeStruct((t, LANES), F32)],
        compiler_params=_cparams(("parallel",)),
    )(x, g.reshape(1, d), w_hi, w_lo, bias)


def _moe_kernel(te_ref, r0_ref, nv_ref, nused_ref, tok_ref, dst_ref, h_hbm, wg_ref, wu_ref, wd_ref, y_hbm,
                xbuf0, xbuf1, ybuf0, ybuf1, gsem, ssem, *, tm, n_pairs):
    del te_ref
    i = pl.program_id(0)
    n_valid = nv_ref[i]
    n_used = nused_ref[0]
    xbuf = (xbuf0, xbuf1)
    ybuf = (ybuf0, ybuf1)

    def start_gather(tile, sl):
        base = r0_ref[tile]
        for r in range(tm):
            pltpu.make_async_copy(h_hbm.at[pl.ds(tok_ref[base + r], 1)],
                                  xbuf[sl].at[pl.ds(r, 1)], gsem.at[sl]).start()

    def wait_gather(sl):
        pltpu.make_async_copy(h_hbm.at[pl.ds(0, tm)], xbuf[sl], gsem.at[sl]).wait()

    def start_scatter(tile, sl, valid):
        base = r0_ref[tile]
        spare = n_pairs + sl * tm
        for r in range(tm):
            dst = jnp.where(r < valid, dst_ref[base + r], spare + r)
            pltpu.make_async_copy(ybuf[sl].at[pl.ds(r, 1)], y_hbm.at[pl.ds(dst, 1)], ssem.at[sl]).start()

    def wait_scatter(sl):
        pltpu.make_async_copy(ybuf[sl], y_hbm.at[pl.ds(0, tm)], ssem.at[sl]).wait()

    @pl.when(i == 0)
    def _():
        ybuf1[...] = jnp.zeros(ybuf1.shape, ybuf1.dtype)
        pltpu.make_async_copy(ybuf1, y_hbm.at[pl.ds(n_pairs, tm)], ssem.at[0]).start()
        start_gather(0, 0)

    def used_step(sl):
        wait_gather(sl)
        wait_scatter(sl)
        start_gather(i + 1, 1 - sl)
        prev = jnp.maximum(i - 1, 0)
        start_scatter(prev, 1 - sl, jnp.where(i >= 1, nv_ref[prev], 0))
        x = xbuf[sl][...].astype(BF16)
        a = _dot(x, wg_ref[0].astype(BF16))
        u = _dot(x, wu_ref[0].astype(BF16))
        act = a * jax.nn.sigmoid(a) * u
        ybuf[sl][...] = _dot(act.astype(BF16), wd_ref[0].astype(BF16))

    def flush_step(sl):
        last = i - 1
        wait_gather(sl)
        wait_scatter(sl)
        start_scatter(last, 1 - sl, nv_ref[last])
        wait_scatter(1 - sl)

    for sl in range(2):
        pl.when((n_valid > 0) & (i % 2 == sl))(functools.partial(used_step, sl))
        pl.when((i == n_used) & (i % 2 == sl))(functools.partial(flush_step, sl))


def _lookup(table, idx):
    n = table.shape[0]
    hit = idx[:, None] == jnp.arange(n, dtype=jnp.int32)[None, :]
    return jnp.sum(jnp.where(hit, table[None, :], 0), axis=1)


def _moe(h, info, w_gate, w_up, w_down, layer):
    t, d = h.shape
    n_exp, ff = N_EXPERTS, w_gate.shape[-1]
    tm = MOE_TILE
    pairs = 2 * t
    n_tiles = -(-pairs // tm) + n_exp + 1
    eid = jnp.concatenate([info[:, 0], info[:, 1]]).astype(jnp.int32)
    sorted_eid, order = lax.sort_key_val(eid, jnp.arange(pairs, dtype=jnp.int32))
    bounds = jnp.sum((sorted_eid[None, :] < jnp.arange(n_exp + 1, dtype=jnp.int32)[:, None]).astype(jnp.int32),
                     axis=1)
    starts, counts = bounds[:-1], bounds[1:] - bounds[:-1]
    tiles_e = (counts + tm - 1) // tm
    tile_end = jnp.cumsum(tiles_e)
    n_used = tile_end[-1]
    tile = jnp.arange(n_tiles, dtype=jnp.int32)
    tile_e = jnp.minimum(jnp.sum((tile[:, None] >= tile_end[None, :]).astype(jnp.int32), axis=1), n_exp - 1)
    used = tile < n_used
    k_in_e = tile - _lookup(tile_end - tiles_e, tile_e)
    tile_r0 = jnp.where(used, _lookup(starts, tile_e) + k_in_e * tm, 0).astype(jnp.int32)
    tile_valid = jnp.where(used, jnp.clip(_lookup(counts, tile_e) - k_in_e * tm, 0, tm), 0).astype(jnp.int32)
    last_e = jnp.sum(jnp.where(tile == n_used - 1, tile_e, 0))
    tile_e = jnp.where(used, tile_e, last_e).astype(jnp.int32)
    tok_sorted = jnp.pad(order % t, (0, tm))
    dst_sorted = jnp.pad(order, (0, tm))

    wg = w_gate.reshape((-1,) + w_gate.shape[-2:])
    wu = w_up.reshape((-1,) + w_up.shape[-2:])
    wd = w_down.reshape((-1,) + w_down.shape[-2:])
    e0 = layer * n_exp

    def w_map(i, te, *_):
        return (e0 + te[i], 0, 0)

    grid_spec = pltpu.PrefetchScalarGridSpec(
        num_scalar_prefetch=6,
        grid=(n_tiles,),
        in_specs=[pl.BlockSpec(memory_space=pl.ANY),
                  pl.BlockSpec((1, d, ff), w_map),
                  pl.BlockSpec((1, d, ff), w_map),
                  pl.BlockSpec((1, ff, d), w_map)],
        out_specs=pl.BlockSpec(memory_space=pl.ANY),
        scratch_shapes=[pltpu.VMEM((tm, d), F32)] * 4
                       + [pltpu.SemaphoreType.DMA((2,)), pltpu.SemaphoreType.DMA((2,))],
    )
    return pl.pallas_call(
        functools.partial(_moe_kernel, tm=tm, n_pairs=pairs),
        grid_spec=grid_spec,
        out_shape=jax.ShapeDtypeStruct((pairs + 2 * tm, d), F32),
        compiler_params=_cparams(("arbitrary",)),
    )(tile_e, tile_r0, tile_valid, n_used.reshape(1).astype(jnp.int32), tok_sorted, dst_sorted, h, wg, wu, wd)


def _rope_tables(pos):
    half = ROPE_DIM // 2
    inv_freq = ROPE_THETA ** (-2.0 * jnp.arange(half, dtype=F32) / ROPE_DIM)
    ang = pos.astype(F32)[:, None] * inv_freq[None, :]
    cos, sin = jnp.cos(ang), jnp.sin(ang)
    t = pos.shape[0]
    ones = jnp.ones((t, SWA_HEAD_DIM - ROPE_DIM), F32)
    zeros = jnp.zeros((t, half), F32)
    zrest = jnp.zeros((t, SWA_HEAD_DIM - ROPE_DIM), F32)
    c_head = jnp.concatenate([cos, cos, ones], axis=1)
    plus_head = jnp.concatenate([zeros, sin, zrest], axis=1)
    minus_head = jnp.concatenate([-sin, zeros, zrest], axis=1)
    rep = LANES // SWA_HEAD_DIM
    return tuple(jnp.concatenate([a] * rep, axis=1) for a in (c_head, plus_head, minus_head))


def kernel(x_prompt, x_sample, mem_prompt, cache_swa_k, cache_swa_v, cache_sb_k, cache_sb_v, cache_mem_k, cache_mem_v, page_table, norm_mix, w_in_swa, sinks_swa, w_out_swa, w_in_sb, sb_bias, w_out_sb, norm_mem_q, norm_mem_kv, w_mem_q, w_mem_kv, w_mem_o, norm_ffn, w_group, b_group, w_router, b_router, w_gate, w_up, w_down, norm_final):
    b_p, s_p, d = x_prompt.shape
    b_s, n_s, _ = x_sample.shape
    depth = norm_mix.shape[0]
    t_p, t_s = b_p * s_p, b_s * n_s
    t = t_p + t_s
    past_len = page_table.shape[1] * cache_sb_k.shape[2]
    assert s_p >= WINDOW and s_p % WINDOW == 0

    x = (x_prompt.reshape(t_p, d), x_sample.reshape(t_s, d))
    rope = (s_p, past_len, n_s)
    mem_flat = mem_prompt.reshape(-1, d)
    mem_m = mem_prompt.shape[1]
    mem_w = MEM_HEADS * MEM_HEAD_DIM
    sb_pool_k = cache_sb_k.reshape(-1, cache_sb_k.shape[2] * SB_KV_HEADS, SB_HEAD_DIM)
    sb_pool_v = cache_sb_v.reshape(-1, cache_sb_v.shape[2] * SB_KV_HEADS, SB_HEAD_DIM)
    mem_cache_k = cache_mem_k.reshape(-1, cache_mem_k.shape[2] * MEM_HEADS, MEM_HEAD_DIM)
    mem_cache_v = cache_mem_v.reshape(-1, cache_mem_v.shape[2] * MEM_HEADS, MEM_HEAD_DIM)

    swa_kp, swa_vp, swa_ks, swa_vs = [], [], [], []
    sb_kp, sb_vp, sb_ks, sb_vs = [], [], [], []
    mem_kp, mem_vp = [], []
    moe = None
    for i in range(depth):
        j = i // 2
        first = moe is None
        if i % 2 == 0:
            nq = SWA_HEADS * SWA_HEAD_DIM
            nk = SWA_KV_HEADS * SWA_HEAD_DIM
            res = _norm_matmul(x, norm_mix[i], w_in_swa[j].astype(BF16), moe=moe, rope=rope,
                               rope_cols=nq + nk, emit_x=not first)
            qkv, x = (res, x) if first else res
            o_p = _swa_prompt(qkv, sinks_swa[j], b_p, s_p)
            qs = qkv[t_p:, :nq].reshape(b_s, n_s, SWA_KV_HEADS, SWA_GROUP, SWA_HEAD_DIM)
            qs = qs.transpose(0, 2, 3, 1, 4).reshape(b_s, SWA_KV_HEADS, SWA_GROUP * n_s, 1, SWA_HEAD_DIM)
            own = jnp.eye(SWA_KV_HEADS, dtype=F32)[None, :, None, :, None]
            qs = (qs * own).astype(BF16).reshape(b_s, SWA_HEADS * n_s, nk)
            kn = qkv[t_p:, nq:nq + nk].reshape(b_s, n_s, nk)
            vn = qkv[t_p:, nq + nk:].reshape(b_s, n_s, nk)
            sink_rows = jnp.broadcast_to(jnp.repeat(sinks_swa[j], n_s)[:, None], (SWA_HEADS * n_s, LANES))
            o_s, kbuf, vbuf = _swa_sample(qs, kn, vn, cache_swa_k[j].reshape(b_s, -1, nk),
                                          cache_swa_v[j].reshape(b_s, -1, nk), sink_rows)
            o_s = o_s.reshape(b_s, SWA_KV_HEADS, SWA_GROUP, n_s, SWA_KV_HEADS, SWA_HEAD_DIM)
            o_s = jnp.sum(o_s * own[:, :, :, None].astype(BF16), axis=4)
            o_s = o_s.transpose(0, 3, 1, 2, 4).reshape(t_s, nq)
            last = qkv[:t_p].reshape(b_p, s_p, -1)[:, s_p - WINDOW:, nq:]
            swa_kp.append(last[:, :, :nk].reshape(b_p, WINDOW, SWA_KV_HEADS, SWA_HEAD_DIM))
            swa_vp.append(last[:, :, nk:].reshape(b_p, WINDOW, SWA_KV_HEADS, SWA_HEAD_DIM))
            swa_ks.append(kbuf.reshape(b_s, -1, SWA_KV_HEADS, SWA_HEAD_DIM))
            swa_vs.append(vbuf.reshape(b_s, -1, SWA_KV_HEADS, SWA_HEAD_DIM))
            w_out = w_out_swa[j]
        else:
            nq = SB_HEADS * SB_HEAD_DIM
            nk = SB_KV_HEADS * SB_HEAD_DIM
            res = _norm_matmul(x, norm_mix[i], w_in_sb[j].astype(BF16), moe=moe, emit_x=not first)
            qkv, x = (res, x) if first else res
            o_p = _sb_prompt(qkv, sb_bias[j], b_p, s_p)
            qs = qkv[t_p:, :nq].reshape(b_s, n_s, SB_KV_HEADS, SB_GROUP, SB_HEAD_DIM)
            qs = (qs.transpose(0, 2, 3, 1, 4).reshape(b_s, SB_HEADS * n_s, SB_HEAD_DIM) * SB_SCALE).astype(BF16)
            kn = qkv[t_p:, nq:nq + nk].reshape(b_s, n_s * SB_KV_HEADS, SB_HEAD_DIM)
            vn = qkv[t_p:, nq + nk:].reshape(b_s, n_s * SB_KV_HEADS, SB_HEAD_DIM)
            bias_rows = jnp.broadcast_to(jnp.repeat(sb_bias[j], n_s)[:, None], (SB_HEADS * n_s, LANES))
            o_s = _sb_sample(qs, kn, vn, sb_pool_k, sb_pool_v, page_table, bias_rows,
                             j * cache_sb_k.shape[1])
            o_s = o_s.reshape(b_s, SB_KV_HEADS, SB_GROUP, n_s, SB_HEAD_DIM)
            o_s = o_s.transpose(0, 3, 1, 2, 4).reshape(t_s, nq)
            heads = qkv[:t_p].reshape(b_p, s_p, SB_HEADS + 2 * SB_KV_HEADS, SB_HEAD_DIM)
            sb_kp.append(heads[:, :, SB_HEADS:SB_HEADS + SB_KV_HEADS])
            sb_vp.append(heads[:, :, SB_HEADS + SB_KV_HEADS:])
            sb_ks.append(kn.reshape(b_s, n_s, SB_KV_HEADS, SB_HEAD_DIM))
            sb_vs.append(vn.reshape(b_s, n_s, SB_KV_HEADS, SB_HEAD_DIM))
            w_out = w_out_sb[j]
        x = _resid_matmul((o_p, o_s), w_out.astype(BF16), x)

        mkv = _norm_matmul(mem_flat, norm_mem_kv[i], w_mem_kv[i].astype(BF16))
        mem_kp.append(mkv[:, :mem_w].reshape(b_p, mem_m, MEM_HEADS, MEM_HEAD_DIM))
        mem_vp.append(mkv[:, mem_w:].reshape(b_p, mem_m, MEM_HEADS, MEM_HEAD_DIM))
        qm = _norm_matmul(x, norm_mem_q[i], w_mem_q[i].astype(BF16), out_dtype=BF16)
        om_p = _mem_prompt(qm, mkv, b_p, s_p)
        row_pad = 16 - n_s % 16 if n_s % 16 else 0
        qm_s = jnp.pad(qm[t_p:].reshape(b_s, n_s, MEM_HEADS, MEM_HEAD_DIM), ((0, 0), (0, row_pad), (0, 0), (0, 0)))
        qm_s = qm_s.transpose(0, 2, 1, 3).reshape(b_s, MEM_HEADS * (n_s + row_pad), MEM_HEAD_DIM)
        om_s = _mem_sample(qm_s, mem_cache_k, mem_cache_v, i)
        om_s = om_s.reshape(b_s, MEM_HEADS, n_s + row_pad, MEM_HEAD_DIM)[:, :, :n_s].transpose(0, 2, 1, 3)
        x = _resid_matmul((om_p, om_s.reshape(t_s, mem_w)), w_mem_o[i].astype(BF16), x)

        h, info = _router(x, norm_ffn[i], w_router[i], b_router[i], w_group[i], b_group[i])
        moe = (_moe(h, info, w_gate, w_up, w_down, i), info)

    y_prompt = _final_norm(x, moe[0], moe[1], norm_final, 0, t_p).reshape(b_p, s_p, d)
    y_sample = _final_norm(x, moe[0], moe[1], norm_final, t_p, t_s).reshape(b_s, n_s, d)
    return (y_prompt, y_sample,
            jnp.stack(swa_kp), jnp.stack(swa_vp), jnp.stack(swa_ks), jnp.stack(swa_vs),
            jnp.stack(sb_kp), jnp.stack(sb_vp), jnp.stack(sb_ks), jnp.stack(sb_vs),
            jnp.stack(mem_kp), jnp.stack(mem_vp))
```

```python
import functools
import math

import jax
import jax.numpy as jnp
from jax import lax
from jax.experimental import pallas as pl
from jax.experimental.pallas import tpu as pltpu

F32 = jnp.float32
BF16 = jnp.bfloat16

SWA_HEADS, SWA_KV_HEADS, SWA_HEAD_DIM = 32, 4, 64
SWA_GROUP = SWA_HEADS // SWA_KV_HEADS
WINDOW = 128
ROPE_THETA = 500000.0
ROPE_DIM = SWA_HEAD_DIM // 4
SWA_SCALE = SWA_HEAD_DIM ** -0.5
SB_HEADS, SB_KV_HEADS, SB_HEAD_DIM = 16, 4, 128
SB_GROUP = SB_HEADS // SB_KV_HEADS
SB_BLOCK = 128
SB_SCALE = SB_HEAD_DIM ** -0.5
MEM_HEADS, MEM_HEAD_DIM = 4, 128
MEM_SCALE = MEM_HEAD_DIM ** -0.5
N_GROUPS, EXPERTS_PER_GROUP = 4, 8
N_EXPERTS = N_GROUPS * EXPERTS_PER_GROUP
RMS_EPS = 1e-6
NEG_INF = -1e30

LANES = 128
VMEM_LIMIT = 56 * 1024 * 1024
MOE_TILE = 256
SB_PAGES_PER_STEP = 32
SB_PAGE_CHAIN = 4


def _cparams(sem):
    return pltpu.CompilerParams(dimension_semantics=sem, vmem_limit_bytes=VMEM_LIMIT)


def _pick(n, prefs):
    for p in prefs:
        if n % p == 0:
            return p
    return n


def _dot_t(a, b):
    return lax.dot_general(a, b, (((1,), (1,)), ((), ())), preferred_element_type=F32)


def _dot(a, b):
    return jnp.dot(a, b, preferred_element_type=F32)


def _rows(refs, n_first):
    if len(refs) == 1:
        return refs[0][...]
    return jnp.where(pl.program_id(0) < n_first, refs[0][...], refs[1][...])


def _row_specs(srcs, tm, n_first):
    if not isinstance(srcs, tuple):
        return [pl.BlockSpec((tm, srcs.shape[1]), lambda i, *_: (i, 0))], [srcs]
    width = srcs[0].shape[1]
    return ([pl.BlockSpec((tm, width), lambda i, *_: (jnp.minimum(i, n_first - 1), 0)),
             pl.BlockSpec((tm, width), lambda i, *_: (jnp.maximum(i - n_first, 0), 0))], list(srcs))


def _moe_combine(xv, ya_ref, yb_ref, info_ref):
    info = info_ref[...]
    return xv + ya_ref[...] * info[:, 2:3] + yb_ref[...] * info[:, 3:4]


def _norm_matmul_kernel(*refs, n_x, n_first, add, emit_x, rope_cols, tn):
    it = iter(refs)
    x_refs = [next(it) for _ in range(n_x)]
    ya_ref = next(it) if add else None
    yb_ref = next(it) if add else None
    info_ref = next(it) if add else None
    g_ref = next(it)
    w_ref = next(it)
    if rope_cols:
        cos_ref, sp_ref, sm_ref = next(it), next(it), next(it)
    o_ref = next(it)
    xo_ref = next(it) if emit_x else None
    h_scr = next(it)
    j = pl.program_id(1)

    @pl.when(j == 0)
    def _():
        xv = _rows(x_refs, n_first)
        if add:
            xv = _moe_combine(xv, ya_ref, yb_ref, info_ref)
        if emit_x:
            xo_ref[...] = xv
        ms = jnp.mean(xv * xv, axis=-1, keepdims=True)
        h_scr[...] = (xv * lax.rsqrt(ms + RMS_EPS) * g_ref[...]).astype(BF16)

    acc = _dot(h_scr[...], w_ref[...])
    if not rope_cols:
        o_ref[...] = acc.astype(o_ref.dtype)
        return

    @pl.when(j * tn < rope_cols)
    def _():
        reps = tn // LANES
        cos = jnp.concatenate([cos_ref[...]] * reps, axis=1)
        s_plus = jnp.concatenate([sp_ref[...]] * reps, axis=1)
        s_minus = jnp.concatenate([sm_ref[...]] * reps, axis=1)
        half = ROPE_DIM // 2
        roped = (acc * cos + pltpu.roll(acc, half, axis=1) * s_plus
                 + pltpu.roll(acc, tn - half, axis=1) * s_minus)
        col = j * tn + lax.broadcasted_iota(jnp.int32, acc.shape, 1)
        o_ref[...] = jnp.where(col < rope_cols, roped, acc).astype(o_ref.dtype)

    @pl.when(j * tn >= rope_cols)
    def _():
        o_ref[...] = acc.astype(o_ref.dtype)


def _norm_matmul(x, g, w, *, moe=None, rope=None, rope_cols=0, emit_x=False,
                 out_dtype=F32):
    pair = isinstance(x, tuple)
    t = x[0].shape[0] + x[1].shape[0] if pair else x.shape[0]
    d = x[0].shape[1] if pair else x.shape[1]
    n = w.shape[1]
    add = moe is not None
    tm = _pick(math.gcd(x[0].shape[0], x[1].shape[0]) if pair else t, (512, 256))
    tn = _pick(n, (1024, 512, 256, 128) if add else (1280, 1024, 512, 256, 128))
    n_i = t // tm
    n_first = x[0].shape[0] // tm if pair else n_i
    in_specs, args = _row_specs(x, tm, n_first)
    n_x = len(args)
    if add:
        in_specs += [pl.BlockSpec((tm, d), lambda i, j: (i, 0)),
                     pl.BlockSpec((tm, d), lambda i, j: (i + n_i, 0)),
                     pl.BlockSpec((tm, LANES), lambda i, j: (i, 0))]
        args += [moe[0], moe[0], moe[1]]
    in_specs += [pl.BlockSpec((1, d), lambda i, j: (0, 0)),
                 pl.BlockSpec((d, tn), lambda i, j: (0, j))]
    args += [g.reshape(1, d), w]
    if rope_cols:
        seq, pos0, n_new = rope
        assert pair and seq % tm == 0 and tm % n_new == 0
        period = seq // tm
        pos = jnp.concatenate([jnp.arange(seq), pos0 + jnp.arange(tm) % n_new])
        in_specs += [pl.BlockSpec((tm, LANES), lambda i, j: (jnp.where(i < n_first, i % period, period), 0))] * 3
        args += list(_rope_tables(pos))
    out_shape = [jax.ShapeDtypeStruct((t, n), out_dtype)]
    out_specs = [pl.BlockSpec((tm, tn), lambda i, j: (i, j))]
    if emit_x:
        out_shape.append(jax.ShapeDtypeStruct((t, d), F32))
        out_specs.append(pl.BlockSpec((tm, d), lambda i, j: (i, 0)))
    res = pl.pallas_call(
        functools.partial(_norm_matmul_kernel, n_x=n_x, n_first=n_first,
                          add=add, emit_x=emit_x, rope_cols=rope_cols, tn=tn),
        grid=(n_i, n // tn),
        in_specs=in_specs,
        out_specs=out_specs,
        out_shape=out_shape,
        scratch_shapes=[pltpu.VMEM((tm, d), BF16)],
        compiler_params=_cparams(("parallel", "arbitrary")),
    )(*args)
    return res if emit_x else res[0]


def _resid_matmul_kernel(*refs, n_a, n_r, n_first):
    a_refs, w_ref, r_refs, o_ref = refs[:n_a], refs[n_a], refs[n_a + 1:n_a + 1 + n_r], refs[-1]
    o_ref[...] = _rows(r_refs, n_first) + _dot(_rows(a_refs, n_first), w_ref[...])


def _resid_matmul(a, w, resid):
    first = a[0] if isinstance(a, tuple) else resid[0] if isinstance(resid, tuple) else None
    t = sum(p.shape[0] for p in a) if isinstance(a, tuple) else a.shape[0]
    n = w.shape[1]
    tm = _pick(t if first is None else math.gcd(first.shape[0], t - first.shape[0]), (512, 256))
    n_first = t // tm if first is None else first.shape[0] // tm
    a_specs, a_args = _row_specs(a, tm, n_first)
    r_specs, r_args = _row_specs(resid, tm, n_first)
    return pl.pallas_call(
        functools.partial(_resid_matmul_kernel, n_a=len(a_args), n_r=len(r_args), n_first=n_first),
        grid=(t // tm,),
        in_specs=a_specs + [pl.BlockSpec(w.shape, lambda i: (0, 0))] + r_specs,
        out_specs=pl.BlockSpec((tm, n), lambda i: (i, 0)),
        out_shape=jax.ShapeDtypeStruct((t, n), F32),
        compiler_params=_cparams(("parallel",)),
    )(*a_args, w, *r_args)


def _final_norm_kernel(x_ref, ya_ref, yb_ref, info_ref, g_ref, o_ref):
    xv = _moe_combine(x_ref[...], ya_ref, yb_ref, info_ref)
    ms = jnp.mean(xv * xv, axis=-1, keepdims=True)
    o_ref[...] = xv * lax.rsqrt(ms + RMS_EPS) * g_ref[...]


def _final_norm(x, y2, info, g, row0, n_rows):
    t, d = x.shape
    tm = _pick(n_rows, (256, 128, 8))
    assert row0 % tm == 0 and t % tm == 0
    b0, bt = row0 // tm, t // tm
    return pl.pallas_call(
        _final_norm_kernel,
        grid=(n_rows // tm,),
        in_specs=[pl.BlockSpec((tm, d), lambda i: (i + b0, 0)),
                  pl.BlockSpec((tm, d), lambda i: (i + b0, 0)),
                  pl.BlockSpec((tm, d), lambda i: (i + b0 + bt, 0)),
                  pl.BlockSpec((tm, LANES), lambda i: (i + b0, 0)),
                  pl.BlockSpec((1, d), lambda i: (0, 0))],
        out_specs=pl.BlockSpec((tm, d), lambda i: (i, 0)),
        out_shape=jax.ShapeDtypeStruct((n_rows, d), F32),
        compiler_params=_cparams(("parallel",)),
    )(x, y2, y2, info, g.reshape(1, d))


def _swa_prompt_kernel(sink_ref, q_ref, kp_ref, kc_ref, vp_ref, vc_ref, o_ref):
    n = pl.program_id(1)
    blk = WINDOW
    qi = lax.broadcasted_iota(jnp.int32, (blk, 2 * blk), 0)
    sj = lax.broadcasted_iota(jnp.int32, (blk, 2 * blk), 1)
    rel = qi + blk - sj
    first_key = jnp.where(n > 0, 0, blk)
    valid = (rel >= 0) & (rel < WINDOW) & (sj >= first_key)
    k = jnp.concatenate([kp_ref[...], kc_ref[...]], axis=0).astype(BF16)
    v = jnp.concatenate([vp_ref[...], vc_ref[...]], axis=0).astype(BF16)
    hd = SWA_HEAD_DIM
    for kh in range(SWA_KV_HEADS):
        k_h = k[:, kh * hd:(kh + 1) * hd]
        v_h = v[:, kh * hd:(kh + 1) * hd]
        for g2 in range(SWA_GROUP // 2):
            pair = []
            for gg in range(2):
                h = kh * SWA_GROUP + g2 * 2 + gg
                q_h = q_ref[:, h * hd:(h + 1) * hd].astype(BF16)
                s = _dot_t(q_h, k_h) * SWA_SCALE
                s = jnp.where(valid, s, NEG_INF)
                sink = sink_ref[h]
                m = jnp.maximum(jnp.max(s, axis=-1, keepdims=True), sink)
                p = jnp.exp(s - m)
                denom = jnp.sum(p, axis=-1, keepdims=True) + jnp.exp(sink - m)
                pair.append(_dot(p.astype(BF16), v_h) / denom)
            h0 = kh * SWA_GROUP + g2 * 2
            o_ref[:, h0 * hd:(h0 + 2) * hd] = jnp.concatenate(pair, axis=1).astype(o_ref.dtype)


def _swa_prompt(qkv, sinks, batch, seq):
    nq = SWA_HEADS * SWA_HEAD_DIM
    nk = SWA_KV_HEADS * SWA_HEAD_DIM
    nb = seq // WINDOW
    kcol, vcol = nq // nk, nq // nk + 1

    def prev(b, n):
        return b * nb + jnp.maximum(n - 1, 0)

    return pl.pallas_call(
        _swa_prompt_kernel,
        grid=(batch, nb),
        in_specs=[pl.BlockSpec(memory_space=pltpu.SMEM),
                  pl.BlockSpec((WINDOW, nq), lambda b, n: (b * nb + n, 0)),
                  pl.BlockSpec((WINDOW, nk), lambda b, n: (prev(b, n), kcol)),
                  pl.BlockSpec((WINDOW, nk), lambda b, n: (b * nb + n, kcol)),
                  pl.BlockSpec((WINDOW, nk), lambda b, n: (prev(b, n), vcol)),
                  pl.BlockSpec((WINDOW, nk), lambda b, n: (b * nb + n, vcol))],
        out_specs=pl.BlockSpec((WINDOW, nq), lambda b, n: (b * nb + n, 0)),
        out_shape=jax.ShapeDtypeStruct((batch * seq, nq), BF16),
        compiler_params=_cparams(("parallel", "arbitrary")),
    )(sinks, qkv, qkv, qkv, qkv, qkv)


def _swa_sample_kernel(q_ref, kn_ref, vn_ref, ck_ref, cv_ref, sink_ref, o_ref, nk_ref, nv_ref,
                       kall, vall, *, n_new, seqs):
    w = WINDOW
    rows = q_ref.shape[1]
    zeros = jnp.zeros((w, kall.shape[1]), F32)
    kall[w:2 * w, :] = zeros
    vall[w:2 * w, :] = zeros
    r = lax.broadcasted_iota(jnp.int32, (rows, 2 * w), 0)
    j = lax.broadcasted_iota(jnp.int32, (rows, 2 * w), 1)
    qn = r % n_new
    valid = ((j < w) & (j > qn)) | ((j >= w) & (j - w <= qn))
    sink = sink_ref[:, 0:1]

    def body(s, carry):
        kall[0:w, :] = ck_ref[s]
        vall[0:w, :] = cv_ref[s]
        kall[w:w + n_new, :] = kn_ref[s]
        vall[w:w + n_new, :] = vn_ref[s]
        nk_ref[s] = kall[n_new:n_new + w, :]
        nv_ref[s] = vall[n_new:n_new + w, :]
        lg = _dot_t(q_ref[s], kall[...].astype(BF16)) * SWA_SCALE
        lg = jnp.where(valid, lg, NEG_INF)
        m = jnp.maximum(jnp.max(lg, axis=-1, keepdims=True), sink)
        p = jnp.exp(lg - m)
        denom = jnp.sum(p, axis=-1, keepdims=True) + jnp.exp(sink - m)
        o_ref[s] = (_dot(p.astype(BF16), vall[...].astype(BF16)) / denom).astype(o_ref.dtype)
        return carry

    lax.fori_loop(0, seqs, body, 0)


def _swa_sample(q, k_new, v_new, cache_k, cache_v, sink_rows):
    b, rows, _ = q.shape
    n_new = k_new.shape[1]
    w, kw = cache_k.shape[1], cache_k.shape[2]
    assert w == WINDOW
    seqs = _pick(b, (8, 4, 2, 1))
    return pl.pallas_call(
        functools.partial(_swa_sample_kernel, n_new=n_new, seqs=seqs),
        grid=(b // seqs,),
        in_specs=[pl.BlockSpec((seqs, rows, kw), lambda i: (i, 0, 0)),
                  pl.BlockSpec((seqs, n_new, kw), lambda i: (i, 0, 0)),
                  pl.BlockSpec((seqs, n_new, kw), lambda i: (i, 0, 0)),
                  pl.BlockSpec((seqs, w, kw), lambda i: (i, 0, 0)),
                  pl.BlockSpec((seqs, w, kw), lambda i: (i, 0, 0)),
                  pl.BlockSpec((rows, LANES), lambda i: (0, 0))],
        out_specs=[pl.BlockSpec((seqs, rows, kw), lambda i: (i, 0, 0)),
                   pl.BlockSpec((seqs, w, kw), lambda i: (i, 0, 0)),
                   pl.BlockSpec((seqs, w, kw), lambda i: (i, 0, 0))],
        out_shape=[jax.ShapeDtypeStruct((b, rows, kw), BF16),
                   jax.ShapeDtypeStruct((b, w, kw), F32),
                   jax.ShapeDtypeStruct((b, w, kw), F32)],
        scratch_shapes=[pltpu.VMEM((2 * w, kw), F32), pltpu.VMEM((2 * w, kw), F32)],
        compiler_params=_cparams(("parallel",)),
    )(q, k_new, v_new, cache_k, cache_v, sink_rows)


def _mem_heads(q, k_of, v_of):
    hd = MEM_HEAD_DIM
    outs = []
    for h in range(MEM_HEADS):
        s = _dot_t(q[:, h * hd:(h + 1) * hd], k_of(h)) * MEM_SCALE
        m = jnp.max(s, axis=-1, keepdims=True)
        p = jnp.exp(s - m)
        denom = jnp.sum(p, axis=-1, keepdims=True)
        outs.append(_dot(p.astype(BF16), v_of(h)) / denom)
    return jnp.concatenate(outs, axis=1)


def _mem_prompt_kernel(q_ref, k_ref, v_ref, o_ref):
    hd = MEM_HEAD_DIM
    k = k_ref[...].astype(BF16)
    v = v_ref[...].astype(BF16)
    o_ref[...] = _mem_heads(q_ref[...], lambda h: k[:, h * hd:(h + 1) * hd],
                            lambda h: v[:, h * hd:(h + 1) * hd]).astype(o_ref.dtype)


def _mem_prompt(q, kv, batch, seq):
    width = MEM_HEADS * MEM_HEAD_DIM
    m = kv.shape[0] // batch
    tq = _pick(seq, (512, 256, 128))
    nq = seq // tq
    return pl.pallas_call(
        _mem_prompt_kernel,
        grid=(batch, nq),
        in_specs=[pl.BlockSpec((tq, width), lambda b, i: (b * nq + i, 0)),
                  pl.BlockSpec((m, width), lambda b, i: (b, 0)),
                  pl.BlockSpec((m, width), lambda b, i: (b, 1))],
        out_specs=pl.BlockSpec((tq, width), lambda b, i: (b * nq + i, 0)),
        out_shape=jax.ShapeDtypeStruct((batch * seq, width), BF16),
        compiler_params=_cparams(("parallel", "arbitrary")),
    )(q, kv, kv)


def _mem_sample_kernel(q_ref, k_ref, v_ref, o_ref, *, seqs):
    nh = MEM_HEADS
    qrows, cols = q_ref.shape[1], k_ref.shape[1]
    row_h = lax.broadcasted_iota(jnp.int32, (qrows, cols), 0) // (qrows // nh)
    col_h = lax.broadcasted_iota(jnp.int32, (qrows, cols), 1) % nh
    own = row_h == col_h

    def body(s, carry):
        lg = _dot_t(q_ref[s], k_ref[s].astype(BF16)) * MEM_SCALE
        lg = jnp.where(own, lg, NEG_INF)
        mx = jnp.max(lg, axis=-1, keepdims=True)
        p = jnp.exp(lg - mx)
        denom = jnp.sum(p, axis=-1, keepdims=True)
        o_ref[s] = (_dot(p.astype(BF16), v_ref[s].astype(BF16)) / denom).astype(o_ref.dtype)
        return carry

    lax.fori_loop(0, seqs, body, 0)


def _mem_sample(q, k, v, layer):
    b, qrows, hd = q.shape
    mh = k.shape[1]
    seqs = _pick(b, (8, 4, 2, 1))
    nb = b // seqs
    return pl.pallas_call(
        functools.partial(_mem_sample_kernel, seqs=seqs),
        grid=(nb,),
        in_specs=[pl.BlockSpec((seqs, qrows, hd), lambda i: (i, 0, 0)),
                  pl.BlockSpec((seqs, mh, hd), lambda i: (layer * nb + i, 0, 0)),
                  pl.BlockSpec((seqs, mh, hd), lambda i: (layer * nb + i, 0, 0))],
        out_specs=pl.BlockSpec((seqs, qrows, hd), lambda i: (i, 0, 0)),
        out_shape=jax.ShapeDtypeStruct((b, qrows, hd), BF16),
        compiler_params=_cparams(("parallel",)),
    )(q, k, v)


def _sb_local(z, mask, tri2):
    sp = jnp.maximum(z, 0.0) + jnp.log(1.0 + jnp.exp(-jnp.abs(z)))
    spm = sp if mask is None else jnp.where(mask, sp, 0.0)
    cum = _dot(spm.astype(BF16), tri2)
    kb = z.shape[1]
    a = z - sp - cum[:, :kb]
    if mask is not None:
        a = jnp.where(mask, a, NEG_INF)
    return a, cum[:, kb:]


def _tri2(kb):
    r = lax.broadcasted_iota(jnp.int32, (kb, kb + LANES), 0)
    cidx = lax.broadcasted_iota(jnp.int32, (kb, kb + LANES), 1)
    return jnp.where((cidx >= kb) | (r > cidx), 1.0, 0.0).astype(BF16)


def _sb_prompt_kernel(bias_ref, q_ref, k_ref, v_ref, o_ref, kb_scr, vb_scr, c_scr, acc_scr):
    kh = pl.program_id(1)
    n = pl.program_id(2)
    blk = SB_BLOCK
    hd = SB_HEAD_DIM

    @pl.when(n == 0)
    def _():
        kb_scr[...] = k_ref[...].astype(BF16)
        vb_scr[...] = v_ref[...].astype(BF16)

    q = q_ref[...]
    qs = (jnp.concatenate([q[:, g * hd:(g + 1) * hd] for g in range(SB_GROUP)], axis=0) * SB_SCALE).astype(BF16)
    bias = jnp.concatenate(
        [jnp.full((blk, LANES), bias_ref[kh * SB_GROUP + g], F32) for g in range(SB_GROUP)], axis=0)
    tri2 = _tri2(blk)
    rows = SB_GROUP * blk
    qi = lax.broadcasted_iota(jnp.int32, (rows, blk), 0) % blk
    sj = lax.broadcasted_iota(jnp.int32, (rows, blk), 1)
    diag_mask = sj < qi

    def local(kblk, mask):
        start = pl.multiple_of(kblk * blk, blk)
        z = _dot_t(qs, kb_scr[pl.ds(start, blk), :]) + bias
        return _sb_local(z, mask, tri2)

    def accumulate(kblk, a, tot):
        start = pl.multiple_of(kblk * blk, blk)
        wgt = jnp.exp(a + c_scr[...])
        acc_scr[...] += _dot(wgt.astype(BF16), vb_scr[pl.ds(start, blk), :])
        c_scr[...] -= tot

    c_scr[...] = jnp.zeros_like(c_scr)
    acc_scr[...] = jnp.zeros_like(acc_scr)
    accumulate(n, *local(n, diag_mask))

    def wide(first_blk, width):
        start = pl.multiple_of(first_blk * blk, blk)
        z = _dot_t(qs, kb_scr[pl.ds(start, width * blk), :]) + jnp.concatenate([bias] * width, axis=1)
        a, tot = _sb_local(z, None, _tri2(width * blk))
        c = c_scr[...]
        wgt = jnp.exp(a + jnp.concatenate([c] * width, axis=1))
        acc_scr[...] += _dot(wgt.astype(BF16), vb_scr[pl.ds(start, width * blk), :])
        c_scr[...] = c - tot

    def body(i, carry):
        wide(n - 4 - 4 * i, 4)
        return carry

    lax.fori_loop(0, n // 4, body, 0)

    @pl.when(n % 4 >= 2)
    def _():
        wide(n % 4 - 2, 2)

    @pl.when(n % 2 == 1)
    def _():
        accumulate(0, *local(0, None))

    acc = acc_scr[...]
    o_ref[...] = jnp.concatenate([acc[g * blk:(g + 1) * blk] for g in range(SB_GROUP)],
                                 axis=1).astype(o_ref.dtype)


def _sb_prompt(qkv, bias, batch, seq):
    hd = SB_HEAD_DIM
    nb = seq // SB_BLOCK
    gw = SB_GROUP * hd
    kcol0 = SB_HEADS
    vcol0 = SB_HEADS + SB_KV_HEADS
    return pl.pallas_call(
        _sb_prompt_kernel,
        grid=(batch, SB_KV_HEADS, nb),
        in_specs=[pl.BlockSpec(memory_space=pltpu.SMEM),
                  pl.BlockSpec((SB_BLOCK, gw), lambda b, kh, n: (b * nb + n, kh)),
                  pl.BlockSpec((seq, hd), lambda b, kh, n: (b, kcol0 + kh)),
                  pl.BlockSpec((seq, hd), lambda b, kh, n: (b, vcol0 + kh))],
        out_specs=pl.BlockSpec((SB_BLOCK, gw), lambda b, kh, n: (b * nb + n, kh)),
        out_shape=jax.ShapeDtypeStruct((batch * seq, SB_HEADS * hd), BF16),
        scratch_shapes=[pltpu.VMEM((seq, hd), BF16), pltpu.VMEM((seq, hd), BF16),
                        pltpu.VMEM((SB_GROUP * SB_BLOCK, LANES), F32),
                        pltpu.VMEM((SB_GROUP * SB_BLOCK, hd), F32)],
        compiler_params=_cparams(("parallel", "parallel", "arbitrary")),
    )(bias, qkv, qkv, qkv)


def _sb_sample_kernel(pt_ref, q_ref, kn_ref, vn_ref, bias_ref, kpool, vpool, o_ref,
                      kbuf, vbuf, kpad, vpad, c_scr, acc_scr, ksem, vsem, *, n_new, pps, n_pages, page0):
    seq = pl.program_id(0)
    step = pl.program_id(1)
    steps = pl.num_programs(1)
    gstep = seq * steps + step
    slot = gstep % 2
    nkv = SB_KV_HEADS
    page = SB_BLOCK
    rows = SB_GROUP * n_new
    qr = nkv * rows
    tri2 = _tri2(page)
    q = q_ref[0]
    bias = bias_ref[...]

    def start_fetch(sq, chunk, sl):
        for i in range(pps):
            pg = page0 + pt_ref[sq * n_pages + (n_pages - 1 - (chunk * pps + i))]
            pltpu.make_async_copy(kpool.at[pg], kbuf.at[sl, i], ksem.at[sl]).start()
            pltpu.make_async_copy(vpool.at[pg], vbuf.at[sl, i], vsem.at[sl]).start()

    def wait_fetch(sl):
        pltpu.make_async_copy(kpool.at[pl.ds(0, pps)], kbuf.at[sl], ksem.at[sl]).wait()
        pltpu.make_async_copy(vpool.at[pl.ds(0, pps)], vbuf.at[sl], vsem.at[sl]).wait()

    @pl.when(gstep == 0)
    def _():
        start_fetch(0, 0, 0)

    @pl.when(gstep + 1 < pl.num_programs(0) * steps)
    def _():
        wrap = step + 1 == steps
        start_fetch(jnp.where(wrap, seq + 1, seq), jnp.where(wrap, 0, step + 1), 1 - slot)

    def head(ref, kh):
        return ref[pl.ds(kh, page, stride=nkv), :].astype(BF16)

    def logits(page_refs):
        z = jnp.concatenate(
            [_dot_t(q[kh * rows:(kh + 1) * rows], head(k_ref, kh))
             for k_ref, _ in page_refs for kh in range(nkv)], axis=0)
        return z + jnp.concatenate([bias] * len(page_refs), axis=0)

    def weights(z, mask):
        a, tot = _sb_local(z, mask, tri2)
        return jnp.exp(a).astype(BF16), tot

    def outputs(page_refs, wb, tot):
        outs = [jnp.concatenate(
            [_dot(wb[p * qr + kh * rows:p * qr + (kh + 1) * rows], head(v_ref, kh))
             for kh in range(nkv)], axis=0) for p, (_, v_ref) in enumerate(page_refs)]
        return outs, [tot[p * qr:(p + 1) * qr] for p in range(len(page_refs))]

    def local(groups, mask):
        zs, ws, parts = {}, {}, []
        for t in range(len(groups) + 2):
            if t < len(groups):
                zs[t] = logits(groups[t])
            if 0 <= t - 1 < len(groups):
                ws[t - 1] = weights(zs.pop(t - 1), mask)
            if 0 <= t - 2 < len(groups):
                parts.append(outputs(groups[t - 2], *ws.pop(t - 2)))
        return parts

    def combine(parts):
        off = c_scr[...]
        acc = acc_scr[...]
        for outs, tots in parts:
            for o_p, t_p in zip(outs, tots):
                acc = acc + jnp.exp(off) * o_p
                off = off - t_p
        acc_scr[...] = acc
        c_scr[...] = off

    @pl.when(step == 0)
    def _():
        c_scr[...] = jnp.zeros_like(c_scr)
        acc_scr[...] = jnp.zeros_like(acc_scr)
        kpad[...] = jnp.zeros_like(kpad)
        vpad[...] = jnp.zeros_like(vpad)
        kpad[0:nkv * n_new, :] = kn_ref[0]
        vpad[0:nkv * n_new, :] = vn_ref[0]
        r = lax.broadcasted_iota(jnp.int32, (qr, page), 0)
        j = lax.broadcasted_iota(jnp.int32, (qr, page), 1)
        combine(local([[(kpad, vpad)]], j < (r % n_new)))

    wait_fetch(slot)
    pages = [(kbuf.at[slot, i], vbuf.at[slot, i]) for i in range(pps)]
    chain = min(SB_PAGE_CHAIN, pps)
    combine(local([pages[c:c + chain] for c in range(0, pps, chain)], None))

    @pl.when(step == steps - 1)
    def _():
        o_ref[0] = acc_scr[...].astype(o_ref.dtype)


def _sb_sample(q, k_new, v_new, k_pool, v_pool, page_table, bias_rows, page0):
    b, qrows, hd = q.shape
    new_rows = k_new.shape[1]
    n_new = new_rows // SB_KV_HEADS
    n_pages = page_table.shape[1]
    prow = k_pool.shape[1]
    assert prow == SB_BLOCK * SB_KV_HEADS
    pps = _pick(n_pages, (SB_PAGES_PER_STEP, 8, 4, 2, 1))
    grid_spec = pltpu.PrefetchScalarGridSpec(
        num_scalar_prefetch=1,
        grid=(b, n_pages // pps),
        in_specs=[pl.BlockSpec((1, qrows, hd), lambda s, c, pt: (s, 0, 0)),
                  pl.BlockSpec((1, new_rows, hd), lambda s, c, pt: (s, 0, 0)),
                  pl.BlockSpec((1, new_rows, hd), lambda s, c, pt: (s, 0, 0)),
                  pl.BlockSpec((qrows, LANES), lambda s, c, pt: (0, 0)),
                  pl.BlockSpec(memory_space=pl.ANY),
                  pl.BlockSpec(memory_space=pl.ANY)],
        out_specs=pl.BlockSpec((1, qrows, hd), lambda s, c, pt: (s, 0, 0)),
        scratch_shapes=[pltpu.VMEM((2, pps, prow, hd), F32), pltpu.VMEM((2, pps, prow, hd), F32),
                        pltpu.VMEM((prow, hd), F32), pltpu.VMEM((prow, hd), F32),
                        pltpu.VMEM((qrows, LANES), F32), pltpu.VMEM((qrows, hd), F32),
                        pltpu.SemaphoreType.DMA((2,)), pltpu.SemaphoreType.DMA((2,))],
    )
    return pl.pallas_call(
        functools.partial(_sb_sample_kernel, n_new=n_new, pps=pps, n_pages=n_pages, page0=page0),
        grid_spec=grid_spec,
        out_shape=jax.ShapeDtypeStruct((b, qrows, hd), BF16),
        compiler_params=_cparams(("arbitrary", "arbitrary")),
    )(page_table.reshape(-1), q, k_new, v_new, bias_rows, k_pool, v_pool)


def _router_kernel(x_ref, g_ref, wh_ref, wl_ref, b_ref, h_ref, info_ref):
    xv = x_ref[...]
    ms = jnp.mean(xv * xv, axis=-1, keepdims=True)
    h = xv * lax.rsqrt(ms + RMS_EPS) * g_ref[...]
    h_ref[...] = h
    h_hi = h.astype(BF16)
    h_lo = (h - h_hi.astype(F32)).astype(BF16)
    logits = _dot(h_hi, wh_ref[...]) + _dot(h_hi, wl_ref[...]) + _dot(h_lo, wh_ref[...]) + b_ref[...]
    lane = lax.broadcasted_iota(jnp.int32, logits.shape, 1)
    big = jnp.int32(1 << 20)
    is_g = (lane >= N_EXPERTS) & (lane < N_EXPERTS + N_GROUPS)
    gl = jnp.where(is_g, logits, -jnp.inf)
    gmax = jnp.max(gl, axis=-1, keepdims=True)
    gidx = jnp.min(jnp.where(gl == gmax, lane - N_EXPERTS, big), axis=-1, keepdims=True)
    p_group = 1.0 / jnp.sum(jnp.exp(gl - gmax), axis=-1, keepdims=True)
    lo = gidx * EXPERTS_PER_GROUP
    in_grp = (lane >= lo) & (lane < lo + EXPERTS_PER_GROUP)
    el = jnp.where(in_grp, logits, -jnp.inf)
    emax = jnp.max(el, axis=-1, keepdims=True)
    pe = jnp.exp(el - emax)
    prob = pe / jnp.sum(pe, axis=-1, keepdims=True)
    prob = jnp.where(in_grp, prob, -1.0)
    p1 = jnp.max(prob, axis=-1, keepdims=True)
    i1 = jnp.min(jnp.where(prob == p1, lane, big), axis=-1, keepdims=True)
    rest = jnp.where(lane == i1, -1.0, prob)
    p2 = jnp.max(rest, axis=-1, keepdims=True)
    i2 = jnp.min(jnp.where(rest == p2, lane, big), axis=-1, keepdims=True)
    tot = p1 + p2
    w1 = p1 / tot * p_group
    w2 = p2 / tot * p_group
    info = jnp.where(lane == 0, i1.astype(F32),
                     jnp.where(lane == 1, i2.astype(F32),
                               jnp.where(lane == 2, w1, jnp.where(lane == 3, w2, 0.0))))
    info_ref[...] = info


def _router(x, g, w_router, b_router, w_group, b_group):
    t, d = x.shape
    pad = LANES - N_EXPERTS - N_GROUPS
    w = jnp.concatenate([w_router, w_group, jnp.zeros((d, pad), F32)], axis=1)
    bias = jnp.concatenate([b_router, b_group, jnp.zeros((pad,), F32)]).reshape(1, LANES)
    w_hi = w.astype(BF16)
    w_lo = (w - w_hi.astype(F32)).astype(BF16)
    tm = _pick(t, (256, 128, 8))
    return pl.pallas_call(
        _router_kernel,
        grid=(t // tm,),
        in_specs=[pl.BlockSpec((tm, d), lambda i: (i, 0)),
                  pl.BlockSpec((1, d), lambda i: (0, 0)),
                  pl.BlockSpec((d, LANES), lambda i: (0, 0)),
                  pl.BlockSpec((d, LANES), lambda i: (0, 0)),
                  pl.BlockSpec((1, LANES), lambda i: (0, 0))],
        out_specs=[pl.BlockSpec((tm, d), lambda i: (i, 0)),
                   pl.BlockSpec((tm, LANES), lambda i: (i, 0))],
        out_shape=[jax.ShapeDtypeStruct((t, d), F32), jax.ShapeDtypeStruct((t, LANES), F32)],
        compiler_params=_cparams(("parallel",)),
    )(x, g.reshape(1, d), w_hi, w_lo, bias)


def _moe_kernel(te_ref, r0_ref, nv_ref, nused_ref, tok_ref, dst_ref, h_hbm, wg_ref, wu_ref, wd_ref, y_hbm,
                xbuf0, xbuf1, ybuf0, ybuf1, gsem, ssem, *, tm, n_pairs):
    del te_ref
    i = pl.program_id(0)
    n_valid = nv_ref[i]
    n_used = nused_ref[0]
    xbuf = (xbuf0, xbuf1)
    ybuf = (ybuf0, ybuf1)

    def start_gather(tile, sl):
        base = r0_ref[tile]
        for r in range(tm):
            pltpu.make_async_copy(h_hbm.at[pl.ds(tok_ref[base + r], 1)],
                                  xbuf[sl].at[pl.ds(r, 1)], gsem.at[sl]).start()

    def wait_gather(sl):
        pltpu.make_async_copy(h_hbm.at[pl.ds(0, tm)], xbuf[sl], gsem.at[sl]).wait()

    def start_scatter(tile, sl, valid):
        base = r0_ref[tile]
        spare = n_pairs + sl * tm
        for r in range(tm):
            dst = jnp.where(r < valid, dst_ref[base + r], spare + r)
            pltpu.make_async_copy(ybuf[sl].at[pl.ds(r, 1)], y_hbm.at[pl.ds(dst, 1)], ssem.at[sl]).start()

    def wait_scatter(sl):
        pltpu.make_async_copy(ybuf[sl], y_hbm.at[pl.ds(0, tm)], ssem.at[sl]).wait()

    @pl.when(i == 0)
    def _():
        ybuf1[...] = jnp.zeros(ybuf1.shape, ybuf1.dtype)
        pltpu.make_async_copy(ybuf1, y_hbm.at[pl.ds(n_pairs, tm)], ssem.at[0]).start()
        start_gather(0, 0)

    def used_step(sl):
        wait_gather(sl)
        wait_scatter(sl)
        start_gather(i + 1, 1 - sl)
        prev = jnp.maximum(i - 1, 0)
        start_scatter(prev, 1 - sl, jnp.where(i >= 1, nv_ref[prev], 0))
        x = xbuf[sl][...].astype(BF16)
        a = _dot(x, wg_ref[0].astype(BF16))
        u = _dot(x, wu_ref[0].astype(BF16))
        act = a * jax.nn.sigmoid(a) * u
        ybuf[sl][...] = _dot(act.astype(BF16), wd_ref[0].astype(BF16))

    def flush_step(sl):
        last = i - 1
        wait_gather(sl)
        wait_scatter(sl)
        start_scatter(last, 1 - sl, nv_ref[last])
        wait_scatter(1 - sl)

    for sl in range(2):
        pl.when((n_valid > 0) & (i % 2 == sl))(functools.partial(used_step, sl))
        pl.when((i == n_used) & (i % 2 == sl))(functools.partial(flush_step, sl))


def _lookup(table, idx):
    n = table.shape[0]
    hit = idx[:, None] == jnp.arange(n, dtype=jnp.int32)[None, :]
    return jnp.sum(jnp.where(hit, table[None, :], 0), axis=1)


def _moe(h, info, w_gate, w_up, w_down, layer):
    t, d = h.shape
    n_exp, ff = N_EXPERTS, w_gate.shape[-1]
    tm = MOE_TILE
    pairs = 2 * t
    n_tiles = -(-pairs // tm) + n_exp + 1
    eid = jnp.concatenate([info[:, 0], info[:, 1]]).astype(jnp.int32)
    sorted_eid, order = lax.sort_key_val(eid, jnp.arange(pairs, dtype=jnp.int32))
    bounds = jnp.sum((sorted_eid[None, :] < jnp.arange(n_exp + 1, dtype=jnp.int32)[:, None]).astype(jnp.int32),
                     axis=1)
    starts, counts = bounds[:-1], bounds[1:] - bounds[:-1]
    tiles_e = (counts + tm - 1) // tm
    tile_end = jnp.cumsum(tiles_e)
    n_used = tile_end[-1]
    tile = jnp.arange(n_tiles, dtype=jnp.int32)
    tile_e = jnp.minimum(jnp.sum((tile[:, None] >= tile_end[None, :]).astype(jnp.int32), axis=1), n_exp - 1)
    used = tile < n_used
    k_in_e = tile - _lookup(tile_end - tiles_e, tile_e)
    tile_r0 = jnp.where(used, _lookup(starts, tile_e) + k_in_e * tm, 0).astype(jnp.int32)
    tile_valid = jnp.where(used, jnp.clip(_lookup(counts, tile_e) - k_in_e * tm, 0, tm), 0).astype(jnp.int32)
    last_e = jnp.sum(jnp.where(tile == n_used - 1, tile_e, 0))
    tile_e = jnp.where(used, tile_e, last_e).astype(jnp.int32)
    tok_sorted = jnp.pad(order % t, (0, tm))
    dst_sorted = jnp.pad(order, (0, tm))

    wg = w_gate.reshape((-1,) + w_gate.shape[-2:])
    wu = w_up.reshape((-1,) + w_up.shape[-2:])
    wd = w_down.reshape((-1,) + w_down.shape[-2:])
    e0 = layer * n_exp

    def w_map(i, te, *_):
        return (e0 + te[i], 0, 0)

    grid_spec = pltpu.PrefetchScalarGridSpec(
        num_scalar_prefetch=6,
        grid=(n_tiles,),
        in_specs=[pl.BlockSpec(memory_space=pl.ANY),
                  pl.BlockSpec((1, d, ff), w_map),
                  pl.BlockSpec((1, d, ff), w_map),
                  pl.BlockSpec((1, ff, d), w_map)],
        out_specs=pl.BlockSpec(memory_space=pl.ANY),
        scratch_shapes=[pltpu.VMEM((tm, d), F32)] * 4
                       + [pltpu.SemaphoreType.DMA((2,)), pltpu.SemaphoreType.DMA((2,))],
    )
    return pl.pallas_call(
        functools.partial(_moe_kernel, tm=tm, n_pairs=pairs),
        grid_spec=grid_spec,
        out_shape=jax.ShapeDtypeStruct((pairs + 2 * tm, d), F32),
        compiler_params=_cparams(("arbitrary",)),
    )(tile_e, tile_r0, tile_valid, n_used.reshape(1).astype(jnp.int32), tok_sorted, dst_sorted, h, wg, wu, wd)


def _rope_tables(pos):
    half = ROPE_DIM // 2
    inv_freq = ROPE_THETA ** (-2.0 * jnp.arange(half, dtype=F32) / ROPE_DIM)
    ang = pos.astype(F32)[:, None] * inv_freq[None, :]
    cos, sin = jnp.cos(ang), jnp.sin(ang)
    t = pos.shape[0]
    ones = jnp.ones((t, SWA_HEAD_DIM - ROPE_DIM), F32)
    zeros = jnp.zeros((t, half), F32)
    zrest = jnp.zeros((t, SWA_HEAD_DIM - ROPE_DIM), F32)
    c_head = jnp.concatenate([cos, cos, ones], axis=1)
    plus_head = jnp.concatenate([zeros, sin, zrest], axis=1)
    minus_head = jnp.concatenate([-sin, zeros, zrest], axis=1)
    rep = LANES // SWA_HEAD_DIM
    return tuple(jnp.concatenate([a] * rep, axis=1) for a in (c_head, plus_head, minus_head))


def kernel(x_prompt, x_sample, mem_prompt, cache_swa_k, cache_swa_v, cache_sb_k, cache_sb_v, cache_mem_k, cache_mem_v, page_table, norm_mix, w_in_swa, sinks_swa, w_out_swa, w_in_sb, sb_bias, w_out_sb, norm_mem_q, norm_mem_kv, w_mem_q, w_mem_kv, w_mem_o, norm_ffn, w_group, b_group, w_router, b_router, w_gate, w_up, w_down, norm_final):
    b_p, s_p, d = x_prompt.shape
    b_s, n_s, _ = x_sample.shape
    depth = norm_mix.shape[0]
    t_p, t_s = b_p * s_p, b_s * n_s
    t = t_p + t_s
    past_len = page_table.shape[1] * cache_sb_k.shape[2]
    assert s_p >= WINDOW and s_p % WINDOW == 0

    x = (x_prompt.reshape(t_p, d), x_sample.reshape(t_s, d))
    rope = (s_p, past_len, n_s)
    mem_flat = mem_prompt.reshape(-1, d)
    mem_m = mem_prompt.shape[1]
    mem_w = MEM_HEADS * MEM_HEAD_DIM
    sb_pool_k = cache_sb_k.reshape(-1, cache_sb_k.shape[2] * SB_KV_HEADS, SB_HEAD_DIM)
    sb_pool_v = cache_sb_v.reshape(-1, cache_sb_v.shape[2] * SB_KV_HEADS, SB_HEAD_DIM)
    mem_cache_k = cache_mem_k.reshape(-1, cache_mem_k.shape[2] * MEM_HEADS, MEM_HEAD_DIM)
    mem_cache_v = cache_mem_v.reshape(-1, cache_mem_v.shape[2] * MEM_HEADS, MEM_HEAD_DIM)

    swa_kp, swa_vp, swa_ks, swa_vs = [], [], [], []
    sb_kp, sb_vp, sb_ks, sb_vs = [], [], [], []
    mem_kp, mem_vp = [], []
    moe = None
    for i in range(depth):
        j = i // 2
        first = moe is None
        if i % 2 == 0:
            nq = SWA_HEADS * SWA_HEAD_DIM
            nk = SWA_KV_HEADS * SWA_HEAD_DIM
            res = _norm_matmul(x, norm_mix[i], w_in_swa[j].astype(BF16), moe=moe, rope=rope,
                               rope_cols=nq + nk, emit_x=not first)
            qkv, x = (res, x) if first else res
            o_p = _swa_prompt(qkv, sinks_swa[j], b_p, s_p)
            qs = qkv[t_p:, :nq].reshape(b_s, n_s, SWA_KV_HEADS, SWA_GROUP, SWA_HEAD_DIM)
            qs = qs.transpose(0, 2, 3, 1, 4).reshape(b_s, SWA_KV_HEADS, SWA_GROUP * n_s, 1, SWA_HEAD_DIM)
            own = jnp.eye(SWA_KV_HEADS, dtype=F32)[None, :, None, :, None]
            qs = (qs * own).astype(BF16).reshape(b_s, SWA_HEADS * n_s, nk)
            kn = qkv[t_p:, nq:nq + nk].reshape(b_s, n_s, nk)
            vn = qkv[t_p:, nq + nk:].reshape(b_s, n_s, nk)
            sink_rows = jnp.broadcast_to(jnp.repeat(sinks_swa[j], n_s)[:, None], (SWA_HEADS * n_s, LANES))
            o_s, kbuf, vbuf = _swa_sample(qs, kn, vn, cache_swa_k[j].reshape(b_s, -1, nk),
                                          cache_swa_v[j].reshape(b_s, -1, nk), sink_rows)
            o_s = o_s.reshape(b_s, SWA_KV_HEADS, SWA_GROUP, n_s, SWA_KV_HEADS, SWA_HEAD_DIM)
            o_s = jnp.sum(o_s * own[:, :, :, None].astype(BF16), axis=4)
            o_s = o_s.transpose(0, 3, 1, 2, 4).reshape(t_s, nq)
            last = qkv[:t_p].reshape(b_p, s_p, -1)[:, s_p - WINDOW:, nq:]
            swa_kp.append(last[:, :, :nk].reshape(b_p, WINDOW, SWA_KV_HEADS, SWA_HEAD_DIM))
            swa_vp.append(last[:, :, nk:].reshape(b_p, WINDOW, SWA_KV_HEADS, SWA_HEAD_DIM))
            swa_ks.append(kbuf.reshape(b_s, -1, SWA_KV_HEADS, SWA_HEAD_DIM))
            swa_vs.append(vbuf.reshape(b_s, -1, SWA_KV_HEADS, SWA_HEAD_DIM))
            w_out = w_out_swa[j]
        else:
            nq = SB_HEADS * SB_HEAD_DIM
            nk = SB_KV_HEADS * SB_HEAD_DIM
            res = _norm_matmul(x, norm_mix[i], w_in_sb[j].astype(BF16), moe=moe, emit_x=not first)
            qkv, x = (res, x) if first else res
            o_p = _sb_prompt(qkv, sb_bias[j], b_p, s_p)
            qs = qkv[t_p:, :nq].reshape(b_s, n_s, SB_KV_HEADS, SB_GROUP, SB_HEAD_DIM)
            qs = (qs.transpose(0, 2, 3, 1, 4).reshape(b_s, SB_HEADS * n_s, SB_HEAD_DIM) * SB_SCALE).astype(BF16)
            kn = qkv[t_p:, nq:nq + nk].reshape(b_s, n_s * SB_KV_HEADS, SB_HEAD_DIM)
            vn = qkv[t_p:, nq + nk:].reshape(b_s, n_s * SB_KV_HEADS, SB_HEAD_DIM)
            bias_rows = jnp.broadcast_to(jnp.repeat(sb_bias[j], n_s)[:, None], (SB_HEADS * n_s, LANES))
            o_s = _sb_sample(qs, kn, vn, sb_pool_k, sb_pool_v, page_table, bias_rows,
                             j * cache_sb_k.shape[1])
            o_s = o_s.reshape(b_s, SB_KV_HEADS, SB_GROUP, n_s, SB_HEAD_DIM)
            o_s = o_s.transpose(0, 3, 1, 2, 4).reshape(t_s, nq)
            sb_kp.append(qkv[:t_p, nq:nq + nk].reshape(b_p, s_p, SB_KV_HEADS, SB_HEAD_DIM))
            sb_vp.append(qkv[:t_p, nq + nk:].reshape(b_p, s_p, SB_KV_HEADS, SB_HEAD_DIM))
            sb_ks.append(kn.reshape(b_s, n_s, SB_KV_HEADS, SB_HEAD_DIM))
            sb_vs.append(vn.reshape(b_s, n_s, SB_KV_HEADS, SB_HEAD_DIM))
            w_out = w_out_sb[j]
        x = _resid_matmul((o_p, o_s), w_out.astype(BF16), x)

        mkv = _norm_matmul(mem_flat, norm_mem_kv[i], w_mem_kv[i].astype(BF16))
        mem_kp.append(mkv[:, :mem_w].reshape(b_p, mem_m, MEM_HEADS, MEM_HEAD_DIM))
        mem_vp.append(mkv[:, mem_w:].reshape(b_p, mem_m, MEM_HEADS, MEM_HEAD_DIM))
        qm = _norm_matmul(x, norm_mem_q[i], w_mem_q[i].astype(BF16), out_dtype=BF16)
        om_p = _mem_prompt(qm, mkv, b_p, s_p)
        row_pad = 16 - n_s % 16 if n_s % 16 else 0
        qm_s = jnp.pad(qm[t_p:].reshape(b_s, n_s, MEM_HEADS, MEM_HEAD_DIM), ((0, 0), (0, row_pad), (0, 0), (0, 0)))
        qm_s = qm_s.transpose(0, 2, 1, 3).reshape(b_s, MEM_HEADS * (n_s + row_pad), MEM_HEAD_DIM)
        om_s = _mem_sample(qm_s, mem_cache_k, mem_cache_v, i)
        om_s = om_s.reshape(b_s, MEM_HEADS, n_s + row_pad, MEM_HEAD_DIM)[:, :, :n_s].transpose(0, 2, 1, 3)
        x = _resid_matmul((om_p, om_s.reshape(t_s, mem_w)), w_mem_o[i].astype(BF16), x)

        h, info = _router(x, norm_ffn[i], w_router[i], b_router[i], w_group[i], b_group[i])
        moe = (_moe(h, info, w_gate, w_up, w_down, i), info)

    y_prompt = _final_norm(x, moe[0], moe[1], norm_final, 0, t_p).reshape(b_p, s_p, d)
    y_sample = _final_norm(x, moe[0], moe[1], norm_final, t_p, t_s).reshape(b_s, n_s, d)
    return (y_prompt, y_sample,
            jnp.stack(swa_kp), jnp.stack(swa_vp), jnp.stack(swa_ks), jnp.stack(swa_vs),
            jnp.stack(sb_kp), jnp.stack(sb_vp), jnp.stack(sb_ks), jnp.stack(sb_vs),
            jnp.stack(mem_kp), jnp.stack(mem_vp))
```
